```python
import math
import jax, jax.numpy as jnp
from jax import lax
import numpy as np

D_MODEL = 4096
BATCH = 2
SEQ = 4096
DEPTH = 2

RET_HEADS = 8
RET_QK_DIM = 64
RET_V_DIM = 128
RET_CHUNK = 128
RET_ROPE_THETA = 10000.0

DSA_HEADS = 12
DSA_KV_HEADS = 4
DSA_HEAD_DIM = 128
IDX_HEADS = 32
IDX_HEAD_DIM = 128
DSA_TOPK = 256
PARTIAL_ROT_DIM = DSA_HEAD_DIM // 4
IDX_ROT_DIM = IDX_HEAD_DIM // 4
PARTIAL_ROPE_THETA = 500000.0

MLA_HEADS = 12
MLA_Q_LORA = 768
MLA_KV_LORA = 256
MLA_NOPE_DIM = 128
MLA_ROPE_DIM = 64
MLA_V_DIM = 128
MLA_ROPE_THETA = 10000.0

Q_BLOCK = 128
D_FF = 4 * D_MODEL
NORM_EPS = 1e-6

RET_OUT = RET_HEADS * RET_V_DIM
DSA_OUT = DSA_HEADS * DSA_HEAD_DIM
MLA_OUT = MLA_HEADS * MLA_V_DIM
MIX_WIDTH = RET_OUT + DSA_OUT + MLA_OUT

IN_SPLITS = (
    RET_HEADS * RET_QK_DIM,
    RET_HEADS * RET_QK_DIM,
    RET_OUT,
    RET_OUT,
    DSA_HEADS * DSA_HEAD_DIM,
    DSA_KV_HEADS * DSA_HEAD_DIM,
    DSA_KV_HEADS * DSA_HEAD_DIM,
    IDX_HEADS * IDX_HEAD_DIM,
    IDX_HEAD_DIM,
    IDX_HEADS,
    MLA_Q_LORA,
    MLA_KV_LORA,
    MLA_ROPE_DIM,
)
IN_WIDTH = sum(IN_SPLITS)

kernel_name = "hybrid_retention_dsa_mla_block"


def rms_norm(x, g):
    xf = x.astype(jnp.float32)
    y = xf * lax.rsqrt(jnp.mean(xf * xf, axis=-1, keepdims=True) + NORM_EPS)
    return (y * g.astype(jnp.float32)).astype(x.dtype)


def rope(x, pos, theta, rot_dim):
    half = rot_dim // 2
    inv = jnp.exp(-math.log(theta) * jnp.arange(half, dtype=jnp.float32) * (2.0 / rot_dim))
    ang = pos[:, None] * inv[None, :]
    cos = jnp.cos(ang)[None, :, None, :]
    sin = jnp.sin(ang)[None, :, None, :]
    xr = x[..., :rot_dim].astype(jnp.float32)
    x1, x2 = xr[..., :half], xr[..., half:]
    rot = jnp.concatenate([x1 * cos - x2 * sin, x2 * cos + x1 * sin], axis=-1).astype(x.dtype)
    return jnp.concatenate([rot, x[..., rot_dim:]], axis=-1)


def split_points():
    pts, acc = [], 0
    for s in IN_SPLITS[:-1]:
        acc += s
        pts.append(acc)
    return pts


def to_query_blocks(a):
    B, T = a.shape[0], a.shape[1]
    return jnp.moveaxis(a.reshape(B, T // Q_BLOCK, Q_BLOCK, *a.shape[2:]), 1, 0)


def retention(q, k, v, gate):
    f32 = jnp.float32
    B, T, H, dk = q.shape
    dv = v.shape[-1]
    C = RET_CHUNK
    N = T // C
    pos = jnp.arange(T, dtype=f32)
    q = rope(q, pos, RET_ROPE_THETA, dk).astype(f32)
    k = rope(k, pos, RET_ROPE_THETA, dk).astype(f32) * (dk ** -0.5)
    v = v.astype(f32)
    log_gamma = jnp.log1p(-jnp.exp2(-5.0 - jnp.arange(H, dtype=f32)))
    idx = jnp.arange(C, dtype=f32)
    rel = idx[:, None] - idx[None, :]
    inner_decay = jnp.where(rel[None] >= 0,
                            jnp.exp(jnp.maximum(rel, 0.0)[None] * log_gamma[:, None, None]),
                            0.0)
    q_decay = jnp.exp((idx + 1.0)[None, :] * log_gamma[:, None])
    k_decay = jnp.exp((C - 1.0 - idx)[None, :] * log_gamma[:, None])
    chunk_decay = jnp.exp(C * log_gamma)

    def to_chunks(a):
        return a.reshape(B, N, C, H, a.shape[-1]).transpose(1, 0, 3, 2, 4)

    def step(state, inp):
        qi, ki, vi = inp
        scores = jnp.einsum('bhid,bhjd->bhij', qi, ki) * inner_decay
        inner = jnp.einsum('bhij,bhjv->bhiv', scores, vi)
        cross = jnp.einsum('bhid,bhdv->bhiv', qi * q_decay[None, :, :, None], state)
        new_state = state * chunk_decay[None, :, None, None] + jnp.einsum(
            'bhjd,bhjv->bhdv', ki * k_decay[None, :, :, None], vi)
        return new_state, inner + cross

    state0 = jnp.zeros((B, H, dk, dv), f32)
    _, out = lax.scan(step, state0, (to_chunks(q), to_chunks(k), to_chunks(v)))
    out = out.transpose(1, 0, 3, 2, 4).reshape(B, T, H, dv)
    out = out * lax.rsqrt(jnp.mean(out * out, axis=-1, keepdims=True) + NORM_EPS)
    out = out.reshape(B, T, H * dv)
    return (jax.nn.silu(gate.astype(f32)) * out).astype(gate.dtype)


def dsa_attention(q, k, v, q_idx, k_idx, w_idx):
    f32 = jnp.float32
    B, T, H, d = q.shape
    Hkv = k.shape[2]
    G = H // Hkv
    top_k = min(DSA_TOPK, T // 4)
    pos = jnp.arange(T, dtype=f32)
    q = rope(q, pos, PARTIAL_ROPE_THETA, PARTIAL_ROT_DIM)
    k = rope(k, pos, PARTIAL_ROPE_THETA, PARTIAL_ROT_DIM)
    q_idx = rope(q_idx, pos, PARTIAL_ROPE_THETA, IDX_ROT_DIM)
    k_idx = rope(k_idx[:, :, None, :], pos, PARTIAL_ROPE_THETA, IDX_ROT_DIM)[:, :, 0, :]
    w_idx = w_idx.astype(f32) * (IDX_HEADS ** -0.5)
    key_pos = jnp.arange(T)

    def one_block(args):
        qb, qib, wb, start = args
        qpos = start + jnp.arange(Q_BLOCK)
        causal = key_pos[None, :] <= qpos[:, None]
        logits = jnp.einsum('bqhd,bsd->bqhs', qib, k_idx).astype(f32) * (IDX_HEAD_DIM ** -0.5)
        index_score = jnp.einsum('bqhs,bqh->bqs', jax.nn.relu(logits), wb)
        index_score = jnp.where(causal[None], index_score, -jnp.inf)
        _, sel = lax.top_k(index_score, top_k)
        valid = sel <= qpos[None, :, None]
        k_sel = jax.vmap(lambda kb, ib: kb[ib])(k, sel)
        v_sel = jax.vmap(lambda vb, ib: vb[ib])(v, sel)
        qg = qb.reshape(B, Q_BLOCK, Hkv, G, d)
        s = jnp.einsum('bqngd,bqknd->bqngk', qg, k_sel).astype(f32) * (d ** -0.5)
        s = jnp.where(valid[:, :, None, None, :], s, -jnp.inf)
        p = jax.nn.softmax(s, axis=-1).astype(v.dtype)
        o = jnp.einsum('bqngk,bqknd->bqngd', p, v_sel)
        return o.reshape(B, Q_BLOCK, H * d)

    starts = jnp.arange(T // Q_BLOCK) * Q_BLOCK
    out = lax.map(one_block, (to_query_blocks(q), to_query_blocks(q_idx),
                              to_query_blocks(w_idx), starts))
    return jnp.moveaxis(out, 0, 1).reshape(B, T, H * d)


def mla_attention(c_q, c_kv, k_rope, q_norm, kv_norm, w_uq, w_ukv):
    f32 = jnp.float32
    B, T, _ = c_q.shape
    H = MLA_HEADS
    pos = jnp.arange(T, dtype=f32)
    q = (rms_norm(c_q, q_norm) @ w_uq).reshape(B, T, H, MLA_NOPE_DIM + MLA_ROPE_DIM)
    q_nope, q_pe = q[..., :MLA_NOPE_DIM], q[..., MLA_NOPE_DIM:]
    q_pe = rope(q_pe, pos, MLA_ROPE_THETA, MLA_ROPE_DIM)
    kv = (rms_norm(c_kv, kv_norm) @ w_ukv).reshape(B, T, H, MLA_NOPE_DIM + MLA_V_DIM)
    k_nope, v = kv[..., :MLA_NOPE_DIM], kv[..., MLA_NOPE_DIM:]
    k_pe = rope(k_rope[:, :, None, :], pos, MLA_ROPE_THETA, MLA_ROPE_DIM)
    q_full = jnp.concatenate([q_nope, q_pe], axis=-1)
    k_full = jnp.concatenate([k_nope, jnp.broadcast_to(k_pe, (B, T, H, MLA_ROPE_DIM))], axis=-1)
    scale = (MLA_NOPE_DIM + MLA_ROPE_DIM) ** -0.5
    key_pos = jnp.arange(T)

    def one_block(args):
        qb, start = args
        qpos = start + jnp.arange(Q_BLOCK)
        causal = key_pos[None, :] <= qpos[:, None]
        s = jnp.einsum('bqhd,bshd->bhqs', qb, k_full).astype(f32) * scale
        s = jnp.where(causal[None, None], s, -jnp.inf)
        p = jax.nn.softmax(s, axis=-1).astype(v.dtype)
        o = jnp.einsum('bhqs,bshv->bqhv', p, v)
        return o.reshape(B, Q_BLOCK, H * MLA_V_DIM)

    starts = jnp.arange(T // Q_BLOCK) * Q_BLOCK
    out = lax.map(one_block, (to_query_blocks(q_full), starts))
    return jnp.moveaxis(out, 0, 1).reshape(B, T, H * MLA_V_DIM)


def setup_inputs(seed: int = 0) -> dict:
    key = jax.random.key(seed)
    ks = jax.random.split(key, 16)
    f32 = jnp.float32
    nrm = lambda k, shape, fan_in: jax.random.normal(k, shape, f32) * (fan_in ** -0.5)
    gain = lambda k, shape: 1.0 + 0.02 * jax.random.normal(k, shape, f32)
    return {
        "x": jax.random.normal(ks[0], (BATCH, SEQ, D_MODEL), f32),
        "attn_norm": gain(ks[1], (DEPTH, D_MODEL)),
        "w_in": nrm(ks[2], (DEPTH, D_MODEL, IN_WIDTH), D_MODEL),
        "mla_q_norm": gain(ks[3], (DEPTH, MLA_Q_LORA)),
        "mla_kv_norm": gain(ks[4], (DEPTH, MLA_KV_LORA)),
        "w_uq": nrm(ks[5], (DEPTH, MLA_Q_LORA, MLA_HEADS * (MLA_NOPE_DIM + MLA_ROPE_DIM)), MLA_Q_LORA),
        "w_ukv": nrm(ks[6], (DEPTH, MLA_KV_LORA, MLA_HEADS * (MLA_NOPE_DIM + MLA_V_DIM)), MLA_KV_LORA),
        "w_o": nrm(ks[7], (DEPTH, MIX_WIDTH, D_MODEL), MIX_WIDTH),
        "mlp_norm": gain(ks[8], (DEPTH, D_MODEL)),
        "w_up": nrm(ks[9], (DEPTH, D_MODEL, D_FF), D_MODEL),
        "w_down": nrm(ks[10], (DEPTH, D_FF, D_MODEL), D_FF),
        "final_norm": gain(ks[11], (D_MODEL,)),
    }


def reference(x, attn_norm, w_in, mla_q_norm, mla_kv_norm, w_uq, w_ukv, w_o,
              mlp_norm, w_up, w_down, final_norm):
    B, T, _ = x.shape
    pts = split_points()
    h = x
    for l in range(DEPTH):
        u = rms_norm(h, attn_norm[l])
        proj = u @ w_in[l]
        (rq, rk, rv, rg, dq, dk, dv, iq, ik, iw, cq, ckv, kr) = jnp.split(proj, pts, axis=-1)
        ret = retention(rq.reshape(B, T, RET_HEADS, RET_QK_DIM),
                        rk.reshape(B, T, RET_HEADS, RET_QK_DIM),
                        rv.reshape(B, T, RET_HEADS, RET_V_DIM), rg)
        dsa = dsa_attention(dq.reshape(B, T, DSA_HEADS, DSA_HEAD_DIM),
                            dk.reshape(B, T, DSA_KV_HEADS, DSA_HEAD_DIM),
                            dv.reshape(B, T, DSA_KV_HEADS, DSA_HEAD_DIM),
                            iq.reshape(B, T, IDX_HEADS, IDX_HEAD_DIM), ik, iw)
        mla = mla_attention(cq, ckv, kr, mla_q_norm[l], mla_kv_norm[l], w_uq[l], w_ukv[l])
        mixed = jnp.concatenate([ret, dsa, mla], axis=-1)
        h = h + mixed @ w_o[l]
        u = rms_norm(h, mlp_norm[l])
        h = h + jnp.square(jax.nn.relu(u @ w_up[l])) @ w_down[l]
    return rms_norm(h, final_norm)
```

```python
import functools
import math

import numpy as np
import jax
import jax.numpy as jnp
from jax import lax
from jax.experimental import pallas as pl
from jax.experimental.pallas import tpu as pltpu

F32 = jnp.float32
BF16 = jnp.bfloat16

RET_HEADS = 8
RET_QK_DIM = 64
RET_V_DIM = 128
RET_CHUNK = 128
RET_ROPE_THETA = 10000.0
DSA_HEADS = 12
DSA_KV_HEADS = 4
DSA_HEAD_DIM = 128
IDX_HEADS = 32
IDX_HEAD_DIM = 128
DSA_TOPK = 256
PARTIAL_ROT_DIM = 32
PARTIAL_ROPE_THETA = 500000.0
MLA_HEADS = 12
MLA_Q_LORA = 768
MLA_KV_LORA = 256
MLA_NOPE_DIM = 128
MLA_ROPE_DIM = 64
MLA_V_DIM = 128
MLA_ROPE_THETA = 10000.0
NORM_EPS = 1e-6

LANE = 128
VMEM_LIMIT = 56 * 1024 * 1024

RET_OUT = RET_HEADS * RET_V_DIM
DSA_OUT = DSA_HEADS * DSA_HEAD_DIM
MLA_OUT = MLA_HEADS * MLA_V_DIM
MLA_QK_PAD = 256

_NAT = {}
_off = 0
for _name, _w in (("rq", 512), ("rk", 512), ("rv", 1024), ("rg", 1024), ("dq", 1536),
                  ("dk", 512), ("dv", 512), ("iq", 4096), ("ik", 128), ("iw", 32),
                  ("cq", 768), ("ckv", 256), ("kr", 64)):
    _NAT[_name] = (_off, _w)
    _off += _w
IN_WIDTH = _off

_SEG = {"iq": (0, 4096), "rq": (4096, 512), "dq": (4608, 1536), "rk": (6144, 512),
        "dk": (6656, 512), "rv": (7168, 1024), "rg": (8192, 1024), "dv": (9216, 512),
        "ckv": (9728, 256), "cq": (9984, 768), "ik": (10752, 128), "iw": (10880, 128),
        "kr": (11008, 128)}
PROJ_WIDTH = 11264


def _cparams(sem):
    return pltpu.CompilerParams(dimension_semantics=sem, vmem_limit_bytes=VMEM_LIMIT)


def _rmsnorm_kernel(x_ref, g_ref, o_ref):
    x = x_ref[...].astype(F32)
    ms = jnp.mean(x * x, axis=-1, keepdims=True)
    o_ref[...] = (x * lax.rsqrt(ms + NORM_EPS) * g_ref[...]).astype(o_ref.dtype)


def rmsnorm(x, g, out_dtype, *, col_off=0, width=None, bm=256):
    m = x.shape[0]
    width = x.shape[1] if width is None else width
    assert col_off % width == 0 and m % bm == 0
    cb = col_off // width
    return pl.pallas_call(
        _rmsnorm_kernel,
        grid=(m // bm,),
        in_specs=[pl.BlockSpec((bm, width), lambda i: (i, cb)),
                  pl.BlockSpec((1, width), lambda i: (0, 0))],
        out_specs=pl.BlockSpec((bm, width), lambda i: (i, 0)),
        out_shape=jax.ShapeDtypeStruct((m, width), out_dtype),
        compiler_params=_cparams(("parallel",)),
    )(x, g.reshape(1, width).astype(F32))


def _mm_kernel(*refs, act, has_res, nk):
    if has_res:
        x_ref, w_ref, r_ref, o_ref = refs[:4]
        scratch = refs[4:]
    else:
        x_ref, w_ref, o_ref = refs[:3]
        r_ref = None
        scratch = refs[3:]

    def epilogue(acc):
        if act == "relu2":
            acc = jnp.square(jnp.maximum(acc, 0.0))
        if has_res:
            acc = acc + r_ref[...]
        o_ref[...] = acc.astype(o_ref.dtype)

    part = jnp.dot(x_ref[...], w_ref[...], preferred_element_type=F32)
    if nk == 1:
        epilogue(part)
    else:
        acc_ref = scratch[0]
        k = pl.program_id(2)

        @pl.when(k == 0)
        def _():
            acc_ref[...] = part

        @pl.when(jnp.logical_and(k > 0, k < nk - 1))
        def _():
            acc_ref[...] += part

        @pl.when(k == nk - 1)
        def _():
            epilogue(acc_ref[...] + part)


def matmul(x, w, out_dtype, *, act=None, res=None, bm=1024, bn=1024, bk=4096):
    m, kdim = x.shape
    n = w.shape[1]
    bm, bn, bk = min(bm, m), min(bn, n), min(bk, kdim)
    assert m % bm == 0 and n % bn == 0 and kdim % bk == 0
    nk = kdim // bk
    in_specs = [pl.BlockSpec((bm, bk), lambda i, j, k: (i, k)),
                pl.BlockSpec((bk, bn), lambda i, j, k: (k, j))]
    args = [x, w]
    if res is not None:
        in_specs.append(pl.BlockSpec((bm, bn), lambda i, j, k: (i, j)))
        args.append(res)
    scratch = [pltpu.VMEM((bm, bn), F32)] if nk > 1 else []
    return pl.pallas_call(
        functools.partial(_mm_kernel, act=act, has_res=res is not None, nk=nk),
        grid=(m // bm, n // bn, nk),
        in_specs=in_specs,
        out_specs=pl.BlockSpec((bm, bn), lambda i, j, k: (i, j)),
        out_shape=jax.ShapeDtypeStruct((m, n), out_dtype),
        scratch_shapes=scratch,
        compiler_params=_cparams(("parallel", "parallel", "arbitrary")),
    )(*args)


def _rope_kernel(x_ref, a_ref, bp_ref, bm_ref, o_ref, *, ntiles, rope_tiles, half, scale):
    a, bp, bm = a_ref[...], bp_ref[...], bm_ref[...]
    for t in range(ntiles):
        x = x_ref[:, t * LANE:(t + 1) * LANE].astype(F32)
        if t in rope_tiles:
            x = x * a + pltpu.roll(x, half, 1) * bp + pltpu.roll(x, LANE - half, 1) * bm
        o_ref[:, t * LANE:(t + 1) * LANE] = (x * scale).astype(o_ref.dtype)


def rope_cast(x, tables, seq, *, col_off, width, rope_tiles, half, scale, bt=256):
    m = x.shape[0]
    assert col_off % width == 0 and width % LANE == 0 and seq % bt == 0 and m % bt == 0
    cb = col_off // width
    nseq = seq // bt
    tab_spec = pl.BlockSpec((bt, LANE), lambda i: (i % nseq, 0))
    return pl.pallas_call(
        functools.partial(_rope_kernel, ntiles=width // LANE, rope_tiles=frozenset(rope_tiles),
                          half=half, scale=scale),
        grid=(m // bt,),
        in_specs=[pl.BlockSpec((bt, width), lambda i: (i, cb)), tab_spec, tab_spec, tab_spec],
        out_specs=pl.BlockSpec((bt, width), lambda i: (i, 0)),
        out_shape=jax.ShapeDtypeStruct((m, width), BF16),
        compiler_params=_cparams(("parallel",)),
    )(x, *tables)


def _rope_tables(seq, theta, rot_dim):
    half = rot_dim // 2
    inv = jnp.exp(-math.log(theta) * jnp.arange(half, dtype=F32) * (2.0 / rot_dim))
    ang = jnp.arange(seq, dtype=F32)[:, None] * inv[None, :]
    cos, sin = jnp.cos(ang), jnp.sin(ang)
    pad = LANE - rot_dim
    a = jnp.concatenate([cos, cos, jnp.ones((seq, pad), F32)], axis=1)
    bp = jnp.concatenate([jnp.zeros((seq, half), F32), sin, jnp.zeros((seq, pad), F32)], axis=1)
    bm = jnp.concatenate([-sin, jnp.zeros((seq, half + pad), F32)], axis=1)
    return a, bp, bm


def _transpose_kernel(x_ref, o_ref, *, scale):
    o_ref[0] = (x_ref[...].astype(F32) * scale).T.astype(o_ref.dtype)


def transpose_tiles(x, out_dtype, *, col_off, width, bt, scale=1.0):
    m = x.shape[0]
    assert col_off % width == 0 and m % bt == 0
    cb = col_off // width
    return pl.pallas_call(
        functools.partial(_transpose_kernel, scale=scale),
        grid=(m // bt,),
        in_specs=[pl.BlockSpec((bt, width), lambda i: (i, cb))],
        out_specs=pl.BlockSpec((1, width, bt), lambda i: (i, 0, 0)),
        out_shape=jax.ShapeDtypeStruct((m // bt, width, bt), out_dtype),
        compiler_params=_cparams(("parallel",)),
    )(x)


def _retention_kernel(q_ref, k_ref, v_ref, g_ref, cos_ref, sin_ref, hmask_ref, idec_ref,
                      qdec_ref, kdec_ref, cdec_ref, bmask_ref, o_ref, state_ref):
    hw = RET_HEADS * RET_QK_DIM // 2

    @pl.when(pl.program_id(1) == 0)
    def _():
        state_ref[...] = jnp.zeros_like(state_ref)

    cos, sin = cos_ref[...], sin_ref[...]

    def rot(x):
        x1, x2 = x[:, :hw], x[:, hw:]
        return jnp.concatenate([x1 * cos - x2 * sin, x2 * cos + x1 * sin], axis=1)

    q = rot(q_ref[...])
    k = rot(k_ref[...]) * (RET_QK_DIM ** -0.5)
    v = v_ref[...]
    kt = k.T.astype(BF16)
    state = state_ref[...]
    cross = jnp.dot(q.astype(BF16), state.astype(BF16), preferred_element_type=F32) * qdec_ref[...]
    for h in range(RET_HEADS):
        cols = slice(h * RET_V_DIM, (h + 1) * RET_V_DIM)
        qm = (q * hmask_ref[h:h + 1, :]).astype(BF16)
        scores = jnp.dot(qm, kt, preferred_element_type=F32) * idec_ref[h]
        inner = jnp.dot(scores.astype(BF16), v[:, cols].astype(BF16), preferred_element_type=F32)
        o = inner + cross[:, cols]
        o = o * lax.rsqrt(jnp.mean(o * o, axis=-1, keepdims=True) + NORM_EPS)
        gate = g_ref[:, cols]
        o_ref[:, cols] = (gate * jax.nn.sigmoid(gate) * o).astype(o_ref.dtype)
    vd = (v * kdec_ref[...]).astype(BF16)
    upd = jnp.dot(kt, vd, preferred_element_type=F32)
    state_ref[...] = state * cdec_ref[...] + upd * bmask_ref[...]


def _retention_tables(seq):
    h, c, dv = RET_HEADS, RET_CHUNK, RET_V_DIM
    half = RET_QK_DIM // 2
    inv = jnp.exp(-math.log(RET_ROPE_THETA) * jnp.arange(half, dtype=F32) * (2.0 / RET_QK_DIM))
    ang = jnp.arange(seq, dtype=F32)[:, None] * inv[None, :]
    cos = jnp.tile(jnp.cos(ang), (1, h))
    sin = jnp.tile(jnp.sin(ang), (1, h))
    log_gamma = jnp.log1p(-jnp.exp2(-5.0 - jnp.arange(h, dtype=F32)))
    idx = jnp.arange(c, dtype=F32)
    rel = idx[:, None] - idx[None, :]
    idec = jnp.where(rel[None] >= 0,
                     jnp.exp(jnp.maximum(rel, 0.0)[None] * log_gamma[:, None, None]), 0.0)
    qdec = jnp.exp((idx + 1.0)[:, None] * log_gamma[None, :])
    kdec = jnp.exp((c - 1.0 - idx)[:, None] * log_gamma[None, :])
    cdec = jnp.exp(c * log_gamma)[None, :]
    rep = lambda a: jnp.repeat(a, dv, axis=1)
    lane_head = (np.arange(2 * h * half) % (h * half)) // half
    hmask = jnp.asarray(lane_head[None, :] == np.arange(h)[:, None], F32)
    bmask = jnp.asarray(lane_head[:, None] == (np.arange(h * dv) // dv)[None, :], F32)
    return cos, sin, hmask, idec, rep(qdec), rep(kdec), rep(cdec), bmask


def retention(proj, batch, seq, tables):
    c = RET_CHUNK
    nch = seq // c
    qw = RET_HEADS * RET_QK_DIM
    col = lambda name: _SEG[name][0] // _SEG[name][1]
    row = lambda b, n: b * nch + n
    const2 = lambda b, n: (0, 0)
    cos, sin, hmask, idec, qdec, kdec, cdec, bmask = tables
    return pl.pallas_call(
        _retention_kernel,
        grid=(batch, nch),
        in_specs=[pl.BlockSpec((c, qw), lambda b, n: (row(b, n), col("rq"))),
                  pl.BlockSpec((c, qw), lambda b, n: (row(b, n), col("rk"))),
                  pl.BlockSpec((c, RET_OUT), lambda b, n: (row(b, n), col("rv"))),
                  pl.BlockSpec((c, RET_OUT), lambda b, n: (row(b, n), col("rg"))),
                  pl.BlockSpec((c, qw // 2), lambda b, n: (n, 0)),
                  pl.BlockSpec((c, qw // 2), lambda b, n: (n, 0)),
                  pl.BlockSpec(hmask.shape, const2),
                  pl.BlockSpec(idec.shape, lambda b, n: (0, 0, 0)),
                  pl.BlockSpec(qdec.shape, const2),
                  pl.BlockSpec(kdec.shape, const2),
                  pl.BlockSpec(cdec.shape, const2),
                  pl.BlockSpec(bmask.shape, const2)],
        out_specs=pl.BlockSpec((c, RET_OUT), lambda b, n: (row(b, n), 0)),
        out_shape=jax.ShapeDtypeStruct((batch * seq, RET_OUT), BF16),
        scratch_shapes=[pltpu.VMEM((qw, RET_OUT), F32)],
        compiler_params=_cparams(("parallel", "arbitrary")),
    )(proj, proj, proj, proj, cos, sin, hmask, idec, qdec, kdec, cdec, bmask)


def _attend(q, load_k, load_vt, load_bias, nchunks, tq, dv):
    neg_inf = -jnp.inf

    def body(c, carry):
        m, l, acc = carry
        s = lax.dot_general(load_k(c), q, (((1,), (1,)), ((), ())),
                            preferred_element_type=F32)
        s = s + load_bias(c)
        m_new = jnp.maximum(m, jnp.max(s, axis=0, keepdims=True))
        m_safe = jnp.where(m_new == neg_inf, 0.0, m_new)
        p = jnp.exp(s - m_safe)
        alpha = jnp.exp(m - m_safe)
        l = alpha * l + jnp.sum(p, axis=0, keepdims=True)
        acc = acc * alpha + jnp.dot(load_vt(c), p.astype(BF16), preferred_element_type=F32)
        return m_new, l, acc

    init = (jnp.full((1, tq), neg_inf, F32), jnp.zeros((1, tq), F32), jnp.zeros((dv, tq), F32))
    _, l, acc = lax.fori_loop(0, nchunks, body, init)
    return (acc / l).T


INT_MIN = -2 ** 31


def _dsa_kernel(qi_ref, wt_ref, kidx_ref, q_ref, k_ref, vt_ref, o_ref, key_ref, bias_ref,
                *, tq, topk):
    tk = tq
    i = pl.program_id(1)
    nchunks = i + 1
    q0 = i * tq
    sub = LANE
    qpos = q0 + lax.broadcasted_iota(jnp.int32, (sub, tq), 1)
    row_iota = lax.broadcasted_iota(jnp.int32, (sub, tq), 0)

    def score_body(c, _):
        for s in range(tk // sub):
            r0 = pl.multiple_of(c * tk + s * sub, sub)
            kc = kidx_ref[pl.ds(r0, sub), :]
            acc = jnp.zeros((sub, tq), F32)
            for h in range(IDX_HEADS):
                lg = lax.dot_general(kc, qi_ref[:, h * LANE:(h + 1) * LANE],
                                     (((1,), (1,)), ((), ())), preferred_element_type=F32)
                acc = acc + jnp.maximum(lg, 0.0) * wt_ref[0, h:h + 1, :]
            bits = pltpu.bitcast(acc, jnp.int32)
            key = bits ^ (lax.shift_right_arithmetic(bits, 31) & 0x7FFFFFFF)
            causal = (r0 + row_iota) <= qpos
            key_ref[pl.ds(r0, sub), :] = jnp.where(causal, key, INT_MIN)
        return 0

    lax.fori_loop(0, nchunks, score_body, 0)

    def count_ge(cand):
        def body(c, cnt):
            r0 = pl.multiple_of(c * tk, tk)
            ge = (key_ref[pl.ds(r0, tk), :] >= cand).astype(jnp.int32)
            return cnt + jnp.sum(ge.reshape(tk // 8, 8, tq), axis=0)
        cnt = lax.fori_loop(0, nchunks, body, jnp.zeros((8, tq), jnp.int32))
        return jnp.sum(cnt, axis=0, keepdims=True)

    zero = jnp.zeros((1, tq), jnp.int32)
    t0 = jnp.where(count_ge(zero) >= topk, zero, INT_MIN)

    def bit_body(b, t):
        cand = t | lax.shift_left(jnp.int32(1), 30 - b)
        return jnp.where(count_ge(cand) >= topk, cand, t)

    thr = lax.fori_loop(0, 31, bit_body, t0)
    thr = jnp.maximum(thr, INT_MIN + 1)

    def bias_body(c, _):
        r0 = pl.multiple_of(c * tk, tk)
        sel = key_ref[pl.ds(r0, tk), :] >= thr
        bias_ref[pl.ds(r0, tk), :] = jnp.where(sel, 0.0, -jnp.inf)
        return 0

    lax.fori_loop(0, nchunks, bias_body, 0)

    group = DSA_HEADS // DSA_KV_HEADS
    d = DSA_HEAD_DIM
    load_bias = lambda c: bias_ref[pl.ds(pl.multiple_of(c * tk, tk), tk), :]
    for h in range(DSA_HEADS):
        g = h // group
        load_k = lambda c, g=g: k_ref[pl.ds(pl.multiple_of(c * tk, tk), tk), g * d:(g + 1) * d]
        load_vt = lambda c, g=g: vt_ref[c, g * d:(g + 1) * d, :]
        o = _attend(q_ref[:, h * d:(h + 1) * d], load_k, load_vt, load_bias, nchunks, tq, d)
        o_ref[:, h * d:(h + 1) * d] = o.astype(o_ref.dtype)


def dsa_attention(qi, wt, kidx, q, k, vt, batch, seq, *, tq=256):
    nq = seq // tq
    topk = min(DSA_TOPK, seq // 4)
    row = lambda b, i: (b * nq + i, 0)
    return pl.pallas_call(
        functools.partial(_dsa_kernel, tq=tq, topk=topk),
        grid=(batch, nq),
        in_specs=[pl.BlockSpec((tq, IDX_HEADS * IDX_HEAD_DIM), row),
                  pl.BlockSpec((1, LANE, tq), lambda b, i: (b * nq + i, 0, 0)),
                  pl.BlockSpec((seq, IDX_HEAD_DIM), lambda b, i: (b, 0)),
                  pl.BlockSpec((tq, DSA_OUT), row),
                  pl.BlockSpec((seq, DSA_KV_HEADS * DSA_HEAD_DIM), lambda b, i: (b, 0)),
                  pl.BlockSpec((nq, DSA_KV_HEADS * DSA_HEAD_DIM, tq), lambda b, i: (b, 0, 0))],
        out_specs=pl.BlockSpec((tq, DSA_OUT), row),
        out_shape=jax.ShapeDtypeStruct((batch * seq, DSA_OUT), BF16),
        scratch_shapes=[pltpu.VMEM((seq, tq), jnp.int32), pltpu.VMEM((seq, tq), F32)],
        compiler_params=_cparams(("parallel", "arbitrary")),
    )(qi, wt, kidx, q, k, vt)


def _mla_kernel(q_ref, kn_ref, kpe_ref, vt_ref, o_ref, *, tq):
    tk = tq
    i = pl.program_id(2)
    qpos = i * tq + lax.broadcasted_iota(jnp.int32, (tk, tq), 1)
    kpos0 = lax.broadcasted_iota(jnp.int32, (tk, tq), 0)

    def load_k(c):
        r0 = pl.multiple_of(c * tk, tk)
        return jnp.concatenate([kn_ref[pl.ds(r0, tk), :], kpe_ref[pl.ds(r0, tk), :]], axis=1)

    load_vt = lambda c: vt_ref[c]
    load_bias = lambda c: jnp.where(c * tk + kpos0 <= qpos, 0.0, -jnp.inf)
    o = _attend(q_ref[...], load_k, load_vt, load_bias, i + 1, tq, MLA_V_DIM)
    o_ref[...] = o.astype(o_ref.dtype)


def mla_attention(q, kn, kpe, vt, batch, seq, *, tq=256):
    nq = seq // tq
    return pl.pallas_call(
        functools.partial(_mla_kernel, tq=tq),
        grid=(batch, MLA_HEADS, nq),
        in_specs=[pl.BlockSpec((tq, MLA_QK_PAD), lambda b, h, i: (b * nq + i, h)),
                  pl.BlockSpec((seq, MLA_NOPE_DIM), lambda b, h, i: (b, h)),
                  pl.BlockSpec((seq, LANE), lambda b, h, i: (b, 0)),
                  pl.BlockSpec((nq, MLA_V_DIM, tq), lambda b, h, i: (b, h, 0))],
        out_specs=pl.BlockSpec((tq, MLA_V_DIM), lambda b, h, i: (b * nq + i, h)),
        out_shape=jax.ShapeDtypeStruct((batch * seq, MLA_OUT), BF16),
        compiler_params=_cparams(("parallel", "parallel", "arbitrary")),
    )(q, kn, kpe, vt)


def _relayout_w_in(w):
    d = w.shape[0]
    seg = lambda name: w[:, _NAT[name][0]:_NAT[name][0] + _NAT[name][1]]
    halves = lambda a: a.reshape(d, RET_HEADS, 2, RET_QK_DIM // 2).transpose(0, 2, 1, 3).reshape(d, -1)
    zeros = lambda n: jnp.zeros((d, n), w.dtype)
    order = sorted(_SEG, key=lambda s: _SEG[s][0])
    parts, pos = [], 0
    for name in order:
        off, width = _SEG[name]
        assert off == pos
        a = halves(seg(name)) if name in ("rq", "rk") else seg(name)
        parts.append(a)
        if a.shape[1] < width:
            parts.append(zeros(width - a.shape[1]))
        pos += width
    parts.append(zeros(PROJ_WIDTH - pos))
    return jnp.concatenate(parts, axis=1).astype(BF16)


def _relayout_w_uq(w):
    r = w.shape[0]
    a = w.reshape(r, MLA_HEADS, MLA_NOPE_DIM + MLA_ROPE_DIM)
    a = jnp.pad(a, ((0, 0), (0, 0), (0, MLA_QK_PAD - a.shape[2])))
    return a.reshape(r, MLA_HEADS * MLA_QK_PAD).astype(BF16)


def _relayout_w_ukv(w):
    r = w.shape[0]
    a = w.reshape(r, MLA_HEADS, 2, MLA_NOPE_DIM).transpose(0, 2, 1, 3)
    return a.reshape(r, 2 * MLA_HEADS * MLA_NOPE_DIM).astype(BF16)


def kernel(x, attn_norm, w_in, mla_q_norm, mla_kv_norm, w_uq, w_ukv, w_o, mlp_norm, w_up,
           w_down, final_norm):
    batch, seq, d_model = x.shape
    depth = w_in.shape[0]
    m = batch * seq
    tq = 256
    assert seq % tq == 0

    ret_tables = _retention_tables(seq)
    part_tables = _rope_tables(seq, PARTIAL_ROPE_THETA, PARTIAL_ROT_DIM)
    mla_tables = _rope_tables(seq, MLA_ROPE_THETA, MLA_ROPE_DIM)
    part_half, mla_half = PARTIAL_ROT_DIM // 2, MLA_ROPE_DIM // 2
    mla_scale = (MLA_NOPE_DIM + MLA_ROPE_DIM) ** -0.5

    def seg_rope(proj, name, rope_tiles, half, tables, scale):
        off, width = _SEG[name]
        return rope_cast(proj, tables, seq, col_off=off, width=width, rope_tiles=rope_tiles,
                         half=half, scale=scale)

    h = x.reshape(m, d_model)
    for l in range(depth):
        u = rmsnorm(h, attn_norm[l], BF16)
        proj = matmul(u, _relayout_w_in(w_in[l]), F32)

        ret = retention(proj, batch, seq, ret_tables)

        qi = seg_rope(proj, "iq", range(IDX_HEADS), part_half, part_tables, IDX_HEAD_DIM ** -0.5)
        kidx = seg_rope(proj, "ik", (0,), part_half, part_tables, 1.0)
        dq = seg_rope(proj, "dq", range(DSA_HEADS), part_half, part_tables, DSA_HEAD_DIM ** -0.5)
        dk = seg_rope(proj, "dk", range(DSA_KV_HEADS), part_half, part_tables, 1.0)
        wt = transpose_tiles(proj, F32, col_off=_SEG["iw"][0], width=LANE, bt=tq,
                             scale=IDX_HEADS ** -0.5)
        dvt = transpose_tiles(proj, BF16, col_off=_SEG["dv"][0], width=_SEG["dv"][1], bt=tq)
        dsa = dsa_attention(qi, wt, kidx, dq, dk, dvt, batch, seq, tq=tq)

        cq = rmsnorm(proj, mla_q_norm[l], BF16, col_off=_SEG["cq"][0], width=MLA_Q_LORA)
        ckv = rmsnorm(proj, mla_kv_norm[l], BF16, col_off=_SEG["ckv"][0], width=MLA_KV_LORA)
        q_up = matmul(cq, _relayout_w_uq(w_uq[l]), F32)
        kv_up = matmul(ckv, _relayout_w_ukv(w_ukv[l]), F32)
        q_full = rope_cast(q_up, mla_tables, seq, col_off=0, width=q_up.shape[1],
                           rope_tiles=range(1, 2 * MLA_HEADS, 2), half=mla_half, scale=mla_scale)
        kn = rope_cast(kv_up, mla_tables, seq, col_off=0, width=MLA_OUT, rope_tiles=(),
                       half=mla_half, scale=1.0)
        kpe = seg_rope(proj, "kr", (0,), mla_half, mla_tables, 1.0)
        vt = transpose_tiles(kv_up, BF16, col_off=MLA_OUT, width=MLA_OUT, bt=tq)
        mla = mla_attention(q_full, kn, kpe, vt, batch, seq, tq=tq)

        mixed = jnp.concatenate([ret, dsa, mla], axis=1)
        h = matmul(mixed, w_o[l].astype(BF16), F32, res=h, bn=512)
        u = rmsnorm(h, mlp_norm[l], BF16)
        up = matmul(u, w_up[l].astype(BF16), BF16, act="relu2")
        h = matmul(up, w_down[l].astype(BF16), F32, res=h, bn=512)
    out = rmsnorm(h, final_norm, x.dtype)
    return out.reshape(batch, seq, d_model)
```

```python
import functools
import math

import numpy as np
import jax
import jax.numpy as jnp
from jax import lax
from jax.experimental import pallas as pl
from jax.experimental.pallas import tpu as pltpu

F32 = jnp.float32
BF16 = jnp.bfloat16

RET_HEADS = 8
RET_QK_DIM = 64
RET_V_DIM = 128
RET_CHUNK = 128
RET_ROPE_THETA = 10000.0
DSA_HEADS = 12
DSA_KV_HEADS = 4
DSA_HEAD_DIM = 128
IDX_HEADS = 32
IDX_HEAD_DIM = 128
DSA_TOPK = 256
PARTIAL_ROT_DIM = 32
PARTIAL_ROPE_THETA = 500000.0
MLA_HEADS = 12
MLA_Q_LORA = 768
MLA_KV_LORA = 256
MLA_NOPE_DIM = 128
MLA_ROPE_DIM = 64
MLA_V_DIM = 128
MLA_ROPE_THETA = 10000.0
NORM_EPS = 1e-6

LANE = 128
VMEM_LIMIT = 56 * 1024 * 1024

RET_OUT = RET_HEADS * RET_V_DIM
DSA_OUT = DSA_HEADS * DSA_HEAD_DIM
MLA_OUT = MLA_HEADS * MLA_V_DIM
MLA_QK_PAD = 256

_NAT = {}
_off = 0
for _name, _w in (("rq", 512), ("rk", 512), ("rv", 1024), ("rg", 1024), ("dq", 1536),
                  ("dk", 512), ("dv", 512), ("iq", 4096), ("ik", 128), ("iw", 32),
                  ("cq", 768), ("ckv", 256), ("kr", 64)):
    _NAT[_name] = (_off, _w)
    _off += _w
IN_WIDTH = _off

_SEG = {"iq": (0, 4096), "rq": (4096, 512), "dq": (4608, 1536), "rk": (6144, 512),
        "dk": (6656, 512), "rv": (7168, 1024), "rg": (8192, 1024), "dv": (9216, 512),
        "ckv": (9728, 256), "cq": (9984, 768), "ik": (10752, 128), "iw": (10880, 128),
        "kr": (11008, 128)}
PROJ_WIDTH = 11264


def _cparams(sem):
    return pltpu.CompilerParams(dimension_semantics=sem, vmem_limit_bytes=VMEM_LIMIT)


def _rmsnorm_kernel(x_ref, g_ref, o_ref):
    x = x_ref[...].astype(F32)
    ms = jnp.mean(x * x, axis=-1, keepdims=True)
    o_ref[...] = (x * lax.rsqrt(ms + NORM_EPS) * g_ref[...]).astype(o_ref.dtype)


def rmsnorm(x, g, out_dtype, *, col_off=0, width=None, bm=256):
    m = x.shape[0]
    width = x.shape[1] if width is None else width
    assert col_off % width == 0 and m % bm == 0
    cb = col_off // width
    return pl.pallas_call(
        _rmsnorm_kernel,
        grid=(m // bm,),
        in_specs=[pl.BlockSpec((bm, width), lambda i: (i, cb)),
                  pl.BlockSpec((1, width), lambda i: (0, 0))],
        out_specs=pl.BlockSpec((bm, width), lambda i: (i, 0)),
        out_shape=jax.ShapeDtypeStruct((m, width), out_dtype),
        compiler_params=_cparams(("parallel",)),
        name="rmsnorm",
    )(x, g.reshape(1, width).astype(F32))


def _mm_kernel(*refs, act, has_res, nk):
    if has_res:
        x_ref, w_ref, r_ref, o_ref = refs[:4]
        scratch = refs[4:]
    else:
        x_ref, w_ref, o_ref = refs[:3]
        r_ref = None
        scratch = refs[3:]

    def epilogue(acc):
        if act == "relu2":
            acc = jnp.square(jnp.maximum(acc, 0.0))
        if has_res:
            acc = acc + r_ref[...]
        o_ref[...] = acc.astype(o_ref.dtype)

    part = jnp.dot(x_ref[...], w_ref[...], preferred_element_type=F32)
    if nk == 1:
        epilogue(part)
    else:
        acc_ref = scratch[0]
        k = pl.program_id(2)

        @pl.when(k == 0)
        def _():
            acc_ref[...] = part

        @pl.when(jnp.logical_and(k > 0, k < nk - 1))
        def _():
            acc_ref[...] += part

        @pl.when(k == nk - 1)
        def _():
            epilogue(acc_ref[...] + part)


def matmul(x, w, out_dtype, *, name, act=None, res=None, bm=1024, bn=1024, bk=4096):
    m, kdim = x.shape
    n = w.shape[1]
    bm, bn, bk = min(bm, m), min(bn, n), min(bk, kdim)
    assert m % bm == 0 and n % bn == 0 and kdim % bk == 0
    nk = kdim // bk
    in_specs = [pl.BlockSpec((bm, bk), lambda i, j, k: (i, k)),
                pl.BlockSpec((bk, bn), lambda i, j, k: (k, j))]
    args = [x, w]
    if res is not None:
        in_specs.append(pl.BlockSpec((bm, bn), lambda i, j, k: (i, j)))
        args.append(res)
    scratch = [pltpu.VMEM((bm, bn), F32)] if nk > 1 else []
    return pl.pallas_call(
        functools.partial(_mm_kernel, act=act, has_res=res is not None, nk=nk),
        grid=(m // bm, n // bn, nk),
        in_specs=in_specs,
        out_specs=pl.BlockSpec((bm, bn), lambda i, j, k: (i, j)),
        out_shape=jax.ShapeDtypeStruct((m, n), out_dtype),
        scratch_shapes=scratch,
        compiler_params=_cparams(("parallel", "parallel", "arbitrary")),
        name=name,
    )(*args)


def _rope_kernel(x_ref, a_ref, bp_ref, bm_ref, o_ref, *, ntiles, rope_tiles, half, scale):
    a, bp, bm = a_ref[...], bp_ref[...], bm_ref[...]
    for t in range(ntiles):
        x = x_ref[:, t * LANE:(t + 1) * LANE].astype(F32)
        if t in rope_tiles:
            x = x * a + pltpu.roll(x, half, 1) * bp + pltpu.roll(x, LANE - half, 1) * bm
        o_ref[:, t * LANE:(t + 1) * LANE] = (x * scale).astype(o_ref.dtype)


def rope_cast(x, tables, seq, *, col_off, width, rope_tiles, half, scale, bt=256):
    m = x.shape[0]
    assert col_off % width == 0 and width % LANE == 0 and seq % bt == 0 and m % bt == 0
    cb = col_off // width
    nseq = seq // bt
    tab_spec = pl.BlockSpec((bt, LANE), lambda i: (i % nseq, 0))
    return pl.pallas_call(
        functools.partial(_rope_kernel, ntiles=width // LANE, rope_tiles=frozenset(rope_tiles),
                          half=half, scale=scale),
        grid=(m // bt,),
        in_specs=[pl.BlockSpec((bt, width), lambda i: (i, cb)), tab_spec, tab_spec, tab_spec],
        out_specs=pl.BlockSpec((bt, width), lambda i: (i, 0)),
        out_shape=jax.ShapeDtypeStruct((m, width), BF16),
        compiler_params=_cparams(("parallel",)),
        name="rope_cast",
    )(x, *tables)


def _rope_tables(seq, theta, rot_dim):
    half = rot_dim // 2
    inv = jnp.exp(-math.log(theta) * jnp.arange(half, dtype=F32) * (2.0 / rot_dim))
    ang = jnp.arange(seq, dtype=F32)[:, None] * inv[None, :]
    cos, sin = jnp.cos(ang), jnp.sin(ang)
    pad = LANE - rot_dim
    a = jnp.concatenate([cos, cos, jnp.ones((seq, pad), F32)], axis=1)
    bp = jnp.concatenate([jnp.zeros((seq, half), F32), sin, jnp.zeros((seq, pad), F32)], axis=1)
    bm = jnp.concatenate([-sin, jnp.zeros((seq, half + pad), F32)], axis=1)
    return a, bp, bm


def _retention_kernel(q_ref, k_ref, v_ref, g_ref, cos_ref, sin_ref, hmask_ref, idec_ref,
                      qdec_ref, kdec_ref, cdec_ref, bmask_ref, o_ref, state_ref):
    hw = RET_HEADS * RET_QK_DIM // 2

    @pl.when(pl.program_id(1) == 0)
    def _():
        state_ref[...] = jnp.zeros_like(state_ref)

    cos, sin = cos_ref[...], sin_ref[...]

    def rot(x):
        x1, x2 = x[:, :hw], x[:, hw:]
        return jnp.concatenate([x1 * cos - x2 * sin, x2 * cos + x1 * sin], axis=1)

    q = rot(q_ref[...])
    k = rot(k_ref[...]) * (RET_QK_DIM ** -0.5)
    v = v_ref[...]
    kt = k.T.astype(BF16)
    state = state_ref[...]
    cross = jnp.dot(q.astype(BF16), state.astype(BF16), preferred_element_type=F32) * qdec_ref[...]
    for h in range(RET_HEADS):
        cols = slice(h * RET_V_DIM, (h + 1) * RET_V_DIM)
        qm = (q * hmask_ref[h:h + 1, :]).astype(BF16)
        scores = jnp.dot(qm, kt, preferred_element_type=F32) * idec_ref[h]
        inner = jnp.dot(scores.astype(BF16), v[:, cols].astype(BF16), preferred_element_type=F32)
        o = inner + cross[:, cols]
        o = o * lax.rsqrt(jnp.mean(o * o, axis=-1, keepdims=True) + NORM_EPS)
        gate = g_ref[:, cols]
        o_ref[:, cols] = (gate * jax.nn.sigmoid(gate) * o).astype(o_ref.dtype)
    vd = (v * kdec_ref[...]).astype(BF16)
    upd = jnp.dot(kt, vd, preferred_element_type=F32)
    state_ref[...] = state * cdec_ref[...] + upd * bmask_ref[...]


def _retention_tables(seq):
    h, c, dv = RET_HEADS, RET_CHUNK, RET_V_DIM
    half = RET_QK_DIM // 2
    inv = jnp.exp(-math.log(RET_ROPE_THETA) * jnp.arange(half, dtype=F32) * (2.0 / RET_QK_DIM))
    ang = jnp.arange(seq, dtype=F32)[:, None] * inv[None, :]
    cos = jnp.tile(jnp.cos(ang), (1, h))
    sin = jnp.tile(jnp.sin(ang), (1, h))
    log_gamma = jnp.log1p(-jnp.exp2(-5.0 - jnp.arange(h, dtype=F32)))
    idx = jnp.arange(c, dtype=F32)
    rel = idx[:, None] - idx[None, :]
    idec = jnp.where(rel[None] >= 0,
                     jnp.exp(jnp.maximum(rel, 0.0)[None] * log_gamma[:, None, None]), 0.0)
    qdec = jnp.exp((idx + 1.0)[:, None] * log_gamma[None, :])
    kdec = jnp.exp((c - 1.0 - idx)[:, None] * log_gamma[None, :])
    cdec = jnp.exp(c * log_gamma)[None, :]
    rep = lambda a: jnp.repeat(a, dv, axis=1)
    lane_head = (np.arange(2 * h * half) % (h * half)) // half
    hmask = jnp.asarray(lane_head[None, :] == np.arange(h)[:, None], F32)
    bmask = jnp.asarray(lane_head[:, None] == (np.arange(h * dv) // dv)[None, :], F32)
    return cos, sin, hmask, idec, rep(qdec), rep(kdec), rep(cdec), bmask


def retention(proj, batch, seq, tables):
    c = RET_CHUNK
    nch = seq // c
    qw = RET_HEADS * RET_QK_DIM
    col = lambda name: _SEG[name][0] // _SEG[name][1]
    row = lambda b, n: b * nch + n
    const2 = lambda b, n: (0, 0)
    cos, sin, hmask, idec, qdec, kdec, cdec, bmask = tables
    return pl.pallas_call(
        _retention_kernel,
        grid=(batch, nch),
        in_specs=[pl.BlockSpec((c, qw), lambda b, n: (row(b, n), col("rq"))),
                  pl.BlockSpec((c, qw), lambda b, n: (row(b, n), col("rk"))),
                  pl.BlockSpec((c, RET_OUT), lambda b, n: (row(b, n), col("rv"))),
                  pl.BlockSpec((c, RET_OUT), lambda b, n: (row(b, n), col("rg"))),
                  pl.BlockSpec((c, qw // 2), lambda b, n: (n, 0)),
                  pl.BlockSpec((c, qw // 2), lambda b, n: (n, 0)),
                  pl.BlockSpec(hmask.shape, const2),
                  pl.BlockSpec(idec.shape, lambda b, n: (0, 0, 0)),
                  pl.BlockSpec(qdec.shape, const2),
                  pl.BlockSpec(kdec.shape, const2),
                  pl.BlockSpec(cdec.shape, const2),
                  pl.BlockSpec(bmask.shape, const2)],
        out_specs=pl.BlockSpec((c, RET_OUT), lambda b, n: (row(b, n), 0)),
        out_shape=jax.ShapeDtypeStruct((batch * seq, RET_OUT), BF16),
        scratch_shapes=[pltpu.VMEM((qw, RET_OUT), F32)],
        compiler_params=_cparams(("parallel", "arbitrary")),
        name="retention",
    )(proj, proj, proj, proj, cos, sin, hmask, idec, qdec, kdec, cdec, bmask)


def _flash_update(s, v_c, m_ref, l_ref, acc_ref, h, *, guard):
    reps = s.shape[1] // LANE
    m_prev = m_ref[h]
    m_new = jnp.maximum(m_prev, jnp.max(s, axis=1, keepdims=True))
    m_sub = jnp.where(m_new == -jnp.inf, 0.0, m_new) if guard else m_new
    p = jnp.exp(s - jnp.tile(m_sub, (1, reps)))
    alpha = jnp.exp(m_prev - m_sub)
    l_ref[h] = alpha * l_ref[h] + jnp.sum(p, axis=1, keepdims=True)
    m_ref[h] = m_new
    acc_ref[h] = acc_ref[h] * alpha + jnp.dot(p.astype(BF16), v_c, preferred_element_type=F32)


INT_MIN = -2 ** 31
IDX_GROUP = 8
ROW_BLOCK = 64


def _dsa_kernel(qi_ref, w_ref, kidx_ref, q_ref, k_ref, v_ref, o_ref,
                qis_ref, wb_ref, lg_ref, key_ref, bias_ref, qs_ref, m_ref, l_ref, acc_ref,
                *, tq, topk):
    tk = tq
    i = pl.program_id(1)
    nchunks = i + 1
    reps = tk // LANE
    ngroups = IDX_HEADS // IDX_GROUP
    group = DSA_HEADS // DSA_KV_HEADS
    d = DSA_HEAD_DIM

    w = w_ref[...] * (IDX_HEADS ** -0.5)
    for h in range(IDX_HEADS):
        qis_ref[h * tq:(h + 1) * tq, :] = qi_ref[:, h * LANE:(h + 1) * LANE]
        wb_ref[h] = jnp.broadcast_to(w[:, h:h + 1], (tq, LANE))

    def score_chunk(c, diagonal):
        r0 = pl.multiple_of(c * tk, tk)
        kc = kidx_ref[pl.ds(r0, tk), :]
        for g in range(ngroups):
            lg_ref[...] = lax.dot_general(
                qis_ref[g * IDX_GROUP * tq:(g + 1) * IDX_GROUP * tq, :], kc,
                (((1,), (1,)), ((), ())), preferred_element_type=F32)

            def rows_body(rb, _, g=g):
                q0 = pl.multiple_of(rb * ROW_BLOCK, ROW_BLOCK)
                rows = pl.ds(q0, ROW_BLOCK)
                acc = jnp.zeros((ROW_BLOCK, tk), F32) if g == 0 else bias_ref[c, rows, :]
                for h in range(IDX_GROUP):
                    lg = lg_ref[pl.ds(h * tq + q0, ROW_BLOCK), :]
                    wgt = jnp.tile(wb_ref[g * IDX_GROUP + h, rows, :], (1, reps))
                    acc = acc + jnp.maximum(lg, 0.0) * wgt
                if g < ngroups - 1:
                    bias_ref[c, rows, :] = acc
                else:
                    bits = pltpu.bitcast(acc, jnp.int32)
                    key = bits ^ (lax.shift_right_arithmetic(bits, 31) & 0x7FFFFFFF)
                    if diagonal:
                        kpos = lax.broadcasted_iota(jnp.int32, (ROW_BLOCK, tk), 1)
                        qpos = q0 + lax.broadcasted_iota(jnp.int32, (ROW_BLOCK, tk), 0)
                        key = jnp.where(kpos <= qpos, key, INT_MIN)
                    key_ref[c, rows, :] = key
                return 0

            lax.fori_loop(0, tq // ROW_BLOCK, rows_body, 0)

    def score_body(c, _):
        score_chunk(c, False)
        return 0

    lax.fori_loop(0, i, score_body, 0)
    score_chunk(i, True)

    def count_ge(cand):
        def body(c, cnt):
            blk = key_ref[c]
            for r in range(reps):
                cnt = cnt + (blk[:, r * LANE:(r + 1) * LANE] >= cand).astype(jnp.int32)
            return cnt
        cnt = lax.fori_loop(0, nchunks, body, jnp.zeros((tq, LANE), jnp.int32))
        return jnp.sum(cnt, axis=1, keepdims=True)

    zero = jnp.zeros((tq, LANE), jnp.int32)
    t0 = jnp.where(count_ge(zero) >= topk, zero, INT_MIN)

    def bit_body(b, t):
        cand = t | lax.shift_left(jnp.int32(1), 30 - b)
        return jnp.where(count_ge(cand) >= topk, cand, t)

    thr = lax.fori_loop(0, 31, bit_body, t0)
    thr = jnp.maximum(thr, INT_MIN + 1)
    thr = jnp.tile(thr, (1, reps))

    def bias_body(c, _):
        bias_ref[c] = jnp.where(key_ref[c] >= thr, 0.0, -jnp.inf)
        return 0

    lax.fori_loop(0, nchunks, bias_body, 0)

    for g in range(DSA_KV_HEADS):
        for hh in range(group):
            h = g * group + hh
            qs_ref[hh * tq:(hh + 1) * tq, :] = q_ref[:, h * d:(h + 1) * d]
        m_ref[...] = jnp.full_like(m_ref, -jnp.inf)
        l_ref[...] = jnp.zeros_like(l_ref)
        acc_ref[...] = jnp.zeros_like(acc_ref)

        def att_body(c, _, g=g):
            r0 = pl.multiple_of(c * tk, tk)
            kc = k_ref[pl.ds(r0, tk), g * d:(g + 1) * d]
            vc = v_ref[pl.ds(r0, tk), g * d:(g + 1) * d]
            s_all = lax.dot_general(qs_ref[...], kc, (((1,), (1,)), ((), ())),
                                    preferred_element_type=F32)
            bias = bias_ref[c]
            for hh in range(group):
                _flash_update(s_all[hh * tq:(hh + 1) * tq] + bias, vc, m_ref, l_ref, acc_ref, hh,
                              guard=True)
            return 0

        lax.fori_loop(0, nchunks, att_body, 0)
        for hh in range(group):
            h = g * group + hh
            o_ref[:, h * d:(h + 1) * d] = (acc_ref[hh] / l_ref[hh]).astype(o_ref.dtype)


def dsa_attention(qi, proj, kidx, q, k, v, batch, seq, *, tq=256):
    nq = seq // tq
    topk = min(DSA_TOPK, seq // 4)
    group = DSA_HEADS // DSA_KV_HEADS
    kvw = DSA_KV_HEADS * DSA_HEAD_DIM
    row = lambda b, i: (b * nq + i, 0)
    wcol = _SEG["iw"][0] // LANE
    return pl.pallas_call(
        functools.partial(_dsa_kernel, tq=tq, topk=topk),
        grid=(batch, nq),
        in_specs=[pl.BlockSpec((tq, IDX_HEADS * IDX_HEAD_DIM), row),
                  pl.BlockSpec((tq, LANE), lambda b, i: (b * nq + i, wcol)),
                  pl.BlockSpec((seq, IDX_HEAD_DIM), lambda b, i: (b, 0)),
                  pl.BlockSpec((tq, DSA_OUT), row),
                  pl.BlockSpec((seq, kvw), lambda b, i: (b, 0)),
                  pl.BlockSpec((seq, kvw), lambda b, i: (b, 0))],
        out_specs=pl.BlockSpec((tq, DSA_OUT), row),
        out_shape=jax.ShapeDtypeStruct((batch * seq, DSA_OUT), BF16),
        scratch_shapes=[pltpu.VMEM((IDX_HEADS * tq, IDX_HEAD_DIM), BF16),
                        pltpu.VMEM((IDX_HEADS, tq, LANE), F32),
                        pltpu.VMEM((IDX_GROUP * tq, tq), F32),
                        pltpu.VMEM((nq, tq, tq), jnp.int32),
                        pltpu.VMEM((nq, tq, tq), F32),
                        pltpu.VMEM((group * tq, DSA_HEAD_DIM), BF16),
                        pltpu.VMEM((group, tq, LANE), F32),
                        pltpu.VMEM((group, tq, LANE), F32),
                        pltpu.VMEM((group, tq, DSA_HEAD_DIM), F32)],
        compiler_params=_cparams(("parallel", "arbitrary")),
        name="dsa_attention",
    )(qi, proj, kidx, q, k, v)


def _mla_kernel(q_ref, kn_ref, kpe_ref, v_ref, o_ref, m_ref, l_ref, acc_ref, *, tq, hb):
    tk = tq
    i = pl.program_id(2)
    m_ref[...] = jnp.full_like(m_ref, -jnp.inf)
    l_ref[...] = jnp.zeros_like(l_ref)
    acc_ref[...] = jnp.zeros_like(acc_ref)
    causal = (lax.broadcasted_iota(jnp.int32, (tq, tk), 1)
              <= lax.broadcasted_iota(jnp.int32, (tq, tk), 0))

    def step(c, masked):
        r0 = pl.multiple_of(c * tk, tk)
        kpe = kpe_ref[pl.ds(r0, tk), :]
        for h in range(hb):
            cols = slice(h * MLA_NOPE_DIM, (h + 1) * MLA_NOPE_DIM)
            kf = jnp.concatenate([kn_ref[pl.ds(r0, tk), cols], kpe], axis=1)
            s = lax.dot_general(q_ref[:, h * MLA_QK_PAD:(h + 1) * MLA_QK_PAD], kf,
                                (((1,), (1,)), ((), ())), preferred_element_type=F32)
            if masked:
                s = jnp.where(causal, s, -jnp.inf)
            _flash_update(s, v_ref[pl.ds(r0, tk), cols], m_ref, l_ref, acc_ref, h, guard=False)

    def body(c, _):
        step(c, False)
        return 0

    lax.fori_loop(0, i, body, 0)
    step(i, True)
    for h in range(hb):
        o_ref[:, h * MLA_V_DIM:(h + 1) * MLA_V_DIM] = (acc_ref[h] / l_ref[h]).astype(o_ref.dtype)


def mla_attention(q, kv, kpe, batch, seq, *, tq=512, hb=2):
    nq = seq // tq
    ng = MLA_HEADS // hb
    return pl.pallas_call(
        functools.partial(_mla_kernel, tq=tq, hb=hb),
        grid=(batch, ng, nq),
        in_specs=[pl.BlockSpec((tq, hb * MLA_QK_PAD), lambda b, g, i: (b * nq + i, g)),
                  pl.BlockSpec((seq, hb * MLA_NOPE_DIM), lambda b, g, i: (b, g)),
                  pl.BlockSpec((seq, LANE), lambda b, g, i: (b, 0)),
                  pl.BlockSpec((seq, hb * MLA_V_DIM), lambda b, g, i: (b, ng + g))],
        out_specs=pl.BlockSpec((tq, hb * MLA_V_DIM), lambda b, g, i: (b * nq + i, g)),
        out_shape=jax.ShapeDtypeStruct((batch * seq, MLA_OUT), BF16),
        scratch_shapes=[pltpu.VMEM((hb, tq, LANE), F32), pltpu.VMEM((hb, tq, LANE), F32),
                        pltpu.VMEM((hb, tq, MLA_V_DIM), F32)],
        compiler_params=_cparams(("parallel", "parallel", "arbitrary")),
        name="mla_attention",
    )(q, kv, kpe, kv)


def _relayout_w_in(w):
    d = w.shape[0]
    seg = lambda name: w[:, _NAT[name][0]:_NAT[name][0] + _NAT[name][1]]
    halves = lambda a: a.reshape(d, RET_HEADS, 2, RET_QK_DIM // 2).transpose(0, 2, 1, 3).reshape(d, -1)
    zeros = lambda n: jnp.zeros((d, n), w.dtype)
    order = sorted(_SEG, key=lambda s: _SEG[s][0])
    parts, pos = [], 0
    for name in order:
        off, width = _SEG[name]
        assert off == pos
        a = halves(seg(name)) if name in ("rq", "rk") else seg(name)
        parts.append(a)
        if a.shape[1] < width:
            parts.append(zeros(width - a.shape[1]))
        pos += width
    parts.append(zeros(PROJ_WIDTH - pos))
    return jnp.concatenate(parts, axis=1).astype(BF16)


def _relayout_w_uq(w):
    r = w.shape[0]
    a = w.reshape(r, MLA_HEADS, MLA_NOPE_DIM + MLA_ROPE_DIM)
    a = jnp.pad(a, ((0, 0), (0, 0), (0, MLA_QK_PAD - a.shape[2])))
    return a.reshape(r, MLA_HEADS * MLA_QK_PAD).astype(BF16)


def _relayout_w_ukv(w):
    r = w.shape[0]
    a = w.reshape(r, MLA_HEADS, 2, MLA_NOPE_DIM).transpose(0, 2, 1, 3)
    return a.reshape(r, 2 * MLA_HEADS * MLA_NOPE_DIM).astype(BF16)


def kernel(x, attn_norm, w_in, mla_q_norm, mla_kv_norm, w_uq, w_ukv, w_o, mlp_norm, w_up,
           w_down, final_norm):
    batch, seq, d_model = x.shape
    depth = w_in.shape[0]
    m = batch * seq
    tq = 256
    assert seq % tq == 0

    ret_tables = _retention_tables(seq)
    part_tables = _rope_tables(seq, PARTIAL_ROPE_THETA, PARTIAL_ROT_DIM)
    mla_tables = _rope_tables(seq, MLA_ROPE_THETA, MLA_ROPE_DIM)
    part_half, mla_half = PARTIAL_ROT_DIM // 2, MLA_ROPE_DIM // 2
    mla_scale = (MLA_NOPE_DIM + MLA_ROPE_DIM) ** -0.5

    def seg_rope(proj, name, rope_tiles, half, tables, scale):
        off, width = _SEG[name]
        return rope_cast(proj, tables, seq, col_off=off, width=width, rope_tiles=rope_tiles,
                         half=half, scale=scale)

    h = x.reshape(m, d_model)
    for l in range(depth):
        u = rmsnorm(h, attn_norm[l], BF16)
        proj = matmul(u, _relayout_w_in(w_in[l]), F32, name="mm_in")

        ret = retention(proj, batch, seq, ret_tables)

        qi = seg_rope(proj, "iq", range(IDX_HEADS), part_half, part_tables, IDX_HEAD_DIM ** -0.5)
        kidx = seg_rope(proj, "ik", (0,), part_half, part_tables, 1.0)
        dq = seg_rope(proj, "dq", range(DSA_HEADS), part_half, part_tables, DSA_HEAD_DIM ** -0.5)
        dk = seg_rope(proj, "dk", range(DSA_KV_HEADS), part_half, part_tables, 1.0)
        dv = seg_rope(proj, "dv", (), part_half, part_tables, 1.0)
        dsa = dsa_attention(qi, proj, kidx, dq, dk, dv, batch, seq, tq=tq)

        cq = rmsnorm(proj, mla_q_norm[l], BF16, col_off=_SEG["cq"][0], width=MLA_Q_LORA)
        ckv = rmsnorm(proj, mla_kv_norm[l], BF16, col_off=_SEG["ckv"][0], width=MLA_KV_LORA)
        q_up = matmul(cq, _relayout_w_uq(w_uq[l]), F32, name="mm_uq")
        kv_up = matmul(ckv, _relayout_w_ukv(w_ukv[l]), BF16, name="mm_ukv")
        q_full = rope_cast(q_up, mla_tables, seq, col_off=0, width=q_up.shape[1],
                           rope_tiles=range(1, 2 * MLA_HEADS, 2), half=mla_half, scale=mla_scale)
        kpe = seg_rope(proj, "kr", (0,), mla_half, mla_tables, 1.0)
        mla = mla_attention(q_full, kv_up, kpe, batch, seq)

        mixed = jnp.concatenate([ret, dsa, mla], axis=1)
        h = matmul(mixed, w_o[l].astype(BF16), F32, res=h, bn=512, name="mm_o")
        u = rmsnorm(h, mlp_norm[l], BF16)
        up = matmul(u, w_up[l].astype(BF16), BF16, act="relu2", name="mm_up")
        h = matmul(up, w_down[l].astype(BF16), F32, res=h, bn=512, name="mm_down")
    out = rmsnorm(h, final_norm, x.dtype)
    return out.reshape(batch, seq, d_model)
```

```python
import functools
import math

import numpy as np
import jax
import jax.numpy as jnp
from jax import lax
from jax.experimental import pallas as pl
from jax.experimental.pallas import tpu as pltpu

F32 = jnp.float32
BF16 = jnp.bfloat16

RET_HEADS = 8
RET_QK_DIM = 64
RET_V_DIM = 128
RET_CHUNK = 128
RET_ROPE_THETA = 10000.0
DSA_HEADS = 12
DSA_KV_HEADS = 4
DSA_HEAD_DIM = 128
IDX_HEADS = 32
IDX_HEAD_DIM = 128
DSA_TOPK = 256
PARTIAL_ROT_DIM = 32
PARTIAL_ROPE_THETA = 500000.0
MLA_HEADS = 12
MLA_Q_LORA = 768
MLA_KV_LORA = 256
MLA_NOPE_DIM = 128
MLA_ROPE_DIM = 64
MLA_V_DIM = 128
MLA_ROPE_THETA = 10000.0
NORM_EPS = 1e-6

LANE = 128
VMEM_LIMIT = 56 * 1024 * 1024

RET_OUT = RET_HEADS * RET_V_DIM
DSA_OUT = DSA_HEADS * DSA_HEAD_DIM
MLA_OUT = MLA_HEADS * MLA_V_DIM
MLA_QK_PAD = 256

_NAT = {}
_off = 0
for _name, _w in (("rq", 512), ("rk", 512), ("rv", 1024), ("rg", 1024), ("dq", 1536),
                  ("dk", 512), ("dv", 512), ("iq", 4096), ("ik", 128), ("iw", 32),
                  ("cq", 768), ("ckv", 256), ("kr", 64)):
    _NAT[_name] = (_off, _w)
    _off += _w
IN_WIDTH = _off

_SEG = {"iq": (0, 4096), "rq": (4096, 512), "dq": (4608, 1536), "rk": (6144, 512),
        "dk": (6656, 512), "rv": (7168, 1024), "rg": (8192, 1024), "dv": (9216, 512),
        "ckv": (9728, 256), "cq": (9984, 768), "ik": (10752, 128), "iw": (10880, 128),
        "kr": (11008, 128)}
PROJ_WIDTH = 11264


def _cparams(sem):
    return pltpu.CompilerParams(dimension_semantics=sem, vmem_limit_bytes=VMEM_LIMIT)


def _rmsnorm_kernel(x_ref, g_ref, o_ref):
    x = x_ref[...].astype(F32)
    ms = jnp.mean(x * x, axis=-1, keepdims=True)
    o_ref[...] = (x * lax.rsqrt(ms + NORM_EPS) * g_ref[...]).astype(o_ref.dtype)


def rmsnorm(x, g, out_dtype, *, col_off=0, width=None, bm=256):
    m = x.shape[0]
    width = x.shape[1] if width is None else width
    assert col_off % width == 0 and m % bm == 0
    cb = col_off // width
    return pl.pallas_call(
        _rmsnorm_kernel,
        grid=(m // bm,),
        in_specs=[pl.BlockSpec((bm, width), lambda i: (i, cb)),
                  pl.BlockSpec((1, width), lambda i: (0, 0))],
        out_specs=pl.BlockSpec((bm, width), lambda i: (i, 0)),
        out_shape=jax.ShapeDtypeStruct((m, width), out_dtype),
        compiler_params=_cparams(("parallel",)),
        name="rmsnorm",
    )(x, g.reshape(1, width).astype(F32))


def _mm_kernel(*refs, act, has_res, nk):
    if has_res:
        x_ref, w_ref, r_ref, o_ref = refs[:4]
        scratch = refs[4:]
    else:
        x_ref, w_ref, o_ref = refs[:3]
        r_ref = None
        scratch = refs[3:]

    def epilogue(acc):
        if act == "relu2":
            acc = jnp.square(jnp.maximum(acc, 0.0))
        if has_res:
            acc = acc + r_ref[...]
        o_ref[...] = acc.astype(o_ref.dtype)

    part = jnp.dot(x_ref[...], w_ref[...], preferred_element_type=F32)
    if nk == 1:
        epilogue(part)
    else:
        acc_ref = scratch[0]
        k = pl.program_id(2)

        @pl.when(k == 0)
        def _():
            acc_ref[...] = part

        @pl.when(jnp.logical_and(k > 0, k < nk - 1))
        def _():
            acc_ref[...] += part

        @pl.when(k == nk - 1)
        def _():
            epilogue(acc_ref[...] + part)


def matmul(x, w, out_dtype, *, name, act=None, res=None, bm=1024, bn=1024, bk=4096):
    m, kdim = x.shape
    n = w.shape[1]
    bm, bn, bk = min(bm, m), min(bn, n), min(bk, kdim)
    assert m % bm == 0 and n % bn == 0 and kdim % bk == 0
    nk = kdim // bk
    in_specs = [pl.BlockSpec((bm, bk), lambda i, j, k: (i, k)),
                pl.BlockSpec((bk, bn), lambda i, j, k: (k, j))]
    args = [x, w]
    if res is not None:
        in_specs.append(pl.BlockSpec((bm, bn), lambda i, j, k: (i, j)))
        args.append(res)
    scratch = [pltpu.VMEM((bm, bn), F32)] if nk > 1 else []
    return pl.pallas_call(
        functools.partial(_mm_kernel, act=act, has_res=res is not None, nk=nk),
        grid=(m // bm, n // bn, nk),
        in_specs=in_specs,
        out_specs=pl.BlockSpec((bm, bn), lambda i, j, k: (i, j)),
        out_shape=jax.ShapeDtypeStruct((m, n), out_dtype),
        scratch_shapes=scratch,
        compiler_params=_cparams(("parallel", "parallel", "arbitrary")),
        name=name,
    )(*args)


def _rope_kernel(x_ref, a_ref, bp_ref, bm_ref, o_ref, *, ntiles, rope_tiles, half, scale):
    a, bp, bm = a_ref[...], bp_ref[...], bm_ref[...]
    for t in range(ntiles):
        x = x_ref[:, t * LANE:(t + 1) * LANE].astype(F32)
        if t in rope_tiles:
            x = x * a + pltpu.roll(x, half, 1) * bp + pltpu.roll(x, LANE - half, 1) * bm
        o_ref[:, t * LANE:(t + 1) * LANE] = (x * scale).astype(o_ref.dtype)


def rope_cast(x, tables, seq, *, col_off, width, rope_tiles, half, scale, bt=256):
    m = x.shape[0]
    assert col_off % width == 0 and width % LANE == 0 and seq % bt == 0 and m % bt == 0
    cb = col_off // width
    nseq = seq // bt
    tab_spec = pl.BlockSpec((bt, LANE), lambda i: (i % nseq, 0))
    return pl.pallas_call(
        functools.partial(_rope_kernel, ntiles=width // LANE, rope_tiles=frozenset(rope_tiles),
                          half=half, scale=scale),
        grid=(m // bt,),
        in_specs=[pl.BlockSpec((bt, width), lambda i: (i, cb)), tab_spec, tab_spec, tab_spec],
        out_specs=pl.BlockSpec((bt, width), lambda i: (i, 0)),
        out_shape=jax.ShapeDtypeStruct((m, width), BF16),
        compiler_params=_cparams(("parallel",)),
        name="rope_cast",
    )(x, *tables)


def _rope_tables(seq, theta, rot_dim):
    half = rot_dim // 2
    inv = jnp.exp(-math.log(theta) * jnp.arange(half, dtype=F32) * (2.0 / rot_dim))
    ang = jnp.arange(seq, dtype=F32)[:, None] * inv[None, :]
    cos, sin = jnp.cos(ang), jnp.sin(ang)
    pad = LANE - rot_dim
    a = jnp.concatenate([cos, cos, jnp.ones((seq, pad), F32)], axis=1)
    bp = jnp.concatenate([jnp.zeros((seq, half), F32), sin, jnp.zeros((seq, pad), F32)], axis=1)
    bm = jnp.concatenate([-sin, jnp.zeros((seq, half + pad), F32)], axis=1)
    return a, bp, bm


def _retention_kernel(q_ref, k_ref, v_ref, g_ref, cos_ref, sin_ref, hmask_ref, idec_ref,
                      qdec_ref, kdec_ref, cdec_ref, bmask_ref, o_ref, state_ref):
    hw = RET_HEADS * RET_QK_DIM // 2

    @pl.when(pl.program_id(1) == 0)
    def _():
        state_ref[...] = jnp.zeros_like(state_ref)

    cos, sin = cos_ref[...], sin_ref[...]

    def rot(x):
        x1, x2 = x[:, :hw], x[:, hw:]
        return jnp.concatenate([x1 * cos - x2 * sin, x2 * cos + x1 * sin], axis=1)

    q = rot(q_ref[...])
    k = rot(k_ref[...]) * (RET_QK_DIM ** -0.5)
    v = v_ref[...]
    kt = k.T.astype(BF16)
    state = state_ref[...]
    cross = jnp.dot(q.astype(BF16), state.astype(BF16), preferred_element_type=F32) * qdec_ref[...]
    for h in range(RET_HEADS):
        cols = slice(h * RET_V_DIM, (h + 1) * RET_V_DIM)
        qm = (q * hmask_ref[h:h + 1, :]).astype(BF16)
        scores = jnp.dot(qm, kt, preferred_element_type=F32) * idec_ref[h]
        inner = jnp.dot(scores.astype(BF16), v[:, cols].astype(BF16), preferred_element_type=F32)
        o = inner + cross[:, cols]
        o = o * lax.rsqrt(jnp.mean(o * o, axis=-1, keepdims=True) + NORM_EPS)
        gate = g_ref[:, cols]
        o_ref[:, cols] = (gate * jax.nn.sigmoid(gate) * o).astype(o_ref.dtype)
    vd = (v * kdec_ref[...]).astype(BF16)
    upd = jnp.dot(kt, vd, preferred_element_type=F32)
    state_ref[...] = state * cdec_ref[...] + upd * bmask_ref[...]


def _retention_tables(seq):
    h, c, dv = RET_HEADS, RET_CHUNK, RET_V_DIM
    half = RET_QK_DIM // 2
    inv = jnp.exp(-math.log(RET_ROPE_THETA) * jnp.arange(half, dtype=F32) * (2.0 / RET_QK_DIM))
    ang = jnp.arange(seq, dtype=F32)[:, None] * inv[None, :]
    cos = jnp.tile(jnp.cos(ang), (1, h))
    sin = jnp.tile(jnp.sin(ang), (1, h))
    log_gamma = jnp.log1p(-jnp.exp2(-5.0 - jnp.arange(h, dtype=F32)))
    idx = jnp.arange(c, dtype=F32)
    rel = idx[:, None] - idx[None, :]
    idec = jnp.where(rel[None] >= 0,
                     jnp.exp(jnp.maximum(rel, 0.0)[None] * log_gamma[:, None, None]), 0.0)
    qdec = jnp.exp((idx + 1.0)[:, None] * log_gamma[None, :])
    kdec = jnp.exp((c - 1.0 - idx)[:, None] * log_gamma[None, :])
    cdec = jnp.exp(c * log_gamma)[None, :]
    rep = lambda a: jnp.repeat(a, dv, axis=1)
    lane_head = (np.arange(2 * h * half) % (h * half)) // half
    hmask = jnp.asarray(lane_head[None, :] == np.arange(h)[:, None], F32)
    bmask = jnp.asarray(lane_head[:, None] == (np.arange(h * dv) // dv)[None, :], F32)
    return cos, sin, hmask, idec, rep(qdec), rep(kdec), rep(cdec), bmask


def retention(proj, batch, seq, tables):
    c = RET_CHUNK
    nch = seq // c
    qw = RET_HEADS * RET_QK_DIM
    col = lambda name: _SEG[name][0] // _SEG[name][1]
    row = lambda b, n: b * nch + n
    const2 = lambda b, n: (0, 0)
    cos, sin, hmask, idec, qdec, kdec, cdec, bmask = tables
    return pl.pallas_call(
        _retention_kernel,
        grid=(batch, nch),
        in_specs=[pl.BlockSpec((c, qw), lambda b, n: (row(b, n), col("rq"))),
                  pl.BlockSpec((c, qw), lambda b, n: (row(b, n), col("rk"))),
                  pl.BlockSpec((c, RET_OUT), lambda b, n: (row(b, n), col("rv"))),
                  pl.BlockSpec((c, RET_OUT), lambda b, n: (row(b, n), col("rg"))),
                  pl.BlockSpec((c, qw // 2), lambda b, n: (n, 0)),
                  pl.BlockSpec((c, qw // 2), lambda b, n: (n, 0)),
                  pl.BlockSpec(hmask.shape, const2),
                  pl.BlockSpec(idec.shape, lambda b, n: (0, 0, 0)),
                  pl.BlockSpec(qdec.shape, const2),
                  pl.BlockSpec(kdec.shape, const2),
                  pl.BlockSpec(cdec.shape, const2),
                  pl.BlockSpec(bmask.shape, const2)],
        out_specs=pl.BlockSpec((c, RET_OUT), lambda b, n: (row(b, n), 0)),
        out_shape=jax.ShapeDtypeStruct((batch * seq, RET_OUT), BF16),
        scratch_shapes=[pltpu.VMEM((qw, RET_OUT), F32)],
        compiler_params=_cparams(("parallel", "arbitrary")),
        name="retention",
    )(proj, proj, proj, proj, cos, sin, hmask, idec, qdec, kdec, cdec, bmask)


def _flash_update(s, v_c, m_ref, l_ref, acc_ref, h, *, guard):
    reps = s.shape[1] // LANE

    def fold(x, op):
        tiles = [x[:, r * LANE:(r + 1) * LANE] for r in range(reps)]
        while len(tiles) > 1:
            tiles = [op(a, b) for a, b in zip(tiles[::2], tiles[1::2])]
        return tiles[0]

    m_prev = m_ref[h]
    m_new = jnp.maximum(m_prev, jnp.max(fold(s, jnp.maximum), axis=1, keepdims=True))
    m_sub = jnp.where(m_new == -jnp.inf, 0.0, m_new) if guard else m_new
    p = jnp.exp(s - jnp.tile(m_sub, (1, reps)))
    alpha = jnp.exp(m_prev - m_sub)
    l_ref[h] = alpha * l_ref[h] + jnp.sum(fold(p, jnp.add), axis=1, keepdims=True)
    m_ref[h] = m_new
    acc_ref[h] = acc_ref[h] * alpha + jnp.dot(p.astype(BF16), v_c, preferred_element_type=F32)


INT_MIN = -2 ** 31
IDX_GROUP = 8
ROW_BLOCK = 64
SEARCH_ROWS = 128
KV_PER_LOOP = 2


def _dsa_kernel(qi_ref, w_ref, kidx_ref, q_ref, k_ref, v_ref, o_ref,
                qis_ref, wb_ref, lg_ref, key_ref, bias_ref, qs_ref, m_ref, l_ref, acc_ref,
                *, tq, topk):
    tk = tq
    i = pl.program_id(1)
    nchunks = i + 1
    reps = tk // LANE
    ngroups = IDX_HEADS // IDX_GROUP
    group = DSA_HEADS // DSA_KV_HEADS
    d = DSA_HEAD_DIM

    w = w_ref[...] * (IDX_HEADS ** -0.5)
    for h in range(IDX_HEADS):
        qis_ref[h * tq:(h + 1) * tq, :] = qi_ref[:, h * LANE:(h + 1) * LANE]
        wb_ref[h] = jnp.broadcast_to(w[:, h:h + 1], (tq, LANE))

    def score_chunk(c, diagonal):
        r0 = pl.multiple_of(c * tk, tk)
        kc = kidx_ref[pl.ds(r0, tk), :]
        for g in range(ngroups):
            lg_ref[g] = lax.dot_general(
                qis_ref[g * IDX_GROUP * tq:(g + 1) * IDX_GROUP * tq, :], kc,
                (((1,), (1,)), ((), ())), preferred_element_type=F32)
        for rb in range(tq // ROW_BLOCK):
            q0 = rb * ROW_BLOCK
            rows = slice(q0, q0 + ROW_BLOCK)
            acc = jnp.zeros((ROW_BLOCK, tk), F32)
            for h in range(IDX_HEADS):
                g, hg = divmod(h, IDX_GROUP)
                lg = lg_ref[g, hg * tq + q0:hg * tq + q0 + ROW_BLOCK, :]
                acc = acc + jnp.maximum(lg, 0.0) * jnp.tile(wb_ref[h, rows, :], (1, reps))
            bits = pltpu.bitcast(acc, jnp.int32)
            key = bits ^ (lax.shift_right_arithmetic(bits, 31) & 0x7FFFFFFF)
            if diagonal:
                kpos = lax.broadcasted_iota(jnp.int32, (ROW_BLOCK, tk), 1)
                qpos = q0 + lax.broadcasted_iota(jnp.int32, (ROW_BLOCK, tk), 0)
                key = jnp.where(kpos <= qpos, key, INT_MIN)
            key_ref[c, rows, :] = key

    def score_body(c, _):
        score_chunk(c, False)
        return 0

    lax.fori_loop(0, i, score_body, 0)
    score_chunk(i, True)

    for rs in range(tq // SEARCH_ROWS):
        rows = slice(rs * SEARCH_ROWS, (rs + 1) * SEARCH_ROWS)

        def count_ge(cand, rows=rows):
            def body(c, cnt):
                for r in range(reps):
                    blk = key_ref[c, rows, r * LANE:(r + 1) * LANE]
                    cnt = cnt + (blk >= cand).astype(jnp.int32)
                return cnt
            cnt = lax.fori_loop(0, nchunks, body, jnp.zeros((SEARCH_ROWS, LANE), jnp.int32))
            return jnp.broadcast_to(jnp.sum(cnt, axis=1, keepdims=True), (SEARCH_ROWS, LANE))

        zero = jnp.zeros((SEARCH_ROWS, LANE), jnp.int32)
        n0 = count_ge(zero)
        t0 = jnp.where(n0 >= topk, zero, INT_MIN)
        c0 = jnp.where(n0 >= topk, n0, nchunks * tk)

        def bit_cond(state):
            b, _, cnt = state
            return jnp.logical_and(b < 31, jnp.max(jnp.abs(cnt - topk)) > 0)

        def bit_body(state, count_ge=count_ge):
            b, t, cnt = state
            cand = t | lax.shift_left(jnp.int32(1), 30 - b)
            n = count_ge(cand)
            take = n >= topk
            return b + 1, jnp.where(take, cand, t), jnp.where(take, n, cnt)

        _, thr, _ = lax.while_loop(bit_cond, bit_body, (jnp.int32(0), t0, c0))
        thr = jnp.maximum(thr, INT_MIN + 1)
        thr = jnp.tile(thr, (1, reps))

        def bias_body(c, _, rows=rows, thr=thr):
            bias_ref[c, rows, :] = jnp.where(key_ref[c, rows, :] >= thr, 0.0, -jnp.inf)
            return 0

        lax.fori_loop(0, nchunks, bias_body, 0)

    @pl.when(nchunks < key_ref.shape[0])
    def _():
        bias_ref[nchunks] = jnp.full((tq, tk), -jnp.inf, F32)

    hpl = KV_PER_LOOP * group
    for g0 in range(0, DSA_KV_HEADS, KV_PER_LOOP):
        for hh in range(hpl):
            h = g0 * group + hh
            qs_ref[hh * tq:(hh + 1) * tq, :] = q_ref[:, h * d:(h + 1) * d]
        m_ref[...] = jnp.full_like(m_ref, -jnp.inf)
        l_ref[...] = jnp.zeros_like(l_ref)
        acc_ref[...] = jnp.zeros_like(acc_ref)

        def att_body(c2, _, g0=g0):
            r0 = pl.multiple_of(c2 * 2 * tk, 2 * tk)
            bias = jnp.concatenate([bias_ref[2 * c2], bias_ref[2 * c2 + 1]], axis=1)
            for gg in range(KV_PER_LOOP):
                g = g0 + gg
                kc = k_ref[pl.ds(r0, 2 * tk), g * d:(g + 1) * d]
                vc = v_ref[pl.ds(r0, 2 * tk), g * d:(g + 1) * d]
                s_all = lax.dot_general(qs_ref[gg * group * tq:(gg + 1) * group * tq, :], kc,
                                        (((1,), (1,)), ((), ())), preferred_element_type=F32)
                for hh in range(group):
                    _flash_update(s_all[hh * tq:(hh + 1) * tq] + bias, vc, m_ref, l_ref, acc_ref,
                                  gg * group + hh, guard=True)
            return 0

        lax.fori_loop(0, (nchunks + 1) // 2, att_body, 0)
        for hh in range(hpl):
            h = g0 * group + hh
            o_ref[:, h * d:(h + 1) * d] = (acc_ref[hh] / l_ref[hh]).astype(o_ref.dtype)


def dsa_attention(qi, proj, kidx, q, k, v, batch, seq, *, tq=256):
    nq = seq // tq
    assert nq % 2 == 0
    topk = min(DSA_TOPK, seq // 4)
    hpl =KV_PER_LOOP * DSA_HEADS // DSA_KV_HEADS
    kvw = DSA_KV_HEADS * DSA_HEAD_DIM
    row = lambda b, i: (b * nq + i, 0)
    wcol = _SEG["iw"][0] // LANE
    return pl.pallas_call(
        functools.partial(_dsa_kernel, tq=tq, topk=topk),
        grid=(batch, nq),
        in_specs=[pl.BlockSpec((tq, IDX_HEADS * IDX_HEAD_DIM), row),
                  pl.BlockSpec((tq, LANE), lambda b, i: (b * nq + i, wcol)),
                  pl.BlockSpec((seq, IDX_HEAD_DIM), lambda b, i: (b, 0)),
                  pl.BlockSpec((tq, DSA_OUT), row),
                  pl.BlockSpec((seq, kvw), lambda b, i: (b, 0)),
                  pl.BlockSpec((seq, kvw), lambda b, i: (b, 0))],
        out_specs=pl.BlockSpec((tq, DSA_OUT), row),
        out_shape=jax.ShapeDtypeStruct((batch * seq, DSA_OUT), BF16),
        scratch_shapes=[pltpu.VMEM((IDX_HEADS * tq, IDX_HEAD_DIM), BF16),
                        pltpu.VMEM((IDX_HEADS, tq, LANE), F32),
                        pltpu.VMEM((IDX_HEADS // IDX_GROUP, IDX_GROUP * tq, tq), F32),
                        pltpu.VMEM((nq, tq, tq), jnp.int32),
                        pltpu.VMEM((nq, tq, tq), F32),
                        pltpu.VMEM((hpl * tq, DSA_HEAD_DIM), BF16),
                        pltpu.VMEM((hpl, tq, LANE), F32),
                        pltpu.VMEM((hpl, tq, LANE), F32),
                        pltpu.VMEM((hpl, tq, DSA_HEAD_DIM), F32)],
        compiler_params=_cparams(("parallel", "arbitrary")),
        name="dsa_attention",
    )(qi, proj, kidx, q, k, v)


def _mla_kernel(q_ref, kn_ref, kpe_ref, v_ref, o_ref, m_ref, l_ref, acc_ref, *, tq, hb):
    tk = tq
    i = pl.program_id(2)
    m_ref[...] = jnp.full_like(m_ref, -jnp.inf)
    l_ref[...] = jnp.zeros_like(l_ref)
    acc_ref[...] = jnp.zeros_like(acc_ref)
    causal = (lax.broadcasted_iota(jnp.int32, (tq, tk), 1)
              <= lax.broadcasted_iota(jnp.int32, (tq, tk), 0))

    def step(c, masked):
        r0 = pl.multiple_of(c * tk, tk)
        kpe = kpe_ref[pl.ds(r0, tk), :]
        for h in range(hb):
            cols = slice(h * MLA_NOPE_DIM, (h + 1) * MLA_NOPE_DIM)
            kf = jnp.concatenate([kn_ref[pl.ds(r0, tk), cols], kpe], axis=1)
            s = lax.dot_general(q_ref[:, h * MLA_QK_PAD:(h + 1) * MLA_QK_PAD], kf,
                                (((1,), (1,)), ((), ())), preferred_element_type=F32)
            if masked:
                s = jnp.where(causal, s, -jnp.inf)
            _flash_update(s, v_ref[pl.ds(r0, tk), cols], m_ref, l_ref, acc_ref, h, guard=False)

    def body(c, _):
        step(c, False)
        return 0

    lax.fori_loop(0, i, body, 0)
    step(i, True)
    for h in range(hb):
        o_ref[:, h * MLA_V_DIM:(h + 1) * MLA_V_DIM] = (acc_ref[h] / l_ref[h]).astype(o_ref.dtype)


def mla_attention(q, kv, kpe, batch, seq, *, tq=512, hb=2):
    nq = seq // tq
    ng = MLA_HEADS // hb
    return pl.pallas_call(
        functools.partial(_mla_kernel, tq=tq, hb=hb),
        grid=(batch, ng, nq),
        in_specs=[pl.BlockSpec((tq, hb * MLA_QK_PAD), lambda b, g, i: (b * nq + i, g)),
                  pl.BlockSpec((seq, hb * MLA_NOPE_DIM), lambda b, g, i: (b, g)),
                  pl.BlockSpec((seq, LANE), lambda b, g, i: (b, 0)),
                  pl.BlockSpec((seq, hb * MLA_V_DIM), lambda b, g, i: (b, ng + g))],
        out_specs=pl.BlockSpec((tq, hb * MLA_V_DIM), lambda b, g, i: (b * nq + i, g)),
        out_shape=jax.ShapeDtypeStruct((batch * seq, MLA_OUT), BF16),
        scratch_shapes=[pltpu.VMEM((hb, tq, LANE), F32), pltpu.VMEM((hb, tq, LANE), F32),
                        pltpu.VMEM((hb, tq, MLA_V_DIM), F32)],
        compiler_params=_cparams(("parallel", "parallel", "arbitrary")),
        name="mla_attention",
    )(q, kv, kpe, kv)


def _relayout_w_in(w):
    d = w.shape[0]
    seg = lambda name: w[:, _NAT[name][0]:_NAT[name][0] + _NAT[name][1]]
    halves = lambda a: a.reshape(d, RET_HEADS, 2, RET_QK_DIM // 2).transpose(0, 2, 1, 3).reshape(d, -1)
    zeros = lambda n: jnp.zeros((d, n), w.dtype)
    order = sorted(_SEG, key=lambda s: _SEG[s][0])
    parts, pos = [], 0
    for name in order:
        off, width = _SEG[name]
        assert off == pos
        a = halves(seg(name)) if name in ("rq", "rk") else seg(name)
        parts.append(a)
        if a.shape[1] < width:
            parts.append(zeros(width - a.shape[1]))
        pos += width
    parts.append(zeros(PROJ_WIDTH - pos))
    return jnp.concatenate(parts, axis=1).astype(BF16)


def _relayout_w_uq(w):
    r = w.shape[0]
    a = w.reshape(r, MLA_HEADS, MLA_NOPE_DIM + MLA_ROPE_DIM)
    a = jnp.pad(a, ((0, 0), (0, 0), (0, MLA_QK_PAD - a.shape[2])))
    return a.reshape(r, MLA_HEADS * MLA_QK_PAD).astype(BF16)


def _relayout_w_ukv(w):
    r = w.shape[0]
    a = w.reshape(r, MLA_HEADS, 2, MLA_NOPE_DIM).transpose(0, 2, 1, 3)
    return a.reshape(r, 2 * MLA_HEADS * MLA_NOPE_DIM).astype(BF16)


def kernel(x, attn_norm, w_in, mla_q_norm, mla_kv_norm, w_uq, w_ukv, w_o, mlp_norm, w_up,
           w_down, final_norm):
    batch, seq, d_model = x.shape
    depth = w_in.shape[0]
    m = batch * seq
    tq = 256
    assert seq % tq == 0

    ret_tables = _retention_tables(seq)
    part_tables = _rope_tables(seq, PARTIAL_ROPE_THETA, PARTIAL_ROT_DIM)
    mla_tables = _rope_tables(seq, MLA_ROPE_THETA, MLA_ROPE_DIM)
    part_half, mla_half = PARTIAL_ROT_DIM // 2, MLA_ROPE_DIM // 2
    mla_scale = (MLA_NOPE_DIM + MLA_ROPE_DIM) ** -0.5

    def seg_rope(proj, name, rope_tiles, half, tables, scale):
        off, width = _SEG[name]
        return rope_cast(proj, tables, seq, col_off=off, width=width, rope_tiles=rope_tiles,
                         half=half, scale=scale)

    h = x.reshape(m, d_model)
    for l in range(depth):
        u = rmsnorm(h, attn_norm[l], BF16)
        proj = matmul(u, _relayout_w_in(w_in[l]), F32, name="mm_in")

        ret = retention(proj, batch, seq, ret_tables)

        qi = seg_rope(proj, "iq", range(IDX_HEADS), part_half, part_tables, IDX_HEAD_DIM ** -0.5)
        kidx = seg_rope(proj, "ik", (0,), part_half, part_tables, 1.0)
        dq = seg_rope(proj, "dq", range(DSA_HEADS), part_half, part_tables, DSA_HEAD_DIM ** -0.5)
        dk = seg_rope(proj, "dk", range(DSA_KV_HEADS), part_half, part_tables, 1.0)
        dv = seg_rope(proj, "dv", (), part_half, part_tables, 1.0)
        dsa = dsa_attention(qi, proj, kidx, dq, dk, dv, batch, seq, tq=tq)

        cq = rmsnorm(proj, mla_q_norm[l], BF16, col_off=_SEG["cq"][0], width=MLA_Q_LORA)
        ckv = rmsnorm(proj, mla_kv_norm[l], BF16, col_off=_SEG["ckv"][0], width=MLA_KV_LORA)
        q_up = matmul(cq, _relayout_w_uq(w_uq[l]), F32, name="mm_uq")
        kv_up = matmul(ckv, _relayout_w_ukv(w_ukv[l]), BF16, name="mm_ukv")
        q_full = rope_cast(q_up, mla_tables, seq, col_off=0, width=q_up.shape[1],
                           rope_tiles=range(1, 2 * MLA_HEADS, 2), half=mla_half, scale=mla_scale)
        kpe = seg_rope(proj, "kr", (0,), mla_half, mla_tables, 1.0)
        mla = mla_attention(q_full, kv_up, kpe, batch, seq)

        mixed = jnp.concatenate([ret, dsa, mla], axis=1)
        h = matmul(mixed, w_o[l].astype(BF16), F32, res=h, bn=512, name="mm_o")
        u = rmsnorm(h, mlp_norm[l], BF16)
        up = matmul(u, w_up[l].astype(BF16), BF16, act="relu2", name="mm_up")
        h = matmul(up, w_down[l].astype(BF16), F32, res=h, bn=512, name="mm_down")
    out = rmsnorm(h, final_norm, x.dtype)
    return out.reshape(batch, seq, d_model)
```

```python
import functools
import math

import numpy as np
import jax
import jax.numpy as jnp
from jax import lax
from jax.experimental import pallas as pl
from jax.experimental.pallas import tpu as pltpu

F32 = jnp.float32
BF16 = jnp.bfloat16

RET_HEADS = 8
RET_QK_DIM = 64
RET_V_DIM = 128
RET_CHUNK = 128
RET_ROPE_THETA = 10000.0
DSA_HEADS = 12
DSA_KV_HEADS = 4
DSA_HEAD_DIM = 128
IDX_HEADS = 32
IDX_HEAD_DIM = 128
DSA_TOPK = 256
PARTIAL_ROT_DIM = 32
PARTIAL_ROPE_THETA = 500000.0
MLA_HEADS = 12
MLA_Q_LORA = 768
MLA_KV_LORA = 256
MLA_NOPE_DIM = 128
MLA_ROPE_DIM = 64
MLA_V_DIM = 128
MLA_ROPE_THETA = 10000.0
NORM_EPS = 1e-6

LANE = 128
VMEM_LIMIT = 56 * 1024 * 1024

RET_OUT = RET_HEADS * RET_V_DIM
DSA_OUT = DSA_HEADS * DSA_HEAD_DIM
MLA_OUT = MLA_HEADS * MLA_V_DIM
MLA_QK_PAD = 256

_NAT = {}
_off = 0
for _name, _w in (("rq", 512), ("rk", 512), ("rv", 1024), ("rg", 1024), ("dq", 1536),
                  ("dk", 512), ("dv", 512), ("iq", 4096), ("ik", 128), ("iw", 32),
                  ("cq", 768), ("ckv", 256), ("kr", 64)):
    _NAT[_name] = (_off, _w)
    _off += _w
IN_WIDTH = _off
MAIN_WIDTH = 10240
MAIN_BN = 1024
_TAIL = {"cq": (0, 768), "ckv": (768, 256), "iw": (1024, 128), "kr": (1152, 128)}
TAIL_WIDTH = 1280
QI_BLOCK = 512


def _cparams(sem):
    return pltpu.CompilerParams(dimension_semantics=sem, vmem_limit_bytes=VMEM_LIMIT)


def _rmsnorm_kernel(x_ref, g_ref, o_ref):
    x = x_ref[...].astype(F32)
    ms = jnp.mean(x * x, axis=-1, keepdims=True)
    o_ref[...] = (x * lax.rsqrt(ms + NORM_EPS) * g_ref[...]).astype(o_ref.dtype)


def rmsnorm(x, g, out_dtype, *, col_off=0, width=None, bm=256):
    m = x.shape[0]
    width = x.shape[1] if width is None else width
    assert col_off % width == 0 and m % bm == 0
    cb = col_off // width
    return pl.pallas_call(
        _rmsnorm_kernel,
        grid=(m // bm,),
        in_specs=[pl.BlockSpec((bm, width), lambda i: (i, cb)),
                  pl.BlockSpec((1, width), lambda i: (0, 0))],
        out_specs=pl.BlockSpec((bm, width), lambda i: (i, 0)),
        out_shape=jax.ShapeDtypeStruct((m, width), out_dtype),
        compiler_params=_cparams(("parallel",)),
        name="rmsnorm",
    )(x, g.reshape(1, width).astype(F32))


def _mm_kernel(*refs, act, has_res, nk, groups, halves, ntab):
    x_ref, w_ref = refs[:2]
    pos = 2
    r_ref = refs[pos] if has_res else None
    pos += int(has_res)
    tab_refs = refs[pos:pos + 3 * ntab]
    pos += 3 * ntab
    o_ref = refs[pos]
    scratch = refs[pos + 1:]

    def store_tiles(acc, pattern):
        for t, (scale, ridx) in enumerate(pattern):
            x = acc[:, t * LANE:(t + 1) * LANE]
            if ridx is not None:
                a, bp, bm = (tab_refs[3 * ridx + k][...] for k in range(3))
                half = halves[ridx]
                x = x * a + pltpu.roll(x, half, 1) * bp + pltpu.roll(x, LANE - half, 1) * bm
            if scale != 1.0:
                x = x * scale
            o_ref[:, t * LANE:(t + 1) * LANE] = x.astype(o_ref.dtype)

    def epilogue(acc):
        if act == "relu2":
            acc = jnp.square(jnp.maximum(acc, 0.0))
        if has_res:
            acc = acc + r_ref[...]
        if groups is None:
            o_ref[...] = acc.astype(o_ref.dtype)
        elif len(groups) == 1:
            store_tiles(acc, groups[0][0])
        else:
            j = pl.program_id(1)
            for pattern, js in groups:
                cond = functools.reduce(jnp.logical_or, [j == jj for jj in js])
                pl.when(cond)(functools.partial(store_tiles, acc, pattern))

    part = jnp.dot(x_ref[...], w_ref[...], preferred_element_type=F32)
    if nk == 1:
        epilogue(part)
    else:
        acc_ref = scratch[0]
        k = pl.program_id(2)

        @pl.when(k == 0)
        def _():
            acc_ref[...] = part

        @pl.when(jnp.logical_and(k > 0, k < nk - 1))
        def _():
            acc_ref[...] += part

        @pl.when(k == nk - 1)
        def _():
            epilogue(acc_ref[...] + part)


def matmul(x, w, out_dtype, *, name, act=None, res=None, bm=1024, bn=1024, bk=4096,
           n_out=None, tile_modes=None, rope_tabs=(), seq=None):
    m, kdim = x.shape
    n = w.shape[1] if n_out is None else n_out
    bm, bn, bk = min(bm, m), min(bn, n), min(bk, kdim)
    assert m % bm == 0 and n % bn == 0 and kdim % bk == 0
    nk = kdim // bk
    in_specs = [pl.BlockSpec((bm, bk), lambda i, j, k: (i, k)),
                pl.BlockSpec((bk, bn), lambda i, j, k: (k, j))]
    args = [x, w]
    if res is not None:
        in_specs.append(pl.BlockSpec((bm, bn), lambda i, j, k: (i, j)))
        args.append(res)
    groups = None
    if tile_modes is not None:
        assert len(tile_modes) == n // LANE and seq % bm == 0
        per_block = bn // LANE
        by_pattern = {}
        for j in range(n // bn):
            by_pattern.setdefault(tuple(tile_modes[j * per_block:(j + 1) * per_block]), []).append(j)
        groups = tuple((p, tuple(js)) for p, js in by_pattern.items())
        nseq = seq // bm
        for a, bp, bmt, _ in rope_tabs:
            in_specs += [pl.BlockSpec((bm, LANE), lambda i, j, k: (i % nseq, 0))] * 3
            args += [a, bp, bmt]
    scratch = [pltpu.VMEM((bm, bn), F32)] if nk > 1 else []
    return pl.pallas_call(
        functools.partial(_mm_kernel, act=act, has_res=res is not None, nk=nk, groups=groups,
                          halves=tuple(t[3] for t in rope_tabs), ntab=len(rope_tabs)),
        grid=(m // bm, n // bn, nk),
        in_specs=in_specs,
        out_specs=pl.BlockSpec((bm, bn), lambda i, j, k: (i, j)),
        out_shape=jax.ShapeDtypeStruct((m, n), out_dtype),
        scratch_shapes=scratch,
        compiler_params=_cparams(("parallel", "parallel", "arbitrary")),
        name=name,
    )(*args)


def _rope_tables(seq, theta, rot_dim):
    half = rot_dim // 2
    inv = jnp.exp(-math.log(theta) * jnp.arange(half, dtype=F32) * (2.0 / rot_dim))
    ang = jnp.arange(seq, dtype=F32)[:, None] * inv[None, :]
    cos, sin = jnp.cos(ang), jnp.sin(ang)
    pad = LANE - rot_dim
    a = jnp.concatenate([cos, cos, jnp.ones((seq, pad), F32)], axis=1)
    bp = jnp.concatenate([jnp.zeros((seq, half), F32), sin, jnp.zeros((seq, pad), F32)], axis=1)
    bm = jnp.concatenate([-sin, jnp.zeros((seq, half + pad), F32)], axis=1)
    return a, bp, bm, half


def _retention_kernel(q_ref, k_ref, v_ref, g_ref, cos_ref, sp_ref, sm_ref, hmask_ref, idec_ref,
                      qdec_ref, kdec_ref, cdec_ref, bmask_ref, o_ref, state_ref):
    half = RET_QK_DIM // 2

    @pl.when(pl.program_id(1) == 0)
    def _():
        state_ref[...] = jnp.zeros_like(state_ref)

    cos, sp, sm = cos_ref[...], sp_ref[...], sm_ref[...]

    def rot(x):
        tiles = []
        for t in range(x.shape[1] // LANE):
            xt = x[:, t * LANE:(t + 1) * LANE]
            tiles.append(xt * cos + pltpu.roll(xt, half, 1) * sp
                         + pltpu.roll(xt, LANE - half, 1) * sm)
        return jnp.concatenate(tiles, axis=1)

    q = rot(q_ref[...].astype(F32))
    k = rot(k_ref[...].astype(F32)) * (RET_QK_DIM ** -0.5)
    v = v_ref[...].astype(F32)
    kt = k.T.astype(BF16)
    state = state_ref[...]
    cross = jnp.dot(q.astype(BF16), state.astype(BF16), preferred_element_type=F32) * qdec_ref[...]
    for h in range(RET_HEADS):
        cols = slice(h * RET_V_DIM, (h + 1) * RET_V_DIM)
        qm = (q * hmask_ref[h:h + 1, :]).astype(BF16)
        scores = jnp.dot(qm, kt, preferred_element_type=F32) * idec_ref[h]
        inner = jnp.dot(scores.astype(BF16), v[:, cols].astype(BF16), preferred_element_type=F32)
        o = inner + cross[:, cols]
        o = o * lax.rsqrt(jnp.mean(o * o, axis=-1, keepdims=True) + NORM_EPS)
        gate = g_ref[:, cols].astype(F32)
        o_ref[:, cols] = (gate * jax.nn.sigmoid(gate) * o).astype(o_ref.dtype)
    vd = (v * kdec_ref[...]).astype(BF16)
    upd = jnp.dot(kt, vd, preferred_element_type=F32)
    state_ref[...] = state * cdec_ref[...] + upd * bmask_ref[...]


def _retention_tables(seq):
    h, c, dv, dk = RET_HEADS, RET_CHUNK, RET_V_DIM, RET_QK_DIM
    half = dk // 2
    inv = jnp.exp(-math.log(RET_ROPE_THETA) * jnp.arange(half, dtype=F32) * (2.0 / dk))
    ang = jnp.arange(seq, dtype=F32)[:, None] * inv[None, :]
    cos1, sin1, zero = jnp.cos(ang), jnp.sin(ang), jnp.zeros((seq, half), F32)
    reps = LANE // dk
    cos = jnp.tile(jnp.concatenate([cos1, cos1], axis=1), (1, reps))
    sp = jnp.tile(jnp.concatenate([zero, sin1], axis=1), (1, reps))
    sm = jnp.tile(jnp.concatenate([-sin1, zero], axis=1), (1, reps))
    log_gamma = jnp.log1p(-jnp.exp2(-5.0 - jnp.arange(h, dtype=F32)))
    idx = jnp.arange(c, dtype=F32)
    rel = idx[:, None] - idx[None, :]
    idec = jnp.where(rel[None] >= 0,
                     jnp.exp(jnp.maximum(rel, 0.0)[None] * log_gamma[:, None, None]), 0.0)
    qdec = jnp.exp((idx + 1.0)[:, None] * log_gamma[None, :])
    kdec = jnp.exp((c - 1.0 - idx)[:, None] * log_gamma[None, :])
    cdec = jnp.exp(c * log_gamma)[None, :]
    rep = lambda a: jnp.repeat(a, dv, axis=1)
    lane_head = np.arange(h * dk) // dk
    hmask = jnp.asarray(lane_head[None, :] == np.arange(h)[:, None], F32)
    bmask = jnp.asarray(lane_head[:, None] == (np.arange(h * dv) // dv)[None, :], F32)
    return cos, sp, sm, hmask, idec, rep(qdec), rep(kdec), rep(cdec), bmask


def retention(proj, batch, seq, tables):
    c = RET_CHUNK
    nch = seq // c
    qw = RET_HEADS * RET_QK_DIM
    col = lambda name: _NAT[name][0] // _NAT[name][1]
    row = lambda b, n: b * nch + n
    const2 = lambda b, n: (0, 0)
    cos, sp, sm, hmask, idec, qdec, kdec, cdec, bmask = tables
    tab = pl.BlockSpec((c, LANE), lambda b, n: (n, 0))
    return pl.pallas_call(
        _retention_kernel,
        grid=(batch, nch),
        in_specs=[pl.BlockSpec((c, qw), lambda b, n: (row(b, n), col("rq"))),
                  pl.BlockSpec((c, qw), lambda b, n: (row(b, n), col("rk"))),
                  pl.BlockSpec((c, RET_OUT), lambda b, n: (row(b, n), col("rv"))),
                  pl.BlockSpec((c, RET_OUT), lambda b, n: (row(b, n), col("rg"))),
                  tab, tab, tab,
                  pl.BlockSpec(hmask.shape, const2),
                  pl.BlockSpec(idec.shape, lambda b, n: (0, 0, 0)),
                  pl.BlockSpec(qdec.shape, const2),
                  pl.BlockSpec(kdec.shape, const2),
                  pl.BlockSpec(cdec.shape, const2),
                  pl.BlockSpec(bmask.shape, const2)],
        out_specs=pl.BlockSpec((c, RET_OUT), lambda b, n: (row(b, n), 0)),
        out_shape=jax.ShapeDtypeStruct((batch * seq, RET_OUT), BF16),
        scratch_shapes=[pltpu.VMEM((qw, RET_OUT), F32)],
        compiler_params=_cparams(("parallel", "arbitrary")),
        name="retention",
    )(proj, proj, proj, proj, cos, sp, sm, hmask, idec, qdec, kdec, cdec, bmask)


def _flash_update(s, v_c, m_ref, l_ref, acc_ref, h, *, guard):
    reps = s.shape[1] // LANE

    def fold(x, op):
        tiles = [x[:, r * LANE:(r + 1) * LANE] for r in range(reps)]
        while len(tiles) > 1:
            tiles = [op(a, b) for a, b in zip(tiles[::2], tiles[1::2])]
        return tiles[0]

    m_prev = m_ref[h]
    m_new = jnp.maximum(m_prev, jnp.max(fold(s, jnp.maximum), axis=1, keepdims=True))
    m_sub = jnp.where(m_new == -jnp.inf, 0.0, m_new) if guard else m_new
    p = jnp.exp(s - jnp.tile(m_sub, (1, reps)))
    alpha = jnp.exp(m_prev - m_sub)
    l_ref[h] = alpha * l_ref[h] + jnp.sum(fold(p, jnp.add), axis=1, keepdims=True)
    m_ref[h] = m_new
    acc_ref[h] = acc_ref[h] * alpha + jnp.dot(p.astype(BF16), v_c, preferred_element_type=F32)


INT_MIN = -2 ** 31
IDX_GROUP = 8
ROW_BLOCK = 64
SEARCH_ROWS = 128
KV_PER_LOOP = 2


def _dsa_kernel(*refs, tq, topk, nqi):
    qi_refs = refs[:nqi]
    (w_ref, kidx_ref, q_ref, k_ref, v_ref, o_ref,
     qis_ref, wb_ref, lg_ref, key_ref, bias_ref, qs_ref, m_ref, l_ref, acc_ref) = refs[nqi:]
    tk = tq
    i = pl.program_id(1)
    nchunks = i + 1
    reps = tk // LANE
    ngroups = IDX_HEADS // IDX_GROUP
    group = DSA_HEADS // DSA_KV_HEADS
    d = DSA_HEAD_DIM
    heads_per_ref = IDX_HEADS // nqi

    w = w_ref[...].astype(F32) * (IDX_HEADS ** -0.5)
    for h in range(IDX_HEADS):
        r, t = divmod(h, heads_per_ref)
        qis_ref[h * tq:(h + 1) * tq, :] = qi_refs[r][:, t * LANE:(t + 1) * LANE]
        wb_ref[h] = jnp.broadcast_to(w[:, h:h + 1], (tq, LANE))

    def score_chunk(c, diagonal):
        r0 = pl.multiple_of(c * tk, tk)
        kc = kidx_ref[pl.ds(r0, tk), :]
        for g in range(ngroups):
            lg_ref[g] = lax.dot_general(
                qis_ref[g * IDX_GROUP * tq:(g + 1) * IDX_GROUP * tq, :], kc,
                (((1,), (1,)), ((), ())), preferred_element_type=F32)
        for rb in range(tq // ROW_BLOCK):
            q0 = rb * ROW_BLOCK
            rows = slice(q0, q0 + ROW_BLOCK)
            acc = jnp.zeros((ROW_BLOCK, tk), F32)
            for h in range(IDX_HEADS):
                g, hg = divmod(h, IDX_GROUP)
                lg = lg_ref[g, hg * tq + q0:hg * tq + q0 + ROW_BLOCK, :]
                acc = acc + jnp.maximum(lg, 0.0) * jnp.tile(wb_ref[h, rows, :], (1, reps))
            bits = pltpu.bitcast(acc, jnp.int32)
            key = bits ^ (lax.shift_right_arithmetic(bits, 31) & 0x7FFFFFFF)
            if diagonal:
                kpos = lax.broadcasted_iota(jnp.int32, (ROW_BLOCK, tk), 1)
                qpos = q0 + lax.broadcasted_iota(jnp.int32, (ROW_BLOCK, tk), 0)
                key = jnp.where(kpos <= qpos, key, INT_MIN)
            key_ref[c, rows, :] = key

    def score_body(c, _):
        score_chunk(c, False)
        return 0

    lax.fori_loop(0, i, score_body, 0)
    score_chunk(i, True)

    for rs in range(tq // SEARCH_ROWS):
        rows = slice(rs * SEARCH_ROWS, (rs + 1) * SEARCH_ROWS)

        def count_ge(cand, rows=rows):
            def body(c, cnt):
                for r in range(reps):
                    blk = key_ref[c, rows, r * LANE:(r + 1) * LANE]
                    cnt = cnt + (blk >= cand).astype(jnp.int32)
                return cnt
            cnt = lax.fori_loop(0, nchunks, body, jnp.zeros((SEARCH_ROWS, LANE), jnp.int32))
            return jnp.broadcast_to(jnp.sum(cnt, axis=1, keepdims=True), (SEARCH_ROWS, LANE))

        zero = jnp.zeros((SEARCH_ROWS, LANE), jnp.int32)
        n0 = count_ge(zero)
        t0 = jnp.where(n0 >= topk, zero, INT_MIN)
        c0 = jnp.where(n0 >= topk, n0, nchunks * tk)

        def bit_cond(state):
            b, _, cnt = state
            return jnp.logical_and(b < 31, jnp.max(jnp.abs(cnt - topk)) > 0)

        def bit_body(state, count_ge=count_ge):
            b, t, cnt = state
            cand = t | lax.shift_left(jnp.int32(1), 30 - b)
            n = count_ge(cand)
            take = n >= topk
            return b + 1, jnp.where(take, cand, t), jnp.where(take, n, cnt)

        _, thr, _ = lax.while_loop(bit_cond, bit_body, (jnp.int32(0), t0, c0))
        thr = jnp.maximum(thr, INT_MIN + 1)
        thr = jnp.tile(thr, (1, reps))

        def bias_body(c, _, rows=rows, thr=thr):
            bias_ref[c, rows, :] = jnp.where(key_ref[c, rows, :] >= thr, 0.0, -jnp.inf)
            return 0

        lax.fori_loop(0, nchunks, bias_body, 0)

    @pl.when(nchunks < key_ref.shape[0])
    def _():
        bias_ref[nchunks] = jnp.full((tq, tk), -jnp.inf, F32)

    hpl = KV_PER_LOOP * group
    for g0 in range(0, DSA_KV_HEADS, KV_PER_LOOP):
        for hh in range(hpl):
            h = g0 * group + hh
            qs_ref[hh * tq:(hh + 1) * tq, :] = q_ref[:, h * d:(h + 1) * d]
        m_ref[...] = jnp.full_like(m_ref, -jnp.inf)
        l_ref[...] = jnp.zeros_like(l_ref)
        acc_ref[...] = jnp.zeros_like(acc_ref)

        def att_body(c2, _, g0=g0):
            r0 = pl.multiple_of(c2 * 2 * tk, 2 * tk)
            bias = jnp.concatenate([bias_ref[2 * c2], bias_ref[2 * c2 + 1]], axis=1)
            for gg in range(KV_PER_LOOP):
                g = g0 + gg
                kc = k_ref[pl.ds(r0, 2 * tk), g * d:(g + 1) * d]
                vc = v_ref[pl.ds(r0, 2 * tk), g * d:(g + 1) * d]
                s_all = lax.dot_general(qs_ref[gg * group * tq:(gg + 1) * group * tq, :], kc,
                                        (((1,), (1,)), ((), ())), preferred_element_type=F32)
                for hh in range(group):
                    _flash_update(s_all[hh * tq:(hh + 1) * tq] + bias, vc, m_ref, l_ref, acc_ref,
                                  gg * group + hh, guard=True)
            return 0

        lax.fori_loop(0, (nchunks + 1) // 2, att_body, 0)
        for hh in range(hpl):
            h = g0 * group + hh
            o_ref[:, h * d:(h + 1) * d] = (acc_ref[hh] / l_ref[hh]).astype(o_ref.dtype)


def dsa_attention(proj, tail, batch, seq, *, tq=256):
    nq = seq // tq
    assert nq % 2 == 0
    topk = min(DSA_TOPK, seq // 4)
    hpl = KV_PER_LOOP * DSA_HEADS // DSA_KV_HEADS
    kvw = DSA_KV_HEADS * DSA_HEAD_DIM
    iqw = IDX_HEADS * IDX_HEAD_DIM
    nqi = iqw // QI_BLOCK
    col = lambda name, width: _NAT[name][0] // width
    qi_specs = [pl.BlockSpec((tq, QI_BLOCK), lambda b, i, r=r: (b * nq + i, col("iq", QI_BLOCK) + r))
                for r in range(nqi)]
    return pl.pallas_call(
        functools.partial(_dsa_kernel, tq=tq, topk=topk, nqi=nqi),
        grid=(batch, nq),
        in_specs=qi_specs + [
            pl.BlockSpec((tq, LANE), lambda b, i: (b * nq + i, _TAIL["iw"][0] // LANE)),
            pl.BlockSpec((seq, IDX_HEAD_DIM), lambda b, i: (b, col("ik", IDX_HEAD_DIM))),
            pl.BlockSpec((tq, DSA_OUT), lambda b, i: (b * nq + i, col("dq", DSA_OUT))),
            pl.BlockSpec((seq, kvw), lambda b, i: (b, col("dk", kvw))),
            pl.BlockSpec((seq, kvw), lambda b, i: (b, col("dv", kvw)))],
        out_specs=pl.BlockSpec((tq, DSA_OUT), lambda b, i: (b * nq + i, 0)),
        out_shape=jax.ShapeDtypeStruct((batch * seq, DSA_OUT), BF16),
        scratch_shapes=[pltpu.VMEM((IDX_HEADS * tq, IDX_HEAD_DIM), BF16),
                        pltpu.VMEM((IDX_HEADS, tq, LANE), F32),
                        pltpu.VMEM((IDX_HEADS // IDX_GROUP, IDX_GROUP * tq, tq), F32),
                        pltpu.VMEM((nq, tq, tq), jnp.int32),
                        pltpu.VMEM((nq, tq, tq), F32),
                        pltpu.VMEM((hpl * tq, DSA_HEAD_DIM), BF16),
                        pltpu.VMEM((hpl, tq, LANE), F32),
                        pltpu.VMEM((hpl, tq, LANE), F32),
                        pltpu.VMEM((hpl, tq, DSA_HEAD_DIM), F32)],
        compiler_params=_cparams(("parallel", "arbitrary")),
        name="dsa_attention",
    )(*([proj] * nqi), tail, proj, proj, proj, proj)


def _mla_kernel(q_ref, kn_ref, kpe_ref, v_ref, o_ref, m_ref, l_ref, acc_ref, *, tq, hb):
    tk = tq
    i = pl.program_id(2)
    m_ref[...] = jnp.full_like(m_ref, -jnp.inf)
    l_ref[...] = jnp.zeros_like(l_ref)
    acc_ref[...] = jnp.zeros_like(acc_ref)
    causal = (lax.broadcasted_iota(jnp.int32, (tq, tk), 1)
              <= lax.broadcasted_iota(jnp.int32, (tq, tk), 0))

    def step(c, masked):
        r0 = pl.multiple_of(c * tk, tk)
        kpe = kpe_ref[pl.ds(r0, tk), :]
        for h in range(hb):
            cols = slice(h * MLA_NOPE_DIM, (h + 1) * MLA_NOPE_DIM)
            kf = jnp.concatenate([kn_ref[pl.ds(r0, tk), cols], kpe], axis=1)
            s = lax.dot_general(q_ref[:, h * MLA_QK_PAD:(h + 1) * MLA_QK_PAD], kf,
                                (((1,), (1,)), ((), ())), preferred_element_type=F32)
            if masked:
                s = jnp.where(causal, s, -jnp.inf)
            _flash_update(s, v_ref[pl.ds(r0, tk), cols], m_ref, l_ref, acc_ref, h, guard=False)

    def body(c, _):
        step(c, False)
        return 0

    lax.fori_loop(0, i, body, 0)
    step(i, True)
    for h in range(hb):
        o_ref[:, h * MLA_V_DIM:(h + 1) * MLA_V_DIM] = (acc_ref[h] / l_ref[h]).astype(o_ref.dtype)


def mla_attention(q, kv, tail, batch, seq, *, tq=512, hb=2):
    nq = seq // tq
    ng = MLA_HEADS // hb
    return pl.pallas_call(
        functools.partial(_mla_kernel, tq=tq, hb=hb),
        grid=(batch, ng, nq),
        in_specs=[pl.BlockSpec((tq, hb * MLA_QK_PAD), lambda b, g, i: (b * nq + i, g)),
                  pl.BlockSpec((seq, hb * MLA_NOPE_DIM), lambda b, g, i: (b, g)),
                  pl.BlockSpec((seq, LANE), lambda b, g, i: (b, _TAIL["kr"][0] // LANE)),
                  pl.BlockSpec((seq, hb * MLA_V_DIM), lambda b, g, i: (b, ng + g))],
        out_specs=pl.BlockSpec((tq, hb * MLA_V_DIM), lambda b, g, i: (b * nq + i, g)),
        out_shape=jax.ShapeDtypeStruct((batch * seq, MLA_OUT), BF16),
        scratch_shapes=[pltpu.VMEM((hb, tq, LANE), F32), pltpu.VMEM((hb, tq, LANE), F32),
                        pltpu.VMEM((hb, tq, MLA_V_DIM), F32)],
        compiler_params=_cparams(("parallel", "parallel", "arbitrary")),
        name="mla_attention",
    )(q, kv, tail, kv)


def _tail_w_in(w):
    d = w.shape[0]
    parts, pos = [], 0
    for name in sorted(_TAIL, key=lambda s: _TAIL[s][0]):
        off, width = _TAIL[name]
        assert off == pos
        a = w[:, _NAT[name][0]:_NAT[name][0] + _NAT[name][1]]
        parts.append(a)
        if a.shape[1] < width:
            parts.append(jnp.zeros((d, width - a.shape[1]), w.dtype))
        pos += width
    assert pos == TAIL_WIDTH
    return jnp.concatenate(parts, axis=1).astype(BF16)


def _relayout_w_uq(w):
    r = w.shape[0]
    a = w.reshape(r, MLA_HEADS, MLA_NOPE_DIM + MLA_ROPE_DIM)
    a = jnp.pad(a, ((0, 0), (0, 0), (0, MLA_QK_PAD - a.shape[2])))
    return a.reshape(r, MLA_HEADS * MLA_QK_PAD).astype(BF16)


def _relayout_w_ukv(w):
    r = w.shape[0]
    a = w.reshape(r, MLA_HEADS, 2, MLA_NOPE_DIM).transpose(0, 2, 1, 3)
    return a.reshape(r, 2 * MLA_HEADS * MLA_NOPE_DIM).astype(BF16)


def _main_tile_modes():
    modes = [(1.0, None)] * (MAIN_WIDTH // LANE)

    def fill(name, mode):
        off, width = _NAT[name]
        for t in range(off // LANE, (off + width) // LANE):
            modes[t] = mode

    fill("dq", (DSA_HEAD_DIM ** -0.5, 0))
    fill("dk", (1.0, 0))
    fill("iq", (IDX_HEAD_DIM ** -0.5, 0))
    fill("ik", (1.0, 0))
    return modes


def _tail_tile_modes():
    modes = [(1.0, None)] * (TAIL_WIDTH // LANE)
    modes[_TAIL["kr"][0] // LANE] = (1.0, 0)
    return modes


def kernel(x, attn_norm, w_in, mla_q_norm, mla_kv_norm, w_uq, w_ukv, w_o, mlp_norm, w_up,
           w_down, final_norm):
    batch, seq, d_model = x.shape
    depth = w_in.shape[0]
    m = batch * seq

    ret_tables = _retention_tables(seq)
    part_tab = _rope_tables(seq, PARTIAL_ROPE_THETA, PARTIAL_ROT_DIM)
    mla_tab = _rope_tables(seq, MLA_ROPE_THETA, MLA_ROPE_DIM)
    mla_scale = (MLA_NOPE_DIM + MLA_ROPE_DIM) ** -0.5
    uq_modes = [(mla_scale, None), (mla_scale, 0)] * MLA_HEADS

    h = x.reshape(m, d_model)
    for l in range(depth):
        u = rmsnorm(h, attn_norm[l], BF16)
        proj = matmul(u, w_in[l].astype(BF16), BF16, name="mm_in", n_out=MAIN_WIDTH, bn=MAIN_BN,
                      tile_modes=_main_tile_modes(), rope_tabs=(part_tab,), seq=seq)
        tail = matmul(u, _tail_w_in(w_in[l]), BF16, name="mm_in_tail", bn=TAIL_WIDTH,
                      tile_modes=_tail_tile_modes(), rope_tabs=(mla_tab,), seq=seq)

        ret = retention(proj, batch, seq, ret_tables)
        dsa = dsa_attention(proj, tail, batch, seq)

        cq = rmsnorm(tail, mla_q_norm[l], BF16, col_off=_TAIL["cq"][0], width=MLA_Q_LORA)
        ckv = rmsnorm(tail, mla_kv_norm[l], BF16, col_off=_TAIL["ckv"][0], width=MLA_KV_LORA)
        q_full = matmul(cq, _relayout_w_uq(w_uq[l]), BF16, name="mm_uq", tile_modes=uq_modes,
                        rope_tabs=(mla_tab,), seq=seq)
        kv_up = matmul(ckv, _relayout_w_ukv(w_ukv[l]), BF16, name="mm_ukv")
        mla = mla_attention(q_full, kv_up, tail, batch, seq)

        mixed = jnp.concatenate([ret, dsa, mla], axis=1)
        h = matmul(mixed, w_o[l].astype(BF16), F32, res=h, bn=512, name="mm_o")
        u = rmsnorm(h, mlp_norm[l], BF16)
        up = matmul(u, w_up[l].astype(BF16), BF16, act="relu2", name="mm_up")
        h = matmul(up, w_down[l].astype(BF16), F32, res=h, bn=512, name="mm_down")
    out = rmsnorm(h, final_norm, x.dtype)
    return out.reshape(batch, seq, d_model)
```

```python
import functools
import math

import numpy as np
import jax
import jax.numpy as jnp
from jax import lax
from jax.experimental import pallas as pl
from jax.experimental.pallas import tpu as pltpu

F32 = jnp.float32
BF16 = jnp.bfloat16

RET_HEADS = 8
RET_QK_DIM = 64
RET_V_DIM = 128
RET_CHUNK = 128
RET_ROPE_THETA = 10000.0
DSA_HEADS = 12
DSA_KV_HEADS = 4
DSA_HEAD_DIM = 128
IDX_HEADS = 32
IDX_HEAD_DIM = 128
DSA_TOPK = 256
PARTIAL_ROT_DIM = 32
PARTIAL_ROPE_THETA = 500000.0
MLA_HEADS = 12
MLA_Q_LORA = 768
MLA_KV_LORA = 256
MLA_NOPE_DIM = 128
MLA_ROPE_DIM = 64
MLA_V_DIM = 128
MLA_ROPE_THETA = 10000.0
NORM_EPS = 1e-6

LANE = 128
VMEM_LIMIT = 56 * 1024 * 1024

RET_OUT = RET_HEADS * RET_V_DIM
DSA_OUT = DSA_HEADS * DSA_HEAD_DIM
MLA_OUT = MLA_HEADS * MLA_V_DIM
MLA_QK_PAD = 256

_NAT = {}
_off = 0
for _name, _w in (("rq", 512), ("rk", 512), ("rv", 1024), ("rg", 1024), ("dq", 1536),
                  ("dk", 512), ("dv", 512), ("iq", 4096), ("ik", 128), ("iw", 32),
                  ("cq", 768), ("ckv", 256), ("kr", 64)):
    _NAT[_name] = (_off, _w)
    _off += _w
IN_WIDTH = _off
MAIN_WIDTH = 10240
MAIN_BN = 1024
_TAIL = {"cq": (0, 768), "ckv": (768, 256), "iw": (1024, 128), "kr": (1152, 128)}
TAIL_WIDTH = 1280
QI_BLOCK = 512


def _cparams(sem):
    return pltpu.CompilerParams(dimension_semantics=sem, vmem_limit_bytes=VMEM_LIMIT)


def _rmsnorm_kernel(x_ref, g_ref, o_ref):
    x = x_ref[...].astype(F32)
    ms = jnp.mean(x * x, axis=-1, keepdims=True)
    o_ref[...] = (x * lax.rsqrt(ms + NORM_EPS) * g_ref[...]).astype(o_ref.dtype)


def rmsnorm(x, g, out_dtype, *, col_off=0, width=None, bm=256):
    m = x.shape[0]
    width = x.shape[1] if width is None else width
    assert col_off % width == 0 and m % bm == 0
    cb = col_off // width
    return pl.pallas_call(
        _rmsnorm_kernel,
        grid=(m // bm,),
        in_specs=[pl.BlockSpec((bm, width), lambda i: (i, cb)),
                  pl.BlockSpec((1, width), lambda i: (0, 0))],
        out_specs=pl.BlockSpec((bm, width), lambda i: (i, 0)),
        out_shape=jax.ShapeDtypeStruct((m, width), out_dtype),
        compiler_params=_cparams(("parallel",)),
        name="rmsnorm",
    )(x, g.reshape(1, width).astype(F32))


def _mm_kernel(*refs, nx, act, has_res, nk, groups, halves, ntab):
    x_refs, w_ref = refs[:nx], refs[nx]
    pos = nx + 1
    r_ref = refs[pos] if has_res else None
    pos += int(has_res)
    tab_refs = refs[pos:pos + 3 * ntab]
    pos += 3 * ntab
    o_ref = refs[pos]
    scratch = refs[pos + 1:]

    def store_tiles(acc, pattern):
        for t, (scale, ridx) in enumerate(pattern):
            x = acc[:, t * LANE:(t + 1) * LANE]
            if ridx is not None:
                a, bp, bm = (tab_refs[3 * ridx + k][...] for k in range(3))
                half = halves[ridx]
                x = x * a + pltpu.roll(x, half, 1) * bp + pltpu.roll(x, LANE - half, 1) * bm
            if scale != 1.0:
                x = x * scale
            o_ref[:, t * LANE:(t + 1) * LANE] = x.astype(o_ref.dtype)

    def epilogue(acc):
        if act == "relu2":
            acc = jnp.square(jnp.maximum(acc, 0.0))
        if has_res:
            acc = acc + r_ref[...]
        if groups is None:
            o_ref[...] = acc.astype(o_ref.dtype)
        elif len(groups) == 1:
            store_tiles(acc, groups[0][0])
        else:
            j = pl.program_id(1)
            for pattern, js in groups:
                cond = functools.reduce(jnp.logical_or, [j == jj for jj in js])
                pl.when(cond)(functools.partial(store_tiles, acc, pattern))

    part, k0 = None, 0
    for x_ref in x_refs:
        kw = x_ref.shape[1]
        d = jnp.dot(x_ref[...], w_ref[k0:k0 + kw, :], preferred_element_type=F32)
        part = d if part is None else part + d
        k0 += kw
    if nk == 1:
        epilogue(part)
    else:
        acc_ref = scratch[0]
        k = pl.program_id(2)

        @pl.when(k == 0)
        def _():
            acc_ref[...] = part

        @pl.when(jnp.logical_and(k > 0, k < nk - 1))
        def _():
            acc_ref[...] += part

        @pl.when(k == nk - 1)
        def _():
            epilogue(acc_ref[...] + part)


def matmul(x, w, out_dtype, *, name, layer=None, act=None, res=None, bm=1024, bn=1024, bk=4096,
           n_out=None, tile_modes=None, rope_tabs=(), seq=None):
    xs = x if isinstance(x, (tuple, list)) else (x,)
    m = xs[0].shape[0]
    kdim = sum(a.shape[1] for a in xs)
    assert kdim == w.shape[-2]
    n = w.shape[-1] if n_out is None else n_out
    bm, bn, bk = min(bm, m), min(bn, n), min(bk, kdim)
    assert m % bm == 0 and n % bn == 0 and kdim % bk == 0
    nk = kdim // bk
    assert len(xs) == 1 or nk == 1
    if len(xs) == 1:
        in_specs = [pl.BlockSpec((bm, bk), lambda i, j, k: (i, k))]
    else:
        in_specs = [pl.BlockSpec((bm, a.shape[1]), lambda i, j, k: (i, 0)) for a in xs]
    if layer is None:
        in_specs.append(pl.BlockSpec((bk, bn), lambda i, j, k: (k, j)))
    else:
        in_specs.append(pl.BlockSpec((None, bk, bn), lambda i, j, k: (layer, k, j)))
    args = [*xs, w]
    if res is not None:
        in_specs.append(pl.BlockSpec((bm, bn), lambda i, j, k: (i, j)))
        args.append(res)
    groups = None
    if tile_modes is not None:
        assert len(tile_modes) == n // LANE and seq % bm == 0
        per_block = bn // LANE
        by_pattern = {}
        for j in range(n // bn):
            by_pattern.setdefault(tuple(tile_modes[j * per_block:(j + 1) * per_block]), []).append(j)
        groups = tuple((p, tuple(js)) for p, js in by_pattern.items())
        nseq = seq // bm
        for a, bp, bmt, _ in rope_tabs:
            in_specs += [pl.BlockSpec((bm, LANE), lambda i, j, k: (i % nseq, 0))] * 3
            args += [a, bp, bmt]
    scratch = [pltpu.VMEM((bm, bn), F32)] if nk > 1 else []
    return pl.pallas_call(
        functools.partial(_mm_kernel, nx=len(xs), act=act, has_res=res is not None, nk=nk,
                          groups=groups,
                          halves=tuple(t[3] for t in rope_tabs), ntab=len(rope_tabs)),
        grid=(m // bm, n // bn, nk),
        in_specs=in_specs,
        out_specs=pl.BlockSpec((bm, bn), lambda i, j, k: (i, j)),
        out_shape=jax.ShapeDtypeStruct((m, n), out_dtype),
        scratch_shapes=scratch,
        compiler_params=_cparams(("parallel", "parallel", "arbitrary")),
        name=name,
    )(*args)


def _rope_tables(seq, theta, rot_dim):
    half = rot_dim // 2
    inv = jnp.exp(-math.log(theta) * jnp.arange(half, dtype=F32) * (2.0 / rot_dim))
    ang = jnp.arange(seq, dtype=F32)[:, None] * inv[None, :]
    cos, sin = jnp.cos(ang), jnp.sin(ang)
    pad = LANE - rot_dim
    a = jnp.concatenate([cos, cos, jnp.ones((seq, pad), F32)], axis=1)
    bp = jnp.concatenate([jnp.zeros((seq, half), F32), sin, jnp.zeros((seq, pad), F32)], axis=1)
    bm = jnp.concatenate([-sin, jnp.zeros((seq, half + pad), F32)], axis=1)
    return a, bp, bm, half


def _retention_kernel(q_ref, k_ref, v_ref, g_ref, cos_ref, sp_ref, sm_ref, hmask_ref, idec_ref,
                      qdec_ref, kdec_ref, cdec_ref, bmask_ref, o_ref, state_ref):
    half = RET_QK_DIM // 2

    @pl.when(pl.program_id(1) == 0)
    def _():
        state_ref[...] = jnp.zeros_like(state_ref)

    cos, sp, sm = cos_ref[...], sp_ref[...], sm_ref[...]

    def rot(x):
        tiles = []
        for t in range(x.shape[1] // LANE):
            xt = x[:, t * LANE:(t + 1) * LANE]
            tiles.append(xt * cos + pltpu.roll(xt, half, 1) * sp
                         + pltpu.roll(xt, LANE - half, 1) * sm)
        return jnp.concatenate(tiles, axis=1)

    q = rot(q_ref[...].astype(F32))
    k = rot(k_ref[...].astype(F32)) * (RET_QK_DIM ** -0.5)
    v = v_ref[...].astype(F32)
    kt = k.T.astype(BF16)
    state = state_ref[...]
    cross = jnp.dot(q.astype(BF16), state.astype(BF16), preferred_element_type=F32) * qdec_ref[...]
    for h in range(RET_HEADS):
        cols = slice(h * RET_V_DIM, (h + 1) * RET_V_DIM)
        qm = (q * hmask_ref[h:h + 1, :]).astype(BF16)
        scores = jnp.dot(qm, kt, preferred_element_type=F32) * idec_ref[h]
        inner = jnp.dot(scores.astype(BF16), v[:, cols].astype(BF16), preferred_element_type=F32)
        o = inner + cross[:, cols]
        o = o * lax.rsqrt(jnp.mean(o * o, axis=-1, keepdims=True) + NORM_EPS)
        gate = g_ref[:, cols].astype(F32)
        o_ref[:, cols] = (gate * jax.nn.sigmoid(gate) * o).astype(o_ref.dtype)
    vd = (v * kdec_ref[...]).astype(BF16)
    upd = jnp.dot(kt, vd, preferred_element_type=F32)
    state_ref[...] = state * cdec_ref[...] + upd * bmask_ref[...]


def _retention_tables(seq):
    h, c, dv, dk = RET_HEADS, RET_CHUNK, RET_V_DIM, RET_QK_DIM
    half = dk // 2
    inv = jnp.exp(-math.log(RET_ROPE_THETA) * jnp.arange(half, dtype=F32) * (2.0 / dk))
    ang = jnp.arange(seq, dtype=F32)[:, None] * inv[None, :]
    cos1, sin1, zero = jnp.cos(ang), jnp.sin(ang), jnp.zeros((seq, half), F32)
    reps = LANE // dk
    cos = jnp.tile(jnp.concatenate([cos1, cos1], axis=1), (1, reps))
    sp = jnp.tile(jnp.concatenate([zero, sin1], axis=1), (1, reps))
    sm = jnp.tile(jnp.concatenate([-sin1, zero], axis=1), (1, reps))
    log_gamma = jnp.log1p(-jnp.exp2(-5.0 - jnp.arange(h, dtype=F32)))
    idx = jnp.arange(c, dtype=F32)
    rel = idx[:, None] - idx[None, :]
    idec = jnp.where(rel[None] >= 0,
                     jnp.exp(jnp.maximum(rel, 0.0)[None] * log_gamma[:, None, None]), 0.0)
    qdec = jnp.exp((idx + 1.0)[:, None] * log_gamma[None, :])
    kdec = jnp.exp((c - 1.0 - idx)[:, None] * log_gamma[None, :])
    cdec = jnp.exp(c * log_gamma)[None, :]
    rep = lambda a: jnp.repeat(a, dv, axis=1)
    lane_head = np.arange(h * dk) // dk
    hmask = jnp.asarray(lane_head[None, :] == np.arange(h)[:, None], F32)
    bmask = jnp.asarray(lane_head[:, None] == (np.arange(h * dv) // dv)[None, :], F32)
    return cos, sp, sm, hmask, idec, rep(qdec), rep(kdec), rep(cdec), bmask


def retention(proj, batch, seq, tables):
    c = RET_CHUNK
    nch = seq // c
    qw = RET_HEADS * RET_QK_DIM
    col = lambda name: _NAT[name][0] // _NAT[name][1]
    row = lambda b, n: b * nch + n
    const2 = lambda b, n: (0, 0)
    cos, sp, sm, hmask, idec, qdec, kdec, cdec, bmask = tables
    tab = pl.BlockSpec((c, LANE), lambda b, n: (n, 0))
    return pl.pallas_call(
        _retention_kernel,
        grid=(batch, nch),
        in_specs=[pl.BlockSpec((c, qw), lambda b, n: (row(b, n), col("rq"))),
                  pl.BlockSpec((c, qw), lambda b, n: (row(b, n), col("rk"))),
                  pl.BlockSpec((c, RET_OUT), lambda b, n: (row(b, n), col("rv"))),
                  pl.BlockSpec((c, RET_OUT), lambda b, n: (row(b, n), col("rg"))),
                  tab, tab, tab,
                  pl.BlockSpec(hmask.shape, const2),
                  pl.BlockSpec(idec.shape, lambda b, n: (0, 0, 0)),
                  pl.BlockSpec(qdec.shape, const2),
                  pl.BlockSpec(kdec.shape, const2),
                  pl.BlockSpec(cdec.shape, const2),
                  pl.BlockSpec(bmask.shape, const2)],
        out_specs=pl.BlockSpec((c, RET_OUT), lambda b, n: (row(b, n), 0)),
        out_shape=jax.ShapeDtypeStruct((batch * seq, RET_OUT), BF16),
        scratch_shapes=[pltpu.VMEM((qw, RET_OUT), F32)],
        compiler_params=_cparams(("parallel", "arbitrary")),
        name="retention",
    )(proj, proj, proj, proj, cos, sp, sm, hmask, idec, qdec, kdec, cdec, bmask)


def _flash_update(s, v_c, m_ref, l_ref, acc_ref, h, *, guard):
    reps = s.shape[1] // LANE

    def fold(x, op):
        tiles = [x[:, r * LANE:(r + 1) * LANE] for r in range(reps)]
        while len(tiles) > 1:
            tiles = [op(a, b) for a, b in zip(tiles[::2], tiles[1::2])]
        return tiles[0]

    m_prev = m_ref[h]
    m_new = jnp.maximum(m_prev, jnp.max(fold(s, jnp.maximum), axis=1, keepdims=True))
    m_sub = jnp.where(m_new == -jnp.inf, 0.0, m_new) if guard else m_new
    p = jnp.exp(s - jnp.tile(m_sub, (1, reps)))
    alpha = jnp.exp(m_prev - m_sub)
    l_ref[h] = alpha * l_ref[h] + jnp.sum(fold(p, jnp.add), axis=1, keepdims=True)
    m_ref[h] = m_new
    acc_ref[h] = acc_ref[h] * alpha + jnp.dot(p.astype(BF16), v_c, preferred_element_type=F32)


INT_MIN = -2 ** 31
IDX_GROUP = 8
ROW_BLOCK = 64
SEARCH_ROWS = 128
PASSES_PER_CHECK = 4
KV_PER_LOOP = 2


def _dsa_kernel(*refs, tq, topk, nqi):
    qi_refs = refs[:nqi]
    (w_ref, kidx_ref, q_ref, k_ref, v_ref, o_ref,
     qis_ref, wb_ref, lg_ref, key_ref, bias_ref, qs_ref, m_ref, l_ref, acc_ref) = refs[nqi:]
    tk = tq
    i = pl.program_id(1)
    nchunks = i + 1
    reps = tk // LANE
    ngroups = IDX_HEADS // IDX_GROUP
    group = DSA_HEADS // DSA_KV_HEADS
    d = DSA_HEAD_DIM
    heads_per_ref = IDX_HEADS // nqi

    w = w_ref[...].astype(F32) * (IDX_HEADS ** -0.5)
    for h in range(IDX_HEADS):
        r, t = divmod(h, heads_per_ref)
        qis_ref[h * tq:(h + 1) * tq, :] = qi_refs[r][:, t * LANE:(t + 1) * LANE]
        wb_ref[h] = jnp.broadcast_to(w[:, h:h + 1], (tq, LANE))

    def score_chunk(c, diagonal):
        r0 = pl.multiple_of(c * tk, tk)
        kc = kidx_ref[pl.ds(r0, tk), :]
        for g in range(ngroups):
            lg_ref[g] = lax.dot_general(
                qis_ref[g * IDX_GROUP * tq:(g + 1) * IDX_GROUP * tq, :], kc,
                (((1,), (1,)), ((), ())), preferred_element_type=F32)
        for rb in range(tq // ROW_BLOCK):
            q0 = rb * ROW_BLOCK
            rows = slice(q0, q0 + ROW_BLOCK)
            acc = jnp.zeros((ROW_BLOCK, tk), F32)
            for h in range(IDX_HEADS):
                g, hg = divmod(h, IDX_GROUP)
                lg = lg_ref[g, hg * tq + q0:hg * tq + q0 + ROW_BLOCK, :]
                acc = acc + jnp.maximum(lg, 0.0) * jnp.tile(wb_ref[h, rows, :], (1, reps))
            bits = pltpu.bitcast(acc, jnp.int32)
            key = bits ^ (lax.shift_right_arithmetic(bits, 31) & 0x7FFFFFFF)
            if diagonal:
                kpos = lax.broadcasted_iota(jnp.int32, (ROW_BLOCK, tk), 1)
                qpos = q0 + lax.broadcasted_iota(jnp.int32, (ROW_BLOCK, tk), 0)
                key = jnp.where(kpos <= qpos, key, INT_MIN)
            key_ref[c, rows, :] = key

    def score_body(c, _):
        score_chunk(c, False)
        return 0

    lax.fori_loop(0, i, score_body, 0)
    score_chunk(i, True)

    nslab = tq // SEARCH_ROWS

    def count_ge(cand, s):
        rows = slice(s * SEARCH_ROWS, (s + 1) * SEARCH_ROWS)

        def body(c, cnt):
            for r in range(reps):
                blk = key_ref[c, rows, r * LANE:(r + 1) * LANE]
                cnt = cnt + (blk >= cand).astype(jnp.int32)
            return cnt
        return lax.fori_loop(0, nchunks, body, jnp.zeros((SEARCH_ROWS, LANE), jnp.int32))

    def bit_cond(state):
        b, _, cnts = state
        off = functools.reduce(jnp.maximum, [jnp.abs(c - topk) for c in cnts])
        return jnp.logical_and(b < 32, jnp.max(off) > 0)

    def bit_body(state):
        b, ts, cnts = state
        ts, cnts = list(ts), list(cnts)
        for u in range(PASSES_PER_CHECK):
            bit = lax.shift_left(jnp.int32(1), 31 - (b + u))
            cands = [t ^ bit for t in ts]
            partial = [count_ge(cands[s], s) for s in range(nslab)]
            for s in range(nslab):
                n = jnp.broadcast_to(jnp.sum(partial[s], axis=1, keepdims=True),
                                     (SEARCH_ROWS, LANE))
                take = n >= topk
                ts[s], cnts[s] = jnp.where(take, cands[s], ts[s]), jnp.where(take, n, cnts[s])
        return b + PASSES_PER_CHECK, tuple(ts), tuple(cnts)

    t0 = (jnp.full((SEARCH_ROWS, LANE), INT_MIN, jnp.int32),) * nslab
    c0 = (jnp.full((SEARCH_ROWS, LANE), nchunks * tk, jnp.int32),) * nslab
    _, thrs, _ = lax.while_loop(bit_cond, bit_body, (jnp.int32(0), t0, c0))
    thr = jnp.concatenate(thrs, axis=0)
    thr = jnp.maximum(thr, INT_MIN + 1)
    thr = jnp.tile(thr, (1, reps))

    def bias_body(c, _):
        bias_ref[c] = jnp.where(key_ref[c] >= thr, 0.0, -jnp.inf)
        return 0

    lax.fori_loop(0, nchunks, bias_body, 0)

    @pl.when(nchunks < key_ref.shape[0])
    def _():
        bias_ref[nchunks] = jnp.full((tq, tk), -jnp.inf, F32)

    hpl = KV_PER_LOOP * group
    for g0 in range(0, DSA_KV_HEADS, KV_PER_LOOP):
        for hh in range(hpl):
            h = g0 * group + hh
            qs_ref[hh * tq:(hh + 1) * tq, :] = q_ref[:, h * d:(h + 1) * d]
        m_ref[...] = jnp.full_like(m_ref, -jnp.inf)
        l_ref[...] = jnp.zeros_like(l_ref)
        acc_ref[...] = jnp.zeros_like(acc_ref)

        def att_body(c2, _, g0=g0):
            r0 = pl.multiple_of(c2 * 2 * tk, 2 * tk)
            bias = jnp.concatenate([bias_ref[2 * c2], bias_ref[2 * c2 + 1]], axis=1)
            for gg in range(KV_PER_LOOP):
                g = g0 + gg
                kc = k_ref[pl.ds(r0, 2 * tk), g * d:(g + 1) * d]
                vc = v_ref[pl.ds(r0, 2 * tk), g * d:(g + 1) * d]
                s_all = lax.dot_general(qs_ref[gg * group * tq:(gg + 1) * group * tq, :], kc,
                                        (((1,), (1,)), ((), ())), preferred_element_type=F32)
                for hh in range(group):
                    _flash_update(s_all[hh * tq:(hh + 1) * tq] + bias, vc, m_ref, l_ref, acc_ref,
                                  gg * group + hh, guard=True)
            return 0

        lax.fori_loop(0, (nchunks + 1) // 2, att_body, 0)
        for hh in range(hpl):
            h = g0 * group + hh
            o_ref[:, h * d:(h + 1) * d] = (acc_ref[hh] / l_ref[hh]).astype(o_ref.dtype)


def dsa_attention(proj, tail, batch, seq, *, tq=256):
    nq = seq // tq
    assert nq % 2 == 0
    topk = min(DSA_TOPK, seq // 4)
    hpl = KV_PER_LOOP * DSA_HEADS // DSA_KV_HEADS
    kvw = DSA_KV_HEADS * DSA_HEAD_DIM
    iqw = IDX_HEADS * IDX_HEAD_DIM
    nqi = iqw // QI_BLOCK
    col = lambda name, width: _NAT[name][0] // width
    qi_specs = [pl.BlockSpec((tq, QI_BLOCK), lambda b, i, r=r: (b * nq + i, col("iq", QI_BLOCK) + r))
                for r in range(nqi)]
    return pl.pallas_call(
        functools.partial(_dsa_kernel, tq=tq, topk=topk, nqi=nqi),
        grid=(batch, nq),
        in_specs=qi_specs + [
            pl.BlockSpec((tq, LANE), lambda b, i: (b * nq + i, _TAIL["iw"][0] // LANE)),
            pl.BlockSpec((seq, IDX_HEAD_DIM), lambda b, i: (b, col("ik", IDX_HEAD_DIM))),
            pl.BlockSpec((tq, DSA_OUT), lambda b, i: (b * nq + i, col("dq", DSA_OUT))),
            pl.BlockSpec((seq, kvw), lambda b, i: (b, col("dk", kvw))),
            pl.BlockSpec((seq, kvw), lambda b, i: (b, col("dv", kvw)))],
        out_specs=pl.BlockSpec((tq, DSA_OUT), lambda b, i: (b * nq + i, 0)),
        out_shape=jax.ShapeDtypeStruct((batch * seq, DSA_OUT), BF16),
        scratch_shapes=[pltpu.VMEM((IDX_HEADS * tq, IDX_HEAD_DIM), BF16),
                        pltpu.VMEM((IDX_HEADS, tq, LANE), F32),
                        pltpu.VMEM((IDX_HEADS // IDX_GROUP, IDX_GROUP * tq, tq), F32),
                        pltpu.VMEM((nq, tq, tq), jnp.int32),
                        pltpu.VMEM((nq, tq, tq), F32),
                        pltpu.VMEM((hpl * tq, DSA_HEAD_DIM), BF16),
                        pltpu.VMEM((hpl, tq, LANE), F32),
                        pltpu.VMEM((hpl, tq, LANE), F32),
                        pltpu.VMEM((hpl, tq, DSA_HEAD_DIM), F32)],
        compiler_params=_cparams(("parallel", "arbitrary")),
        name="dsa_attention",
    )(*([proj] * nqi), tail, proj, proj, proj, proj)


def _mla_kernel(q_ref, kn_ref, kpe_ref, v_ref, o_ref, m_ref, l_ref, acc_ref, *, tq, hb):
    tk = tq
    i = pl.program_id(2)
    m_ref[...] = jnp.full_like(m_ref, -jnp.inf)
    l_ref[...] = jnp.zeros_like(l_ref)
    acc_ref[...] = jnp.zeros_like(acc_ref)
    causal = (lax.broadcasted_iota(jnp.int32, (tq, tk), 1)
              <= lax.broadcasted_iota(jnp.int32, (tq, tk), 0))

    def step(c, masked):
        r0 = pl.multiple_of(c * tk, tk)
        kpe = kpe_ref[pl.ds(r0, tk), :]
        for h in range(hb):
            cols = slice(h * MLA_NOPE_DIM, (h + 1) * MLA_NOPE_DIM)
            kf = jnp.concatenate([kn_ref[pl.ds(r0, tk), cols], kpe], axis=1)
            s = lax.dot_general(q_ref[:, h * MLA_QK_PAD:(h + 1) * MLA_QK_PAD], kf,
                                (((1,), (1,)), ((), ())), preferred_element_type=F32)
            if masked:
                s = jnp.where(causal, s, -jnp.inf)
            _flash_update(s, v_ref[pl.ds(r0, tk), cols], m_ref, l_ref, acc_ref, h, guard=False)

    def body(c, _):
        step(c, False)
        return 0

    lax.fori_loop(0, i, body, 0)
    step(i, True)
    for h in range(hb):
        o_ref[:, h * MLA_V_DIM:(h + 1) * MLA_V_DIM] = (acc_ref[h] / l_ref[h]).astype(o_ref.dtype)


def mla_attention(q, kv, tail, batch, seq, *, tq=512, hb=2):
    nq = seq // tq
    ng = MLA_HEADS // hb
    return pl.pallas_call(
        functools.partial(_mla_kernel, tq=tq, hb=hb),
        grid=(batch, ng, nq),
        in_specs=[pl.BlockSpec((tq, hb * MLA_QK_PAD), lambda b, g, i: (b * nq + i, g)),
                  pl.BlockSpec((seq, hb * MLA_NOPE_DIM), lambda b, g, i: (b, g)),
                  pl.BlockSpec((seq, LANE), lambda b, g, i: (b, _TAIL["kr"][0] // LANE)),
                  pl.BlockSpec((seq, hb * MLA_V_DIM), lambda b, g, i: (b, ng + g))],
        out_specs=pl.BlockSpec((tq, hb * MLA_V_DIM), lambda b, g, i: (b * nq + i, g)),
        out_shape=jax.ShapeDtypeStruct((batch * seq, MLA_OUT), BF16),
        scratch_shapes=[pltpu.VMEM((hb, tq, LANE), F32), pltpu.VMEM((hb, tq, LANE), F32),
                        pltpu.VMEM((hb, tq, MLA_V_DIM), F32)],
        compiler_params=_cparams(("parallel", "parallel", "arbitrary")),
        name="mla_attention",
    )(q, kv, tail, kv)


def _tail_w_in(w):
    d = w.shape[0]
    parts, pos = [], 0
    for name in sorted(_TAIL, key=lambda s: _TAIL[s][0]):
        off, width = _TAIL[name]
        assert off == pos
        a = w[:, _NAT[name][0]:_NAT[name][0] + _NAT[name][1]]
        parts.append(a)
        if a.shape[1] < width:
            parts.append(jnp.zeros((d, width - a.shape[1]), w.dtype))
        pos += width
    assert pos == TAIL_WIDTH
    return jnp.concatenate(parts, axis=1).astype(BF16)


def _relayout_w_uq(w):
    r = w.shape[0]
    a = w.reshape(r, MLA_HEADS, MLA_NOPE_DIM + MLA_ROPE_DIM)
    a = jnp.pad(a, ((0, 0), (0, 0), (0, MLA_QK_PAD - a.shape[2])))
    return a.reshape(r, MLA_HEADS * MLA_QK_PAD).astype(BF16)


def _relayout_w_ukv(w):
    r = w.shape[0]
    a = w.reshape(r, MLA_HEADS, 2, MLA_NOPE_DIM).transpose(0, 2, 1, 3)
    return a.reshape(r, 2 * MLA_HEADS * MLA_NOPE_DIM).astype(BF16)


def _main_tile_modes():
    modes = [(1.0, None)] * (MAIN_WIDTH // LANE)

    def fill(name, mode):
        off, width = _NAT[name]
        for t in range(off // LANE, (off + width) // LANE):
            modes[t] = mode

    fill("dq", (DSA_HEAD_DIM ** -0.5, 0))
    fill("dk", (1.0, 0))
    fill("iq", (IDX_HEAD_DIM ** -0.5, 0))
    fill("ik", (1.0, 0))
    return modes


def _tail_tile_modes():
    modes = [(1.0, None)] * (TAIL_WIDTH // LANE)
    modes[_TAIL["kr"][0] // LANE] = (1.0, 0)
    return modes


def kernel(x, attn_norm, w_in, mla_q_norm, mla_kv_norm, w_uq, w_ukv, w_o, mlp_norm, w_up,
           w_down, final_norm):
    batch, seq, d_model = x.shape
    depth = w_in.shape[0]
    m = batch * seq

    ret_tables = _retention_tables(seq)
    part_tab = _rope_tables(seq, PARTIAL_ROPE_THETA, PARTIAL_ROT_DIM)
    mla_tab = _rope_tables(seq, MLA_ROPE_THETA, MLA_ROPE_DIM)
    mla_scale = (MLA_NOPE_DIM + MLA_ROPE_DIM) ** -0.5
    uq_modes = [(mla_scale, None), (mla_scale, 0)] * MLA_HEADS

    w_in_b, w_o_b, w_up_b, w_down_b = (a.astype(BF16) for a in (w_in, w_o, w_up, w_down))

    h = x.reshape(m, d_model)
    for l in range(depth):
        u = rmsnorm(h, attn_norm[l], BF16)
        proj = matmul(u, w_in_b, BF16, layer=l, name="mm_in", n_out=MAIN_WIDTH, bn=MAIN_BN,
                      tile_modes=_main_tile_modes(), rope_tabs=(part_tab,), seq=seq)
        tail = matmul(u, _tail_w_in(w_in[l]), BF16, name="mm_in_tail", bn=TAIL_WIDTH,
                      tile_modes=_tail_tile_modes(), rope_tabs=(mla_tab,), seq=seq)

        ret = retention(proj, batch, seq, ret_tables)
        dsa = dsa_attention(proj, tail, batch, seq)

        cq = rmsnorm(tail, mla_q_norm[l], BF16, col_off=_TAIL["cq"][0], width=MLA_Q_LORA)
        ckv = rmsnorm(tail, mla_kv_norm[l], BF16, col_off=_TAIL["ckv"][0], width=MLA_KV_LORA)
        q_full = matmul(cq, _relayout_w_uq(w_uq[l]), BF16, name="mm_uq", tile_modes=uq_modes,
                        rope_tabs=(mla_tab,), seq=seq)
        kv_up = matmul(ckv, _relayout_w_ukv(w_ukv[l]), BF16, name="mm_ukv")
        mla = mla_attention(q_full, kv_up, tail, batch, seq)

        h = matmul((ret, dsa, mla), w_o_b, F32, layer=l, res=h, bn=512, name="mm_o")
        u = rmsnorm(h, mlp_norm[l], BF16)
        up = matmul(u, w_up_b, BF16, layer=l, act="relu2", name="mm_up")
        h = matmul(up, w_down_b, F32, layer=l, res=h, bn=512, name="mm_down")
    out = rmsnorm(h, final_norm, x.dtype)
    return out.reshape(batch, seq, d_model)
```

```python
import functools
import math

import numpy as np
import jax
import jax.numpy as jnp
from jax import lax
from jax.experimental import pallas as pl
from jax.experimental.pallas import tpu as pltpu

F32 = jnp.float32
BF16 = jnp.bfloat16

RET_HEADS = 8
RET_QK_DIM = 64
RET_V_DIM = 128
RET_CHUNK = 128
RET_ROPE_THETA = 10000.0
DSA_HEADS = 12
DSA_KV_HEADS = 4
DSA_HEAD_DIM = 128
IDX_HEADS = 32
IDX_HEAD_DIM = 128
DSA_TOPK = 256
PARTIAL_ROT_DIM = 32
PARTIAL_ROPE_THETA = 500000.0
MLA_HEADS = 12
MLA_Q_LORA = 768
MLA_KV_LORA = 256
MLA_NOPE_DIM = 128
MLA_ROPE_DIM = 64
MLA_V_DIM = 128
MLA_ROPE_THETA = 10000.0
NORM_EPS = 1e-6

LANE = 128
VMEM_LIMIT = 60 * 1024 * 1024

RET_OUT = RET_HEADS * RET_V_DIM
DSA_OUT = DSA_HEADS * DSA_HEAD_DIM
MLA_OUT = MLA_HEADS * MLA_V_DIM
MLA_QK_PAD = 256

_NAT = {}
_off = 0
for _name, _w in (("rq", 512), ("rk", 512), ("rv", 1024), ("rg", 1024), ("dq", 1536),
                  ("dk", 512), ("dv", 512), ("iq", 4096), ("ik", 128), ("iw", 32),
                  ("cq", 768), ("ckv", 256), ("kr", 64)):
    _NAT[_name] = (_off, _w)
    _off += _w
IN_WIDTH = _off
MAIN_WIDTH = 10240
MAIN_BN = 1024
_TAIL = {"cq": (0, 768), "ckv": (768, 256), "iw": (1024, 128), "kr": (1152, 128)}
TAIL_WIDTH = 1280
QI_BLOCK = 512


def _cparams(sem):
    return pltpu.CompilerParams(dimension_semantics=sem, vmem_limit_bytes=VMEM_LIMIT)


def _rmsnorm_kernel(x_ref, g_ref, o_ref):
    x = x_ref[...].astype(F32)
    ms = jnp.mean(x * x, axis=-1, keepdims=True)
    o_ref[...] = (x * lax.rsqrt(ms + NORM_EPS) * g_ref[...]).astype(o_ref.dtype)


def rmsnorm(x, g, out_dtype, *, col_off=0, width=None, bm=256):
    m = x.shape[0]
    width = x.shape[1] if width is None else width
    assert col_off % width == 0 and m % bm == 0
    cb = col_off // width
    return pl.pallas_call(
        _rmsnorm_kernel,
        grid=(m // bm,),
        in_specs=[pl.BlockSpec((bm, width), lambda i: (i, cb)),
                  pl.BlockSpec((1, width), lambda i: (0, 0))],
        out_specs=pl.BlockSpec((bm, width), lambda i: (i, 0)),
        out_shape=jax.ShapeDtypeStruct((m, width), out_dtype),
        compiler_params=_cparams(("parallel",)),
        name="rmsnorm",
    )(x, g.reshape(1, width).astype(F32))


def _mm_kernel(*refs, nx, act, has_res, nk, groups, halves, ntab):
    x_refs, w_ref = refs[:nx], refs[nx]
    pos = nx + 1
    r_ref = refs[pos] if has_res else None
    pos += int(has_res)
    tab_refs = refs[pos:pos + 3 * ntab]
    pos += 3 * ntab
    o_ref = refs[pos]
    scratch = refs[pos + 1:]

    def store_tiles(acc, pattern):
        for t, (scale, ridx) in enumerate(pattern):
            x = acc[:, t * LANE:(t + 1) * LANE]
            if ridx is not None:
                a, bp, bm = (tab_refs[3 * ridx + k][...] for k in range(3))
                half = halves[ridx]
                x = x * a + pltpu.roll(x, half, 1) * bp + pltpu.roll(x, LANE - half, 1) * bm
            if scale != 1.0:
                x = x * scale
            o_ref[:, t * LANE:(t + 1) * LANE] = x.astype(o_ref.dtype)

    def epilogue(acc):
        if act == "relu2":
            acc = jnp.square(jnp.maximum(acc, 0.0))
        if has_res:
            acc = acc + r_ref[...]
        if groups is None:
            o_ref[...] = acc.astype(o_ref.dtype)
        elif len(groups) == 1:
            store_tiles(acc, groups[0][0])
        else:
            j = pl.program_id(1)
            for pattern, js in groups:
                cond = functools.reduce(jnp.logical_or, [j == jj for jj in js])
                pl.when(cond)(functools.partial(store_tiles, acc, pattern))

    part, k0 = None, 0
    for x_ref in x_refs:
        kw = x_ref.shape[1]
        d = jnp.dot(x_ref[...], w_ref[k0:k0 + kw, :], preferred_element_type=F32)
        part = d if part is None else part + d
        k0 += kw
    if nk == 1:
        epilogue(part)
    else:
        acc_ref = scratch[0]
        k = pl.program_id(2)

        @pl.when(k == 0)
        def _():
            acc_ref[...] = part

        @pl.when(jnp.logical_and(k > 0, k < nk - 1))
        def _():
            acc_ref[...] += part

        @pl.when(k == nk - 1)
        def _():
            epilogue(acc_ref[...] + part)


def matmul(x, w, out_dtype, *, name, layer=None, act=None, res=None, bm=1024, bn=1024, bk=4096,
           n_out=None, tile_modes=None, rope_tabs=(), seq=None):
    xs = x if isinstance(x, (tuple, list)) else (x,)
    m = xs[0].shape[0]
    kdim = sum(a.shape[1] for a in xs)
    assert kdim == w.shape[-2]
    n = w.shape[-1] if n_out is None else n_out
    bm, bn, bk = min(bm, m), min(bn, n), min(bk, kdim)
    assert m % bm == 0 and n % bn == 0 and kdim % bk == 0
    nk = kdim // bk
    assert len(xs) == 1 or nk == 1
    if len(xs) == 1:
        in_specs = [pl.BlockSpec((bm, bk), lambda i, j, k: (i, k))]
    else:
        in_specs = [pl.BlockSpec((bm, a.shape[1]), lambda i, j, k: (i, 0)) for a in xs]
    if layer is None:
        in_specs.append(pl.BlockSpec((bk, bn), lambda i, j, k: (k, j)))
    else:
        in_specs.append(pl.BlockSpec((None, bk, bn), lambda i, j, k: (layer, k, j)))
    args = [*xs, w]
    if res is not None:
        in_specs.append(pl.BlockSpec((bm, bn), lambda i, j, k: (i, j),
                                     pipeline_mode=pl.Buffered(1)))
        args.append(res)
    groups = None
    if tile_modes is not None:
        assert len(tile_modes) == n // LANE and seq % bm == 0
        per_block = bn // LANE
        by_pattern = {}
        for j in range(n // bn):
            by_pattern.setdefault(tuple(tile_modes[j * per_block:(j + 1) * per_block]), []).append(j)
        groups = tuple((p, tuple(js)) for p, js in by_pattern.items())
        nseq = seq // bm
        for a, bp, bmt, _ in rope_tabs:
            in_specs += [pl.BlockSpec((bm, LANE), lambda i, j, k: (i % nseq, 0))] * 3
            args += [a, bp, bmt]
    scratch = [pltpu.VMEM((bm, bn), F32)] if nk > 1 else []
    return pl.pallas_call(
        functools.partial(_mm_kernel, nx=len(xs), act=act, has_res=res is not None, nk=nk,
                          groups=groups,
                          halves=tuple(t[3] for t in rope_tabs), ntab=len(rope_tabs)),
        grid=(m // bm, n // bn, nk),
        in_specs=in_specs,
        out_specs=pl.BlockSpec((bm, bn), lambda i, j, k: (i, j)),
        out_shape=jax.ShapeDtypeStruct((m, n), out_dtype),
        scratch_shapes=scratch,
        compiler_params=_cparams(("parallel", "parallel", "arbitrary")),
        name=name,
    )(*args)


def _rope_tables(seq, theta, rot_dim):
    half = rot_dim // 2
    inv = jnp.exp(-math.log(theta) * jnp.arange(half, dtype=F32) * (2.0 / rot_dim))
    ang = jnp.arange(seq, dtype=F32)[:, None] * inv[None, :]
    cos, sin = jnp.cos(ang), jnp.sin(ang)
    pad = LANE - rot_dim
    a = jnp.concatenate([cos, cos, jnp.ones((seq, pad), F32)], axis=1)
    bp = jnp.concatenate([jnp.zeros((seq, half), F32), sin, jnp.zeros((seq, pad), F32)], axis=1)
    bm = jnp.concatenate([-sin, jnp.zeros((seq, half + pad), F32)], axis=1)
    return a, bp, bm, half


def _retention_kernel(q_ref, k_ref, v_ref, g_ref, cos_ref, sp_ref, sm_ref, hmask_ref, idec_ref,
                      qdec_ref, kdec_ref, cdec_ref, bmask_ref, o_ref, state_ref):
    half = RET_QK_DIM // 2

    @pl.when(pl.program_id(1) == 0)
    def _():
        state_ref[...] = jnp.zeros_like(state_ref)

    cos, sp, sm = cos_ref[...], sp_ref[...], sm_ref[...]

    def rot(x):
        tiles = []
        for t in range(x.shape[1] // LANE):
            xt = x[:, t * LANE:(t + 1) * LANE]
            tiles.append(xt * cos + pltpu.roll(xt, half, 1) * sp
                         + pltpu.roll(xt, LANE - half, 1) * sm)
        return jnp.concatenate(tiles, axis=1)

    q = rot(q_ref[...].astype(F32))
    k = rot(k_ref[...].astype(F32)) * (RET_QK_DIM ** -0.5)
    v = v_ref[...].astype(F32)
    kt = k.T.astype(BF16)
    state = state_ref[...]
    cross = jnp.dot(q.astype(BF16), state.astype(BF16), preferred_element_type=F32) * qdec_ref[...]
    for h in range(RET_HEADS):
        cols = slice(h * RET_V_DIM, (h + 1) * RET_V_DIM)
        qm = (q * hmask_ref[h:h + 1, :]).astype(BF16)
        scores = jnp.dot(qm, kt, preferred_element_type=F32) * idec_ref[h]
        inner = jnp.dot(scores.astype(BF16), v[:, cols].astype(BF16), preferred_element_type=F32)
        o = inner + cross[:, cols]
        o = o * lax.rsqrt(jnp.mean(o * o, axis=-1, keepdims=True) + NORM_EPS)
        gate = g_ref[:, cols].astype(F32)
        o_ref[:, cols] = (gate * jax.nn.sigmoid(gate) * o).astype(o_ref.dtype)
    vd = (v * kdec_ref[...]).astype(BF16)
    upd = jnp.dot(kt, vd, preferred_element_type=F32)
    state_ref[...] = state * cdec_ref[...] + upd * bmask_ref[...]


def _retention_tables(seq):
    h, c, dv, dk = RET_HEADS, RET_CHUNK, RET_V_DIM, RET_QK_DIM
    half = dk // 2
    inv = jnp.exp(-math.log(RET_ROPE_THETA) * jnp.arange(half, dtype=F32) * (2.0 / dk))
    ang = jnp.arange(seq, dtype=F32)[:, None] * inv[None, :]
    cos1, sin1, zero = jnp.cos(ang), jnp.sin(ang), jnp.zeros((seq, half), F32)
    reps = LANE // dk
    cos = jnp.tile(jnp.concatenate([cos1, cos1], axis=1), (1, reps))
    sp = jnp.tile(jnp.concatenate([zero, sin1], axis=1), (1, reps))
    sm = jnp.tile(jnp.concatenate([-sin1, zero], axis=1), (1, reps))
    log_gamma = jnp.log1p(-jnp.exp2(-5.0 - jnp.arange(h, dtype=F32)))
    idx = jnp.arange(c, dtype=F32)
    rel = idx[:, None] - idx[None, :]
    idec = jnp.where(rel[None] >= 0,
                     jnp.exp(jnp.maximum(rel, 0.0)[None] * log_gamma[:, None, None]), 0.0)
    qdec = jnp.exp((idx + 1.0)[:, None] * log_gamma[None, :])
    kdec = jnp.exp((c - 1.0 - idx)[:, None] * log_gamma[None, :])
    cdec = jnp.exp(c * log_gamma)[None, :]
    rep = lambda a: jnp.repeat(a, dv, axis=1)
    lane_head = np.arange(h * dk) // dk
    hmask = jnp.asarray(lane_head[None, :] == np.arange(h)[:, None], F32)
    bmask = jnp.asarray(lane_head[:, None] == (np.arange(h * dv) // dv)[None, :], F32)
    return cos, sp, sm, hmask, idec, rep(qdec), rep(kdec), rep(cdec), bmask


def retention(proj, batch, seq, tables):
    c = RET_CHUNK
    nch = seq // c
    qw = RET_HEADS * RET_QK_DIM
    col = lambda name: _NAT[name][0] // _NAT[name][1]
    row = lambda b, n: b * nch + n
    const2 = lambda b, n: (0, 0)
    cos, sp, sm, hmask, idec, qdec, kdec, cdec, bmask = tables
    tab = pl.BlockSpec((c, LANE), lambda b, n: (n, 0))
    return pl.pallas_call(
        _retention_kernel,
        grid=(batch, nch),
        in_specs=[pl.BlockSpec((c, qw), lambda b, n: (row(b, n), col("rq"))),
                  pl.BlockSpec((c, qw), lambda b, n: (row(b, n), col("rk"))),
                  pl.BlockSpec((c, RET_OUT), lambda b, n: (row(b, n), col("rv"))),
                  pl.BlockSpec((c, RET_OUT), lambda b, n: (row(b, n), col("rg"))),
                  tab, tab, tab,
                  pl.BlockSpec(hmask.shape, const2),
                  pl.BlockSpec(idec.shape, lambda b, n: (0, 0, 0)),
                  pl.BlockSpec(qdec.shape, const2),
                  pl.BlockSpec(kdec.shape, const2),
                  pl.BlockSpec(cdec.shape, const2),
                  pl.BlockSpec(bmask.shape, const2)],
        out_specs=pl.BlockSpec((c, RET_OUT), lambda b, n: (row(b, n), 0)),
        out_shape=jax.ShapeDtypeStruct((batch * seq, RET_OUT), BF16),
        scratch_shapes=[pltpu.VMEM((qw, RET_OUT), F32)],
        compiler_params=_cparams(("parallel", "arbitrary")),
        name="retention",
    )(proj, proj, proj, proj, cos, sp, sm, hmask, idec, qdec, kdec, cdec, bmask)


LOG2E = math.log2(math.e)
M_INIT = float(np.finfo(np.float32).min)


def _flash_update(s, v_c, m_ref, l_ref, acc_ref, h):
    reps = s.shape[1] // LANE

    def fold(x, op):
        tiles = [x[:, r * LANE:(r + 1) * LANE] for r in range(reps)]
        while len(tiles) > 1:
            tiles = [op(a, b) for a, b in zip(tiles[::2], tiles[1::2])]
        return tiles[0]

    m_prev = m_ref[h]
    m_new = jnp.maximum(m_prev, jnp.max(fold(s, jnp.maximum), axis=1, keepdims=True))
    p = jnp.exp2(s - jnp.tile(m_new, (1, reps)))
    alpha = jnp.exp2(m_prev - m_new)
    l_ref[h] = alpha * l_ref[h] + jnp.sum(fold(p, jnp.add), axis=1, keepdims=True)
    m_ref[h] = m_new
    acc_ref[h] = acc_ref[h] * alpha + jnp.dot(p.astype(BF16), v_c, preferred_element_type=F32)


INT_MIN = -2 ** 31
IDX_GROUP = 8
ROW_BLOCK = 64
SEARCH_ROWS = 128
PASSES_PER_CHECK = 4
KV_PER_LOOP = 4


def _dsa_kernel(*refs, tq, topk, nqi):
    qi_refs = refs[:nqi]
    (w_ref, kidx_ref, q_ref, k_ref, v_ref, o_ref,
     qis_ref, wb_ref, lg_ref, key_ref, bias_ref, qs_ref, m_ref, l_ref, acc_ref) = refs[nqi:]
    tk = tq
    i = pl.program_id(1)
    nchunks = i + 1
    reps = tk // LANE
    ngroups = IDX_HEADS // IDX_GROUP
    group = DSA_HEADS // DSA_KV_HEADS
    d = DSA_HEAD_DIM
    heads_per_ref = IDX_HEADS // nqi

    w = w_ref[...].astype(F32) * (IDX_HEADS ** -0.5)
    for h in range(IDX_HEADS):
        r, t = divmod(h, heads_per_ref)
        qis_ref[h * tq:(h + 1) * tq, :] = qi_refs[r][:, t * LANE:(t + 1) * LANE]
        wb_ref[h] = jnp.broadcast_to(w[:, h:h + 1], (tq, LANE))

    def score_chunk(c, diagonal):
        r0 = pl.multiple_of(c * tk, tk)
        kc = kidx_ref[pl.ds(r0, tk), :]
        for g in range(ngroups):
            lg_ref[g] = lax.dot_general(
                qis_ref[g * IDX_GROUP * tq:(g + 1) * IDX_GROUP * tq, :], kc,
                (((1,), (1,)), ((), ())), preferred_element_type=F32)
        for rb in range(tq // ROW_BLOCK):
            q0 = rb * ROW_BLOCK
            rows = slice(q0, q0 + ROW_BLOCK)
            acc = jnp.zeros((ROW_BLOCK, tk), F32)
            for h in range(IDX_HEADS):
                g, hg = divmod(h, IDX_GROUP)
                lg = lg_ref[g, hg * tq + q0:hg * tq + q0 + ROW_BLOCK, :]
                acc = acc + jnp.maximum(lg, 0.0) * jnp.tile(wb_ref[h, rows, :], (1, reps))
            bits = pltpu.bitcast(acc, jnp.int32)
            key = bits ^ (lax.shift_right_arithmetic(bits, 31) & 0x7FFFFFFF)
            if diagonal:
                kpos = lax.broadcasted_iota(jnp.int32, (ROW_BLOCK, tk), 1)
                qpos = q0 + lax.broadcasted_iota(jnp.int32, (ROW_BLOCK, tk), 0)
                key = jnp.where(kpos <= qpos, key, INT_MIN)
            key_ref[c, rows, :] = key

    def score_body(c, _):
        score_chunk(c, False)
        return 0

    lax.fori_loop(0, i, score_body, 0)
    score_chunk(i, True)

    nslab = tq // SEARCH_ROWS

    def count_ge(cand, s):
        rows = slice(s * SEARCH_ROWS, (s + 1) * SEARCH_ROWS)

        def body(c, cnt):
            for r in range(reps):
                blk = key_ref[c, rows, r * LANE:(r + 1) * LANE]
                cnt = cnt + (blk >= cand).astype(jnp.int32)
            return cnt
        return lax.fori_loop(0, nchunks, body, jnp.zeros((SEARCH_ROWS, LANE), jnp.int32))

    def bit_cond(state):
        b, _, cnts = state
        off = functools.reduce(jnp.maximum, [jnp.abs(c - topk) for c in cnts])
        return jnp.logical_and(b < 32, jnp.max(off) > 0)

    def bit_body(state):
        b, ts, cnts = state
        ts, cnts = list(ts), list(cnts)
        for u in range(PASSES_PER_CHECK):
            bit = lax.shift_left(jnp.int32(1), 31 - (b + u))
            cands = [t ^ bit for t in ts]
            partial = [count_ge(cands[s], s) for s in range(nslab)]
            for s in range(nslab):
                n = jnp.broadcast_to(jnp.sum(partial[s], axis=1, keepdims=True),
                                     (SEARCH_ROWS, LANE))
                take = n >= topk
                ts[s], cnts[s] = jnp.where(take, cands[s], ts[s]), jnp.where(take, n, cnts[s])
        return b + PASSES_PER_CHECK, tuple(ts), tuple(cnts)

    t0 = (jnp.full((SEARCH_ROWS, LANE), INT_MIN, jnp.int32),) * nslab
    c0 = (jnp.full((SEARCH_ROWS, LANE), nchunks * tk, jnp.int32),) * nslab
    _, thrs, _ = lax.while_loop(bit_cond, bit_body, (jnp.int32(0), t0, c0))
    thr = jnp.concatenate(thrs, axis=0)
    thr = jnp.maximum(thr, INT_MIN + 1)
    thr = jnp.tile(thr, (1, reps))

    def bias_body(c, _):
        bias_ref[c] = jnp.where(key_ref[c] >= thr, 0.0, -jnp.inf)
        return 0

    lax.fori_loop(0, nchunks, bias_body, 0)

    @pl.when(nchunks < key_ref.shape[0])
    def _():
        bias_ref[nchunks] = jnp.full((tq, tk), -jnp.inf, F32)

    hpl = KV_PER_LOOP * group
    for g0 in range(0, DSA_KV_HEADS, KV_PER_LOOP):
        for hh in range(hpl):
            h = g0 * group + hh
            qs_ref[hh * tq:(hh + 1) * tq, :] = q_ref[:, h * d:(h + 1) * d]
        m_ref[...] = jnp.full_like(m_ref, M_INIT)
        l_ref[...] = jnp.zeros_like(l_ref)
        acc_ref[...] = jnp.zeros_like(acc_ref)

        def att_body(c2, _, g0=g0):
            r0 = pl.multiple_of(c2 * 2 * tk, 2 * tk)
            bias = jnp.concatenate([bias_ref[2 * c2], bias_ref[2 * c2 + 1]], axis=1)
            for gg in range(KV_PER_LOOP):
                g = g0 + gg
                kc = k_ref[pl.ds(r0, 2 * tk), g * d:(g + 1) * d]
                vc = v_ref[pl.ds(r0, 2 * tk), g * d:(g + 1) * d]
                s_all = lax.dot_general(qs_ref[gg * group * tq:(gg + 1) * group * tq, :], kc,
                                        (((1,), (1,)), ((), ())), preferred_element_type=F32)
                for hh in range(group):
                    _flash_update(s_all[hh * tq:(hh + 1) * tq] + bias, vc, m_ref, l_ref, acc_ref,
                                  gg * group + hh)
            return 0

        lax.fori_loop(0, (nchunks + 1) // 2, att_body, 0)
        for hh in range(hpl):
            h = g0 * group + hh
            o_ref[:, h * d:(h + 1) * d] = (acc_ref[hh] / l_ref[hh]).astype(o_ref.dtype)


def dsa_attention(proj, tail, batch, seq, *, tq=256):
    nq = seq // tq
    assert nq % 2 == 0
    topk = min(DSA_TOPK, seq // 4)
    hpl = KV_PER_LOOP * DSA_HEADS // DSA_KV_HEADS
    kvw = DSA_KV_HEADS * DSA_HEAD_DIM
    iqw = IDX_HEADS * IDX_HEAD_DIM
    nqi = iqw // QI_BLOCK
    col = lambda name, width: _NAT[name][0] // width
    qi_specs = [pl.BlockSpec((tq, QI_BLOCK), lambda b, i, r=r: (b * nq + i, col("iq", QI_BLOCK) + r))
                for r in range(nqi)]
    return pl.pallas_call(
        functools.partial(_dsa_kernel, tq=tq, topk=topk, nqi=nqi),
        grid=(batch, nq),
        in_specs=qi_specs + [
            pl.BlockSpec((tq, LANE), lambda b, i: (b * nq + i, _TAIL["iw"][0] // LANE)),
            pl.BlockSpec((seq, IDX_HEAD_DIM), lambda b, i: (b, col("ik", IDX_HEAD_DIM))),
            pl.BlockSpec((tq, DSA_OUT), lambda b, i: (b * nq + i, col("dq", DSA_OUT))),
            pl.BlockSpec((seq, kvw), lambda b, i: (b, col("dk", kvw))),
            pl.BlockSpec((seq, kvw), lambda b, i: (b, col("dv", kvw)))],
        out_specs=pl.BlockSpec((tq, DSA_OUT), lambda b, i: (b * nq + i, 0)),
        out_shape=jax.ShapeDtypeStruct((batch * seq, DSA_OUT), BF16),
        scratch_shapes=[pltpu.VMEM((IDX_HEADS * tq, IDX_HEAD_DIM), BF16),
                        pltpu.VMEM((IDX_HEADS, tq, LANE), F32),
                        pltpu.VMEM((IDX_HEADS // IDX_GROUP, IDX_GROUP * tq, tq), F32),
                        pltpu.VMEM((nq, tq, tq), jnp.int32),
                        pltpu.VMEM((nq, tq, tq), F32),
                        pltpu.VMEM((hpl * tq, DSA_HEAD_DIM), BF16),
                        pltpu.VMEM((hpl, tq, LANE), F32),
                        pltpu.VMEM((hpl, tq, LANE), F32),
                        pltpu.VMEM((hpl, tq, DSA_HEAD_DIM), F32)],
        compiler_params=_cparams(("parallel", "arbitrary")),
        name="dsa_attention",
    )(*([proj] * nqi), tail, proj, proj, proj, proj)


def _mla_kernel(q_ref, kn_ref, kpe_ref, v_ref, o_ref, m_ref, l_ref, acc_ref, *, tq, hb):
    tk = tq
    i = pl.program_id(2)
    m_ref[...] = jnp.full_like(m_ref, M_INIT)
    l_ref[...] = jnp.zeros_like(l_ref)
    acc_ref[...] = jnp.zeros_like(acc_ref)
    causal = (lax.broadcasted_iota(jnp.int32, (tq, tk), 1)
              <= lax.broadcasted_iota(jnp.int32, (tq, tk), 0))

    def step(c, masked):
        r0 = pl.multiple_of(c * tk, tk)
        kpe = kpe_ref[pl.ds(r0, tk), :]
        for h in range(hb):
            cols = slice(h * MLA_NOPE_DIM, (h + 1) * MLA_NOPE_DIM)
            kf = jnp.concatenate([kn_ref[pl.ds(r0, tk), cols], kpe], axis=1)
            s = lax.dot_general(q_ref[:, h * MLA_QK_PAD:(h + 1) * MLA_QK_PAD], kf,
                                (((1,), (1,)), ((), ())), preferred_element_type=F32)
            if masked:
                s = jnp.where(causal, s, -jnp.inf)
            _flash_update(s, v_ref[pl.ds(r0, tk), cols], m_ref, l_ref, acc_ref, h)

    def body(c, _):
        step(c, False)
        return 0

    lax.fori_loop(0, i, body, 0)
    step(i, True)
    for h in range(hb):
        o_ref[:, h * MLA_V_DIM:(h + 1) * MLA_V_DIM] = (acc_ref[h] / l_ref[h]).astype(o_ref.dtype)


def mla_attention(q, kv, tail, batch, seq, *, tq=512, hb=4):
    nq = seq // tq
    ng = MLA_HEADS // hb
    return pl.pallas_call(
        functools.partial(_mla_kernel, tq=tq, hb=hb),
        grid=(batch, ng, nq),
        in_specs=[pl.BlockSpec((tq, hb * MLA_QK_PAD), lambda b, g, i: (b * nq + i, g)),
                  pl.BlockSpec((seq, hb * MLA_NOPE_DIM), lambda b, g, i: (b, g)),
                  pl.BlockSpec((seq, LANE), lambda b, g, i: (b, _TAIL["kr"][0] // LANE)),
                  pl.BlockSpec((seq, hb * MLA_V_DIM), lambda b, g, i: (b, ng + g))],
        out_specs=pl.BlockSpec((tq, hb * MLA_V_DIM), lambda b, g, i: (b * nq + i, g)),
        out_shape=jax.ShapeDtypeStruct((batch * seq, MLA_OUT), BF16),
        scratch_shapes=[pltpu.VMEM((hb, tq, LANE), F32), pltpu.VMEM((hb, tq, LANE), F32),
                        pltpu.VMEM((hb, tq, MLA_V_DIM), F32)],
        compiler_params=_cparams(("parallel", "parallel", "arbitrary")),
        name="mla_attention",
    )(q, kv, tail, kv)


def _tail_w_in(w):
    parts, pos = [], 0
    for name in sorted(_TAIL, key=lambda s: _TAIL[s][0]):
        off, width = _TAIL[name]
        assert off == pos
        a = w[..., _NAT[name][0]:_NAT[name][0] + _NAT[name][1]]
        parts.append(a)
        if a.shape[-1] < width:
            parts.append(jnp.zeros((*w.shape[:-1], width - a.shape[-1]), w.dtype))
        pos += width
    assert pos == TAIL_WIDTH
    return jnp.concatenate(parts, axis=-1).astype(BF16)


def _relayout_w_uq(w):
    dep, r = w.shape[:2]
    a = w.reshape(dep, r, MLA_HEADS, MLA_NOPE_DIM + MLA_ROPE_DIM)
    a = jnp.pad(a, ((0, 0), (0, 0), (0, 0), (0, MLA_QK_PAD - a.shape[3])))
    return a.reshape(dep, r, MLA_HEADS * MLA_QK_PAD).astype(BF16)


def _relayout_w_ukv(w):
    dep, r = w.shape[:2]
    a = w.reshape(dep, r, MLA_HEADS, 2, MLA_NOPE_DIM).transpose(0, 1, 3, 2, 4)
    return a.reshape(dep, r, 2 * MLA_HEADS * MLA_NOPE_DIM).astype(BF16)


def _main_tile_modes():
    modes = [(1.0, None)] * (MAIN_WIDTH // LANE)

    def fill(name, mode):
        off, width = _NAT[name]
        for t in range(off // LANE, (off + width) // LANE):
            modes[t] = mode

    fill("dq", (DSA_HEAD_DIM ** -0.5 * LOG2E, 0))
    fill("dk", (1.0, 0))
    fill("iq", (IDX_HEAD_DIM ** -0.5, 0))
    fill("ik", (1.0, 0))
    return modes


def _tail_tile_modes():
    modes = [(1.0, None)] * (TAIL_WIDTH // LANE)
    modes[_TAIL["kr"][0] // LANE] = (1.0, 0)
    return modes


def kernel(x, attn_norm, w_in, mla_q_norm, mla_kv_norm, w_uq, w_ukv, w_o, mlp_norm, w_up,
           w_down, final_norm):
    batch, seq, d_model = x.shape
    depth = w_in.shape[0]
    m = batch * seq

    ret_tables = _retention_tables(seq)
    part_tab = _rope_tables(seq, PARTIAL_ROPE_THETA, PARTIAL_ROT_DIM)
    mla_tab = _rope_tables(seq, MLA_ROPE_THETA, MLA_ROPE_DIM)
    mla_scale = (MLA_NOPE_DIM + MLA_ROPE_DIM) ** -0.5 * LOG2E
    uq_modes = [(mla_scale, None), (mla_scale, 0)] * MLA_HEADS

    w_in_b, w_o_b, w_up_b, w_down_b = (a.astype(BF16) for a in (w_in, w_o, w_up, w_down))
    w_tail_b = _tail_w_in(w_in)
    w_uq_b, w_ukv_b = _relayout_w_uq(w_uq), _relayout_w_ukv(w_ukv)

    h = x.reshape(m, d_model)
    for l in range(depth):
        u = rmsnorm(h, attn_norm[l], BF16)
        proj = matmul(u, w_in_b, BF16, layer=l, name="mm_in", n_out=MAIN_WIDTH, bn=MAIN_BN,
                      tile_modes=_main_tile_modes(), rope_tabs=(part_tab,), seq=seq)
        tail = matmul(u, w_tail_b, BF16, layer=l, name="mm_in_tail", bn=TAIL_WIDTH,
                      tile_modes=_tail_tile_modes(), rope_tabs=(mla_tab,), seq=seq)

        ret = retention(proj, batch, seq, ret_tables)
        dsa = dsa_attention(proj, tail, batch, seq)

        cq = rmsnorm(tail, mla_q_norm[l], BF16, col_off=_TAIL["cq"][0], width=MLA_Q_LORA)
        ckv = rmsnorm(tail, mla_kv_norm[l], BF16, col_off=_TAIL["ckv"][0], width=MLA_KV_LORA)
        q_full = matmul(cq, w_uq_b, BF16, layer=l, name="mm_uq", tile_modes=uq_modes,
                        rope_tabs=(mla_tab,), seq=seq)
        kv_up = matmul(ckv, w_ukv_b, BF16, layer=l, name="mm_ukv")
        mla = mla_attention(q_full, kv_up, tail, batch, seq)

        h = matmul((ret, dsa, mla), w_o_b, F32, layer=l, res=h, bn=512, name="mm_o")
        u = rmsnorm(h, mlp_norm[l], BF16)
        up = matmul(u, w_up_b, BF16, layer=l, act="relu2", name="mm_up")
        h = matmul(up, w_down_b, F32, layer=l, res=h, bn=1024, name="mm_down")
    out = rmsnorm(h, final_norm, x.dtype)
    return out.reshape(batch, seq, d_model)
```

```python
import functools
import math

import numpy as np
import jax
import jax.numpy as jnp
from jax import lax
from jax.experimental import pallas as pl
from jax.experimental.pallas import tpu as pltpu

F32 = jnp.float32
BF16 = jnp.bfloat16

RET_HEADS = 8
RET_QK_DIM = 64
RET_V_DIM = 128
RET_CHUNK = 128
RET_ROPE_THETA = 10000.0
DSA_HEADS = 12
DSA_KV_HEADS = 4
DSA_HEAD_DIM = 128
IDX_HEADS = 32
IDX_HEAD_DIM = 128
DSA_TOPK = 256
PARTIAL_ROT_DIM = 32
PARTIAL_ROPE_THETA = 500000.0
MLA_HEADS = 12
MLA_Q_LORA = 768
MLA_KV_LORA = 256
MLA_NOPE_DIM = 128
MLA_ROPE_DIM = 64
MLA_V_DIM = 128
MLA_ROPE_THETA = 10000.0
NORM_EPS = 1e-6

LANE = 128
VMEM_LIMIT = 60 * 1024 * 1024

RET_OUT = RET_HEADS * RET_V_DIM
DSA_OUT = DSA_HEADS * DSA_HEAD_DIM
MLA_OUT = MLA_HEADS * MLA_V_DIM
MLA_QK_PAD = 256

_NAT = {}
_off = 0
for _name, _w in (("rq", 512), ("rk", 512), ("rv", 1024), ("rg", 1024), ("dq", 1536),
                  ("dk", 512), ("dv", 512), ("iq", 4096), ("ik", 128), ("iw", 32),
                  ("cq", 768), ("ckv", 256), ("kr", 64)):
    _NAT[_name] = (_off, _w)
    _off += _w
IN_WIDTH = _off
MAIN_WIDTH = 10240
MAIN_BN = 1024
_TAIL = {"cq": (0, 768), "ckv": (768, 256), "iw": (1024, 128), "kr": (1152, 128)}
TAIL_WIDTH = 1280
QI_BLOCK = 512


def _cparams(sem):
    return pltpu.CompilerParams(dimension_semantics=sem, vmem_limit_bytes=VMEM_LIMIT)


def _rmsnorm_kernel(x_ref, g_ref, o_ref):
    x = x_ref[...].astype(F32)
    ms = jnp.mean(x * x, axis=-1, keepdims=True)
    o_ref[...] = (x * lax.rsqrt(ms + NORM_EPS) * g_ref[...]).astype(o_ref.dtype)


def rmsnorm(x, g, out_dtype, *, col_off=0, width=None, bm=256):
    m = x.shape[0]
    width = x.shape[1] if width is None else width
    assert col_off % width == 0 and m % bm == 0
    cb = col_off // width
    return pl.pallas_call(
        _rmsnorm_kernel,
        grid=(m // bm,),
        in_specs=[pl.BlockSpec((bm, width), lambda i: (i, cb)),
                  pl.BlockSpec((1, width), lambda i: (0, 0))],
        out_specs=pl.BlockSpec((bm, width), lambda i: (i, 0)),
        out_shape=jax.ShapeDtypeStruct((m, width), out_dtype),
        compiler_params=_cparams(("parallel",)),
        name="rmsnorm",
    )(x, g.reshape(1, width).astype(F32))


def _mm_kernel(*refs, nx, act, has_res, nk, groups, halves, ntab):
    x_refs, w_ref = refs[:nx], refs[nx]
    pos = nx + 1
    r_ref = refs[pos] if has_res else None
    pos += int(has_res)
    tab_refs = refs[pos:pos + 3 * ntab]
    pos += 3 * ntab
    o_ref = refs[pos]
    scratch = refs[pos + 1:]

    def store_tiles(acc, pattern):
        for t, (scale, ridx) in enumerate(pattern):
            x = acc[:, t * LANE:(t + 1) * LANE]
            if ridx is not None:
                a, bp, bm = (tab_refs[3 * ridx + k][...] for k in range(3))
                half = halves[ridx]
                x = x * a + pltpu.roll(x, half, 1) * bp + pltpu.roll(x, LANE - half, 1) * bm
            if scale != 1.0:
                x = x * scale
            o_ref[:, t * LANE:(t + 1) * LANE] = x.astype(o_ref.dtype)

    def epilogue(acc):
        if act == "relu2":
            acc = jnp.square(jnp.maximum(acc, 0.0))
        if has_res:
            acc = acc + r_ref[...]
        if groups is None:
            o_ref[...] = acc.astype(o_ref.dtype)
        elif len(groups) == 1:
            store_tiles(acc, groups[0][0])
        else:
            j = pl.program_id(1)
            for pattern, js in groups:
                cond = functools.reduce(jnp.logical_or, [j == jj for jj in js])
                pl.when(cond)(functools.partial(store_tiles, acc, pattern))

    part, k0 = None, 0
    for x_ref in x_refs:
        kw = x_ref.shape[1]
        d = jnp.dot(x_ref[...], w_ref[k0:k0 + kw, :], preferred_element_type=F32)
        part = d if part is None else part + d
        k0 += kw
    if nk == 1:
        epilogue(part)
    else:
        acc_ref = scratch[0]
        k = pl.program_id(2)

        @pl.when(k == 0)
        def _():
            acc_ref[...] = part

        @pl.when(jnp.logical_and(k > 0, k < nk - 1))
        def _():
            acc_ref[...] += part

        @pl.when(k == nk - 1)
        def _():
            epilogue(acc_ref[...] + part)


def matmul(x, w, out_dtype, *, name, layer=None, act=None, res=None, bm=1024, bn=1024, bk=4096,
           n_out=None, tile_modes=None, rope_tabs=(), seq=None):
    xs = x if isinstance(x, (tuple, list)) else (x,)
    m = xs[0].shape[0]
    kdim = sum(a.shape[1] for a in xs)
    assert kdim == w.shape[-2]
    n = w.shape[-1] if n_out is None else n_out
    bm, bn, bk = min(bm, m), min(bn, n), min(bk, kdim)
    assert m % bm == 0 and n % bn == 0 and kdim % bk == 0
    nk = kdim // bk
    assert len(xs) == 1 or nk == 1
    if len(xs) == 1:
        in_specs = [pl.BlockSpec((bm, bk), lambda i, j, k: (i, k))]
    else:
        in_specs = [pl.BlockSpec((bm, a.shape[1]), lambda i, j, k: (i, 0)) for a in xs]
    if layer is None:
        in_specs.append(pl.BlockSpec((bk, bn), lambda i, j, k: (k, j)))
    else:
        in_specs.append(pl.BlockSpec((None, bk, bn), lambda i, j, k: (layer, k, j)))
    args = [*xs, w]
    if res is not None:
        in_specs.append(pl.BlockSpec((bm, bn), lambda i, j, k: (i, j)))
        args.append(res)
    groups = None
    if tile_modes is not None:
        assert len(tile_modes) == n // LANE and seq % bm == 0
        per_block = bn // LANE
        by_pattern = {}
        for j in range(n // bn):
            by_pattern.setdefault(tuple(tile_modes[j * per_block:(j + 1) * per_block]), []).append(j)
        groups = tuple((p, tuple(js)) for p, js in by_pattern.items())
        nseq = seq // bm
        for a, bp, bmt, _ in rope_tabs:
            in_specs += [pl.BlockSpec((bm, LANE), lambda i, j, k: (i % nseq, 0))] * 3
            args += [a, bp, bmt]
    scratch = [pltpu.VMEM((bm, bn), F32)] if nk > 1 else []
    return pl.pallas_call(
        functools.partial(_mm_kernel, nx=len(xs), act=act, has_res=res is not None, nk=nk,
                          groups=groups,
                          halves=tuple(t[3] for t in rope_tabs), ntab=len(rope_tabs)),
        grid=(m // bm, n // bn, nk),
        in_specs=in_specs,
        out_specs=pl.BlockSpec((bm, bn), lambda i, j, k: (i, j)),
        out_shape=jax.ShapeDtypeStruct((m, n), out_dtype),
        scratch_shapes=scratch,
        compiler_params=_cparams(("parallel", "parallel", "arbitrary")),
        name=name,
    )(*args)


def _rope_tables(seq, theta, rot_dim):
    half = rot_dim // 2
    inv = jnp.exp(-math.log(theta) * jnp.arange(half, dtype=F32) * (2.0 / rot_dim))
    ang = jnp.arange(seq, dtype=F32)[:, None] * inv[None, :]
    cos, sin = jnp.cos(ang), jnp.sin(ang)
    pad = LANE - rot_dim
    a = jnp.concatenate([cos, cos, jnp.ones((seq, pad), F32)], axis=1)
    bp = jnp.concatenate([jnp.zeros((seq, half), F32), sin, jnp.zeros((seq, pad), F32)], axis=1)
    bm = jnp.concatenate([-sin, jnp.zeros((seq, half + pad), F32)], axis=1)
    return a, bp, bm, half


def _retention_kernel(q_ref, k_ref, v_ref, g_ref, cos_ref, sp_ref, sm_ref, hmask_ref, idec_ref,
                      qdec_ref, kdec_ref, cdec_ref, bmask_ref, o_ref, state_ref):
    half = RET_QK_DIM // 2

    @pl.when(pl.program_id(1) == 0)
    def _():
        state_ref[...] = jnp.zeros_like(state_ref)

    cos, sp, sm = cos_ref[...], sp_ref[...], sm_ref[...]

    def rot(x):
        tiles = []
        for t in range(x.shape[1] // LANE):
            xt = x[:, t * LANE:(t + 1) * LANE]
            tiles.append(xt * cos + pltpu.roll(xt, half, 1) * sp
                         + pltpu.roll(xt, LANE - half, 1) * sm)
        return jnp.concatenate(tiles, axis=1)

    q = rot(q_ref[...].astype(F32))
    k = rot(k_ref[...].astype(F32)) * (RET_QK_DIM ** -0.5)
    v = v_ref[...].astype(F32)
    kt = k.T.astype(BF16)
    state = state_ref[...]
    cross = jnp.dot(q.astype(BF16), state.astype(BF16), preferred_element_type=F32) * qdec_ref[...]
    for h in range(RET_HEADS):
        cols = slice(h * RET_V_DIM, (h + 1) * RET_V_DIM)
        qm = (q * hmask_ref[h:h + 1, :]).astype(BF16)
        scores = jnp.dot(qm, kt, preferred_element_type=F32) * idec_ref[h]
        inner = jnp.dot(scores.astype(BF16), v[:, cols].astype(BF16), preferred_element_type=F32)
        o = inner + cross[:, cols]
        o = o * lax.rsqrt(jnp.mean(o * o, axis=-1, keepdims=True) + NORM_EPS)
        gate = g_ref[:, cols].astype(F32)
        o_ref[:, cols] = (gate * jax.nn.sigmoid(gate) * o).astype(o_ref.dtype)
    vd = (v * kdec_ref[...]).astype(BF16)
    upd = jnp.dot(kt, vd, preferred_element_type=F32)
    state_ref[...] = state * cdec_ref[...] + upd * bmask_ref[...]


def _retention_tables(seq):
    h, c, dv, dk = RET_HEADS, RET_CHUNK, RET_V_DIM, RET_QK_DIM
    half = dk // 2
    inv = jnp.exp(-math.log(RET_ROPE_THETA) * jnp.arange(half, dtype=F32) * (2.0 / dk))
    ang = jnp.arange(seq, dtype=F32)[:, None] * inv[None, :]
    cos1, sin1, zero = jnp.cos(ang), jnp.sin(ang), jnp.zeros((seq, half), F32)
    reps = LANE // dk
    cos = jnp.tile(jnp.concatenate([cos1, cos1], axis=1), (1, reps))
    sp = jnp.tile(jnp.concatenate([zero, sin1], axis=1), (1, reps))
    sm = jnp.tile(jnp.concatenate([-sin1, zero], axis=1), (1, reps))
    log_gamma = jnp.log1p(-jnp.exp2(-5.0 - jnp.arange(h, dtype=F32)))
    idx = jnp.arange(c, dtype=F32)
    rel = idx[:, None] - idx[None, :]
    idec = jnp.where(rel[None] >= 0,
                     jnp.exp(jnp.maximum(rel, 0.0)[None] * log_gamma[:, None, None]), 0.0)
    qdec = jnp.exp((idx + 1.0)[:, None] * log_gamma[None, :])
    kdec = jnp.exp((c - 1.0 - idx)[:, None] * log_gamma[None, :])
    cdec = jnp.exp(c * log_gamma)[None, :]
    rep = lambda a: jnp.repeat(a, dv, axis=1)
    lane_head = np.arange(h * dk) // dk
    hmask = jnp.asarray(lane_head[None, :] == np.arange(h)[:, None], F32)
    bmask = jnp.asarray(lane_head[:, None] == (np.arange(h * dv) // dv)[None, :], F32)
    return cos, sp, sm, hmask, idec, rep(qdec), rep(kdec), rep(cdec), bmask


def retention(proj, batch, seq, tables):
    c = RET_CHUNK
    nch = seq // c
    qw = RET_HEADS * RET_QK_DIM
    col = lambda name: _NAT[name][0] // _NAT[name][1]
    row = lambda b, n: b * nch + n
    const2 = lambda b, n: (0, 0)
    cos, sp, sm, hmask, idec, qdec, kdec, cdec, bmask = tables
    tab = pl.BlockSpec((c, LANE), lambda b, n: (n, 0))
    return pl.pallas_call(
        _retention_kernel,
        grid=(batch, nch),
        in_specs=[pl.BlockSpec((c, qw), lambda b, n: (row(b, n), col("rq"))),
                  pl.BlockSpec((c, qw), lambda b, n: (row(b, n), col("rk"))),
                  pl.BlockSpec((c, RET_OUT), lambda b, n: (row(b, n), col("rv"))),
                  pl.BlockSpec((c, RET_OUT), lambda b, n: (row(b, n), col("rg"))),
                  tab, tab, tab,
                  pl.BlockSpec(hmask.shape, const2),
                  pl.BlockSpec(idec.shape, lambda b, n: (0, 0, 0)),
                  pl.BlockSpec(qdec.shape, const2),
                  pl.BlockSpec(kdec.shape, const2),
                  pl.BlockSpec(cdec.shape, const2),
                  pl.BlockSpec(bmask.shape, const2)],
        out_specs=pl.BlockSpec((c, RET_OUT), lambda b, n: (row(b, n), 0)),
        out_shape=jax.ShapeDtypeStruct((batch * seq, RET_OUT), BF16),
        scratch_shapes=[pltpu.VMEM((qw, RET_OUT), F32)],
        compiler_params=_cparams(("parallel", "arbitrary")),
        name="retention",
    )(proj, proj, proj, proj, cos, sp, sm, hmask, idec, qdec, kdec, cdec, bmask)


LOG2E = math.log2(math.e)
M_INIT = float(np.finfo(np.float32).min)


def _flash_update(s, v_c, m_ref, l_ref, acc_ref, h):
    reps = s.shape[1] // LANE

    def fold(x, op):
        tiles = [x[:, r * LANE:(r + 1) * LANE] for r in range(reps)]
        while len(tiles) > 1:
            tiles = [op(a, b) for a, b in zip(tiles[::2], tiles[1::2])]
        return tiles[0]

    m_prev = m_ref[h]
    m_new = jnp.maximum(m_prev, jnp.max(fold(s, jnp.maximum), axis=1, keepdims=True))
    p = jnp.exp2(s - jnp.tile(m_new, (1, reps)))
    alpha = jnp.exp2(m_prev - m_new)
    l_ref[h] = alpha * l_ref[h] + jnp.sum(fold(p, jnp.add), axis=1, keepdims=True)
    m_ref[h] = m_new
    acc_ref[h] = acc_ref[h] * alpha + jnp.dot(p.astype(BF16), v_c, preferred_element_type=F32)


INT_MIN = -2 ** 31
IDX_GROUP = 8
ROW_BLOCK = 64
PASSES_PER_CHECK = 4
COUNT_ROWS = 32
KV_PER_LOOP = 4


def _dsa_kernel(*refs, tq, topk, nqi):
    qi_refs = refs[:nqi]
    (w_ref, kidx_ref, q_ref, k_ref, v_ref, o_ref,
     qis_ref, wb_ref, lg_ref, key_ref, bias_ref, qs_ref, m_ref, l_ref, acc_ref) = refs[nqi:]
    tk = tq
    i = pl.program_id(1)
    nchunks = i + 1
    reps = tk // LANE
    ngroups = IDX_HEADS // IDX_GROUP
    group = DSA_HEADS // DSA_KV_HEADS
    d = DSA_HEAD_DIM
    heads_per_ref = IDX_HEADS // nqi

    w = w_ref[...].astype(F32) * (IDX_HEADS ** -0.5)
    for h in range(IDX_HEADS):
        r, t = divmod(h, heads_per_ref)
        qis_ref[h * tq:(h + 1) * tq, :] = qi_refs[r][:, t * LANE:(t + 1) * LANE]
        wb_ref[h] = jnp.broadcast_to(w[:, h:h + 1], (tq, LANE))

    def score_chunk(c, diagonal):
        r0 = pl.multiple_of(c * tk, tk)
        kc = kidx_ref[pl.ds(r0, tk), :]
        for g in range(ngroups):
            lg_ref[g] = lax.dot_general(
                qis_ref[g * IDX_GROUP * tq:(g + 1) * IDX_GROUP * tq, :], kc,
                (((1,), (1,)), ((), ())), preferred_element_type=F32)
        for rb in range(tq // ROW_BLOCK):
            q0 = rb * ROW_BLOCK
            rows = slice(q0, q0 + ROW_BLOCK)
            acc = jnp.zeros((ROW_BLOCK, tk), F32)
            for h in range(IDX_HEADS):
                g, hg = divmod(h, IDX_GROUP)
                lg = lg_ref[g, hg * tq + q0:hg * tq + q0 + ROW_BLOCK, :]
                acc = acc + jnp.maximum(lg, 0.0) * jnp.tile(wb_ref[h, rows, :], (1, reps))
            bits = pltpu.bitcast(acc, jnp.int32)
            key = bits ^ (lax.shift_right_arithmetic(bits, 31) & 0x7FFFFFFF)
            if diagonal:
                kpos = lax.broadcasted_iota(jnp.int32, (ROW_BLOCK, tk), 1)
                qpos = q0 + lax.broadcasted_iota(jnp.int32, (ROW_BLOCK, tk), 0)
                key = jnp.where(kpos <= qpos, key, INT_MIN)
            key_ref[c, rows, :] = key
        bias_ref[c] = pltpu.bitcast(key_ref[c].T, F32)

    def score_body(c, _):
        score_chunk(c, False)
        return 0

    lax.fori_loop(0, i, score_body, 0)
    score_chunk(i, True)

    def count_ge(cand):
        def body(c, cnt):
            ge = (pltpu.bitcast(bias_ref[c], jnp.int32) >= cand).astype(jnp.int32)
            return cnt + jnp.sum(ge.reshape(tk // COUNT_ROWS, COUNT_ROWS, tq), axis=0)
        cnt = lax.fori_loop(0, nchunks, body, jnp.zeros((COUNT_ROWS, tq), jnp.int32))
        return jnp.sum(cnt, axis=0, keepdims=True)

    def bit_cond(state):
        b, _, cnt = state
        return jnp.logical_and(b < 32, jnp.max(jnp.abs(cnt - topk)) > 0)

    def bit_body(state):
        b, t, cnt = state
        for u in range(PASSES_PER_CHECK):
            cand = t ^ lax.shift_left(jnp.int32(1), 31 - (b + u))
            n = count_ge(cand)
            take = n >= topk
            t, cnt = jnp.where(take, cand, t), jnp.where(take, n, cnt)
        return b + PASSES_PER_CHECK, t, cnt

    t0 = jnp.full((1, tq), INT_MIN, jnp.int32)
    c0 = jnp.full((1, tq), nchunks * tk, jnp.int32)
    _, thr_row, _ = lax.while_loop(bit_cond, bit_body, (jnp.int32(0), t0, c0))
    thr_row = jnp.maximum(thr_row, INT_MIN + 1)
    thr = jnp.broadcast_to(thr_row, (LANE, tq)).T
    thr = jnp.tile(thr, (1, reps))

    def bias_body(c, _):
        bias_ref[c] = jnp.where(key_ref[c] >= thr, 0.0, -jnp.inf)
        return 0

    lax.fori_loop(0, nchunks, bias_body, 0)

    @pl.when(nchunks < key_ref.shape[0])
    def _():
        bias_ref[nchunks] = jnp.full((tq, tk), -jnp.inf, F32)

    hpl = KV_PER_LOOP * group
    for g0 in range(0, DSA_KV_HEADS, KV_PER_LOOP):
        for hh in range(hpl):
            h = g0 * group + hh
            qs_ref[hh * tq:(hh + 1) * tq, :] = q_ref[:, h * d:(h + 1) * d]
        m_ref[...] = jnp.full_like(m_ref, M_INIT)
        l_ref[...] = jnp.zeros_like(l_ref)
        acc_ref[...] = jnp.zeros_like(acc_ref)

        def att_body(c2, _, g0=g0):
            r0 = pl.multiple_of(c2 * 2 * tk, 2 * tk)
            bias = jnp.concatenate([bias_ref[2 * c2], bias_ref[2 * c2 + 1]], axis=1)
            for gg in range(KV_PER_LOOP):
                g = g0 + gg
                kc = k_ref[pl.ds(r0, 2 * tk), g * d:(g + 1) * d]
                vc = v_ref[pl.ds(r0, 2 * tk), g * d:(g + 1) * d]
                s_all = lax.dot_general(qs_ref[gg * group * tq:(gg + 1) * group * tq, :], kc,
                                        (((1,), (1,)), ((), ())), preferred_element_type=F32)
                for hh in range(group):
                    _flash_update(s_all[hh * tq:(hh + 1) * tq] + bias, vc, m_ref, l_ref, acc_ref,
                                  gg * group + hh)
            return 0

        lax.fori_loop(0, (nchunks + 1) // 2, att_body, 0)
        for hh in range(hpl):
            h = g0 * group + hh
            o_ref[:, h * d:(h + 1) * d] = (acc_ref[hh] / l_ref[hh]).astype(o_ref.dtype)


def dsa_attention(proj, tail, batch, seq, *, tq=256):
    nq = seq // tq
    assert nq % 2 == 0
    topk = min(DSA_TOPK, seq // 4)
    hpl = KV_PER_LOOP * DSA_HEADS // DSA_KV_HEADS
    kvw = DSA_KV_HEADS * DSA_HEAD_DIM
    iqw = IDX_HEADS * IDX_HEAD_DIM
    nqi = iqw // QI_BLOCK
    col = lambda name, width: _NAT[name][0] // width
    qi_specs = [pl.BlockSpec((tq, QI_BLOCK), lambda b, i, r=r: (b * nq + i, col("iq", QI_BLOCK) + r))
                for r in range(nqi)]
    return pl.pallas_call(
        functools.partial(_dsa_kernel, tq=tq, topk=topk, nqi=nqi),
        grid=(batch, nq),
        in_specs=qi_specs + [
            pl.BlockSpec((tq, LANE), lambda b, i: (b * nq + i, _TAIL["iw"][0] // LANE)),
            pl.BlockSpec((seq, IDX_HEAD_DIM), lambda b, i: (b, col("ik", IDX_HEAD_DIM))),
            pl.BlockSpec((tq, DSA_OUT), lambda b, i: (b * nq + i, col("dq", DSA_OUT))),
            pl.BlockSpec((seq, kvw), lambda b, i: (b, col("dk", kvw))),
            pl.BlockSpec((seq, kvw), lambda b, i: (b, col("dv", kvw)))],
        out_specs=pl.BlockSpec((tq, DSA_OUT), lambda b, i: (b * nq + i, 0)),
        out_shape=jax.ShapeDtypeStruct((batch * seq, DSA_OUT), BF16),
        scratch_shapes=[pltpu.VMEM((IDX_HEADS * tq, IDX_HEAD_DIM), BF16),
                        pltpu.VMEM((IDX_HEADS, tq, LANE), F32),
                        pltpu.VMEM((IDX_HEADS // IDX_GROUP, IDX_GROUP * tq, tq), F32),
                        pltpu.VMEM((nq, tq, tq), jnp.int32),
                        pltpu.VMEM((nq, tq, tq), F32),
                        pltpu.VMEM((hpl * tq, DSA_HEAD_DIM), BF16),
                        pltpu.VMEM((hpl, tq, LANE), F32),
                        pltpu.VMEM((hpl, tq, LANE), F32),
                        pltpu.VMEM((hpl, tq, DSA_HEAD_DIM), F32)],
        compiler_params=_cparams(("parallel", "arbitrary")),
        name="dsa_attention",
    )(*([proj] * nqi), tail, proj, proj, proj, proj)


def _mla_kernel(q_ref, kn_ref, kpe_ref, v_ref, o_ref, m_ref, l_ref, acc_ref, *, tq, hb):
    tk = tq
    i = pl.program_id(2)
    m_ref[...] = jnp.full_like(m_ref, M_INIT)
    l_ref[...] = jnp.zeros_like(l_ref)
    acc_ref[...] = jnp.zeros_like(acc_ref)
    causal = (lax.broadcasted_iota(jnp.int32, (tq, tk), 1)
              <= lax.broadcasted_iota(jnp.int32, (tq, tk), 0))

    def step(c, masked):
        r0 = pl.multiple_of(c * tk, tk)
        kpe = kpe_ref[pl.ds(r0, tk), :]
        for h in range(hb):
            cols = slice(h * MLA_NOPE_DIM, (h + 1) * MLA_NOPE_DIM)
            kf = jnp.concatenate([kn_ref[pl.ds(r0, tk), cols], kpe], axis=1)
            s = lax.dot_general(q_ref[:, h * MLA_QK_PAD:(h + 1) * MLA_QK_PAD], kf,
                                (((1,), (1,)), ((), ())), preferred_element_type=F32)
            if masked:
                s = jnp.where(causal, s, -jnp.inf)
            _flash_update(s, v_ref[pl.ds(r0, tk), cols], m_ref, l_ref, acc_ref, h)

    def body(c, _):
        step(c, False)
        return 0

    lax.fori_loop(0, i, body, 0)
    step(i, True)
    for h in range(hb):
        o_ref[:, h * MLA_V_DIM:(h + 1) * MLA_V_DIM] = (acc_ref[h] / l_ref[h]).astype(o_ref.dtype)


def mla_attention(q, kv, tail, batch, seq, *, tq=512, hb=4):
    nq = seq // tq
    ng = MLA_HEADS // hb
    return pl.pallas_call(
        functools.partial(_mla_kernel, tq=tq, hb=hb),
        grid=(batch, ng, nq),
        in_specs=[pl.BlockSpec((tq, hb * MLA_QK_PAD), lambda b, g, i: (b * nq + i, g)),
                  pl.BlockSpec((seq, hb * MLA_NOPE_DIM), lambda b, g, i: (b, g)),
                  pl.BlockSpec((seq, LANE), lambda b, g, i: (b, _TAIL["kr"][0] // LANE)),
                  pl.BlockSpec((seq, hb * MLA_V_DIM), lambda b, g, i: (b, ng + g))],
        out_specs=pl.BlockSpec((tq, hb * MLA_V_DIM), lambda b, g, i: (b * nq + i, g)),
        out_shape=jax.ShapeDtypeStruct((batch * seq, MLA_OUT), BF16),
        scratch_shapes=[pltpu.VMEM((hb, tq, LANE), F32), pltpu.VMEM((hb, tq, LANE), F32),
                        pltpu.VMEM((hb, tq, MLA_V_DIM), F32)],
        compiler_params=_cparams(("parallel", "parallel", "arbitrary")),
        name="mla_attention",
    )(q, kv, tail, kv)


def _tail_w_in(w):
    parts, pos = [], 0
    for name in sorted(_TAIL, key=lambda s: _TAIL[s][0]):
        off, width = _TAIL[name]
        assert off == pos
        a = w[..., _NAT[name][0]:_NAT[name][0] + _NAT[name][1]]
        parts.append(a)
        if a.shape[-1] < width:
            parts.append(jnp.zeros((*w.shape[:-1], width - a.shape[-1]), w.dtype))
        pos += width
    assert pos == TAIL_WIDTH
    return jnp.concatenate(parts, axis=-1).astype(BF16)


def _relayout_w_uq(w):
    dep, r = w.shape[:2]
    a = w.reshape(dep, r, MLA_HEADS, MLA_NOPE_DIM + MLA_ROPE_DIM)
    a = jnp.pad(a, ((0, 0), (0, 0), (0, 0), (0, MLA_QK_PAD - a.shape[3])))
    return a.reshape(dep, r, MLA_HEADS * MLA_QK_PAD).astype(BF16)


def _relayout_w_ukv(w):
    dep, r = w.shape[:2]
    a = w.reshape(dep, r, MLA_HEADS, 2, MLA_NOPE_DIM).transpose(0, 1, 3, 2, 4)
    return a.reshape(dep, r, 2 * MLA_HEADS * MLA_NOPE_DIM).astype(BF16)


def _main_tile_modes():
    modes = [(1.0, None)] * (MAIN_WIDTH // LANE)

    def fill(name, mode):
        off, width = _NAT[name]
        for t in range(off // LANE, (off + width) // LANE):
            modes[t] = mode

    fill("dq", (DSA_HEAD_DIM ** -0.5 * LOG2E, 0))
    fill("dk", (1.0, 0))
    fill("iq", (IDX_HEAD_DIM ** -0.5, 0))
    fill("ik", (1.0, 0))
    return modes


def _tail_tile_modes():
    modes = [(1.0, None)] * (TAIL_WIDTH // LANE)
    modes[_TAIL["kr"][0] // LANE] = (1.0, 0)
    return modes


def kernel(x, attn_norm, w_in, mla_q_norm, mla_kv_norm, w_uq, w_ukv, w_o, mlp_norm, w_up,
           w_down, final_norm):
    batch, seq, d_model = x.shape
    depth = w_in.shape[0]
    m = batch * seq

    ret_tables = _retention_tables(seq)
    part_tab = _rope_tables(seq, PARTIAL_ROPE_THETA, PARTIAL_ROT_DIM)
    mla_tab = _rope_tables(seq, MLA_ROPE_THETA, MLA_ROPE_DIM)
    mla_scale = (MLA_NOPE_DIM + MLA_ROPE_DIM) ** -0.5 * LOG2E
    uq_modes = [(mla_scale, None), (mla_scale, 0)] * MLA_HEADS

    w_in_b, w_o_b, w_up_b, w_down_b = (a.astype(BF16) for a in (w_in, w_o, w_up, w_down))
    w_tail_b = _tail_w_in(w_in)
    w_uq_b, w_ukv_b = _relayout_w_uq(w_uq), _relayout_w_ukv(w_ukv)

    h = x.reshape(m, d_model)
    for l in range(depth):
        u = rmsnorm(h, attn_norm[l], BF16)
        proj = matmul(u, w_in_b, BF16, layer=l, name="mm_in", n_out=MAIN_WIDTH, bn=MAIN_BN,
                      tile_modes=_main_tile_modes(), rope_tabs=(part_tab,), seq=seq)
        tail = matmul(u, w_tail_b, BF16, layer=l, name="mm_in_tail", bn=TAIL_WIDTH,
                      tile_modes=_tail_tile_modes(), rope_tabs=(mla_tab,), seq=seq)

        ret = retention(proj, batch, seq, ret_tables)
        dsa = dsa_attention(proj, tail, batch, seq)

        cq = rmsnorm(tail, mla_q_norm[l], BF16, col_off=_TAIL["cq"][0], width=MLA_Q_LORA)
        ckv = rmsnorm(tail, mla_kv_norm[l], BF16, col_off=_TAIL["ckv"][0], width=MLA_KV_LORA)
        q_full = matmul(cq, w_uq_b, BF16, layer=l, name="mm_uq", tile_modes=uq_modes,
                        rope_tabs=(mla_tab,), seq=seq)
        kv_up = matmul(ckv, w_ukv_b, BF16, layer=l, name="mm_ukv")
        mla = mla_attention(q_full, kv_up, tail, batch, seq)

        h = matmul((ret, dsa, mla), w_o_b, F32, layer=l, res=h, bn=512, name="mm_o")
        u = rmsnorm(h, mlp_norm[l], BF16)
        up = matmul(u, w_up_b, BF16, layer=l, act="relu2", name="mm_up")
        h = matmul(up, w_down_b, F32, layer=l, res=h, bn=512, name="mm_down")
    out = rmsnorm(h, final_norm, x.dtype)
    return out.reshape(batch, seq, d_model)
```

```python
import functools
import math

import numpy as np
import jax
import jax.numpy as jnp
from jax import lax
from jax.experimental import pallas as pl
from jax.experimental.pallas import tpu as pltpu

F32 = jnp.float32
BF16 = jnp.bfloat16

RET_HEADS = 8
RET_QK_DIM = 64
RET_V_DIM = 128
RET_CHUNK = 128
RET_ROPE_THETA = 10000.0
DSA_HEADS = 12
DSA_KV_HEADS = 4
DSA_HEAD_DIM = 128
IDX_HEADS = 32
IDX_HEAD_DIM = 128
DSA_TOPK = 256
PARTIAL_ROT_DIM = 32
PARTIAL_ROPE_THETA = 500000.0
MLA_HEADS = 12
MLA_Q_LORA = 768
MLA_KV_LORA = 256
MLA_NOPE_DIM = 128
MLA_ROPE_DIM = 64
MLA_V_DIM = 128
MLA_ROPE_THETA = 10000.0
NORM_EPS = 1e-6

LANE = 128
VMEM_LIMIT = 60 * 1024 * 1024

RET_OUT = RET_HEADS * RET_V_DIM
DSA_OUT = DSA_HEADS * DSA_HEAD_DIM
MLA_OUT = MLA_HEADS * MLA_V_DIM
MLA_QK_PAD = 256

_NAT = {}
_off = 0
for _name, _w in (("rq", 512), ("rk", 512), ("rv", 1024), ("rg", 1024), ("dq", 1536),
                  ("dk", 512), ("dv", 512), ("iq", 4096), ("ik", 128), ("iw", 32),
                  ("cq", 768), ("ckv", 256), ("kr", 64)):
    _NAT[_name] = (_off, _w)
    _off += _w
IN_WIDTH = _off
MAIN_WIDTH = 10240
MAIN_BN = 1024
_TAIL = {"cq": (0, 768), "ckv": (768, 256), "iw": (1024, 128), "kr": (1152, 128)}
TAIL_WIDTH = 1280
QI_BLOCK = 512


def _cparams(sem):
    return pltpu.CompilerParams(dimension_semantics=sem, vmem_limit_bytes=VMEM_LIMIT)


def _rmsnorm_kernel(x_ref, g_ref, o_ref):
    x = x_ref[...].astype(F32)
    ms = jnp.mean(x * x, axis=-1, keepdims=True)
    o_ref[...] = (x * lax.rsqrt(ms + NORM_EPS) * g_ref[...]).astype(o_ref.dtype)


def rmsnorm(x, g, out_dtype, *, col_off=0, width=None, bm=256):
    m = x.shape[0]
    width = x.shape[1] if width is None else width
    assert col_off % width == 0 and m % bm == 0
    cb = col_off // width
    return pl.pallas_call(
        _rmsnorm_kernel,
        grid=(m // bm,),
        in_specs=[pl.BlockSpec((bm, width), lambda i: (i, cb)),
                  pl.BlockSpec((1, width), lambda i: (0, 0))],
        out_specs=pl.BlockSpec((bm, width), lambda i: (i, 0)),
        out_shape=jax.ShapeDtypeStruct((m, width), out_dtype),
        compiler_params=_cparams(("parallel",)),
        name="rmsnorm",
    )(x, g.reshape(1, width).astype(F32))


def _mm_kernel(*refs, nx, act, has_res, has_norm, nk, groups, halves, ntab, has_cast):
    x_refs, w_ref = refs[:nx], refs[nx]
    pos = nx + 1
    r_ref = refs[pos] if has_res else None
    pos += int(has_res)
    g_ref = refs[pos] if has_norm else None
    pos += int(has_norm)
    tab_refs = refs[pos:pos + 3 * ntab]
    pos += 3 * ntab
    cast_in_ref = refs[pos] if has_cast else None
    pos += int(has_cast)
    o_ref = refs[pos]
    pos += 1
    if has_cast:
        refs[pos][...] = cast_in_ref[...].astype(BF16)
        pos += 1
    scratch = refs[pos:]

    def store_tiles(acc, pattern):
        for t, (scale, ridx) in enumerate(pattern):
            x = acc[:, t * LANE:(t + 1) * LANE]
            if ridx is not None:
                a, bp, bm = (tab_refs[3 * ridx + k][...] for k in range(3))
                half = halves[ridx]
                x = x * a + pltpu.roll(x, half, 1) * bp + pltpu.roll(x, LANE - half, 1) * bm
            if scale != 1.0:
                x = x * scale
            o_ref[:, t * LANE:(t + 1) * LANE] = x.astype(o_ref.dtype)

    def epilogue(acc):
        if act == "relu2":
            acc = jnp.square(jnp.maximum(acc, 0.0))
        if has_res:
            acc = acc + r_ref[...]
        if groups is None:
            o_ref[...] = acc.astype(o_ref.dtype)
        elif len(groups) == 1:
            store_tiles(acc, groups[0][0])
        else:
            j = pl.program_id(1)
            for pattern, js in groups:
                cond = functools.reduce(jnp.logical_or, [j == jj for jj in js])
                pl.when(cond)(functools.partial(store_tiles, acc, pattern))

    part, k0 = None, 0
    for x_ref in x_refs:
        kw = x_ref.shape[1]
        x = x_ref[...]
        if g_ref is not None:
            xf = x.astype(F32)
            ms = jnp.mean(xf * xf, axis=-1, keepdims=True)
            x = (xf * lax.rsqrt(ms + NORM_EPS) * g_ref[...]).astype(BF16)
        d = jnp.dot(x, w_ref[k0:k0 + kw, :], preferred_element_type=F32)
        part = d if part is None else part + d
        k0 += kw
    if nk == 1:
        epilogue(part)
    else:
        acc_ref = scratch[0]
        k = pl.program_id(2)

        @pl.when(k == 0)
        def _():
            acc_ref[...] = part

        @pl.when(jnp.logical_and(k > 0, k < nk - 1))
        def _():
            acc_ref[...] += part

        @pl.when(k == nk - 1)
        def _():
            epilogue(acc_ref[...] + part)


def matmul(x, w, out_dtype, *, name, layer=None, act=None, res=None, bm=1024, bn=1024, bk=4096,
           n_out=None, tile_modes=None, rope_tabs=(), seq=None, cast=None, x_cols=None,
           prenorm=None):
    xs = x if isinstance(x, (tuple, list)) else (x,)
    m = xs[0].shape[0]
    kdim = sum(a.shape[1] for a in xs) if x_cols is None else x_cols[1]
    assert kdim == w.shape[-2]
    n = w.shape[-1] if n_out is None else n_out
    bm, bn, bk = min(bm, m), min(bn, n), min(bk, kdim)
    assert m % bm == 0 and n % bn == 0 and kdim % bk == 0
    nk = kdim // bk
    assert len(xs) == 1 or nk == 1
    if x_cols is not None:
        assert len(xs) == 1 and nk == 1 and x_cols[0] % kdim == 0
        xcb = x_cols[0] // kdim
        in_specs = [pl.BlockSpec((bm, kdim), lambda i, j, k: (i, xcb))]
    elif len(xs) == 1:
        in_specs = [pl.BlockSpec((bm, bk), lambda i, j, k: (i, k))]
    else:
        in_specs = [pl.BlockSpec((bm, a.shape[1]), lambda i, j, k: (i, 0)) for a in xs]
    if layer is None:
        in_specs.append(pl.BlockSpec((bk, bn), lambda i, j, k: (k, j)))
    else:
        in_specs.append(pl.BlockSpec((None, bk, bn), lambda i, j, k: (layer, k, j)))
    args = [*xs, w]
    if res is not None:
        in_specs.append(pl.BlockSpec((bm, bn), lambda i, j, k: (i, j)))
        args.append(res)
    if prenorm is not None:
        assert nk == 1 and len(xs) == 1
        in_specs.append(pl.BlockSpec((1, kdim), lambda i, j, k: (0, 0)))
        args.append(prenorm.reshape(1, kdim).astype(F32))
    groups = None
    if tile_modes is not None:
        assert len(tile_modes) == n // LANE and seq % bm == 0
        per_block = bn // LANE
        by_pattern = {}
        for j in range(n // bn):
            by_pattern.setdefault(tuple(tile_modes[j * per_block:(j + 1) * per_block]), []).append(j)
        groups = tuple((p, tuple(js)) for p, js in by_pattern.items())
        nseq = seq // bm
        for a, bp, bmt, _ in rope_tabs:
            in_specs += [pl.BlockSpec((bm, LANE), lambda i, j, k: (i % nseq, 0))] * 3
            args += [a, bp, bmt]
    out_specs = pl.BlockSpec((bm, bn), lambda i, j, k: (i, j))
    out_shape = jax.ShapeDtypeStruct((m, n), out_dtype)
    sem = ("parallel", "parallel", "arbitrary")
    if cast is not None:
        spec_in, spec_out, shape_out = _cast_specs(cast, (m // bm, n // bn, nk))
        in_specs.append(spec_in)
        args.append(cast[0])
        out_specs, out_shape = [out_specs, spec_out], [out_shape, shape_out]
        sem = ("arbitrary",) * 3
    scratch = [pltpu.VMEM((bm, bn), F32)] if nk > 1 else []
    return pl.pallas_call(
        functools.partial(_mm_kernel, nx=len(xs), act=act, has_res=res is not None,
                          has_norm=prenorm is not None, nk=nk,
                          groups=groups, halves=tuple(t[3] for t in rope_tabs),
                          ntab=len(rope_tabs), has_cast=cast is not None),
        grid=(m // bm, n // bn, nk),
        in_specs=in_specs,
        out_specs=out_specs,
        out_shape=out_shape,
        scratch_shapes=scratch,
        compiler_params=_cparams(sem),
        name=name,
    )(*args)


def _cast_specs(cast, grid):
    src, layer = cast
    _, rows, cols = src.shape
    nsteps = math.prod(grid)
    rb = 16 * max(1, -(-rows // (16 * nsteps)))
    while rows % rb:
        rb += 16
    nb = rows // rb

    def step(*ids):
        s = 0
        for g, i in zip(grid, ids):
            s = s * g + i
        return jnp.minimum(s, nb - 1)

    spec_in = pl.BlockSpec((None, rb, cols), lambda *ids: (layer, step(*ids), 0))
    spec_out = pl.BlockSpec((rb, cols), lambda *ids: (step(*ids), 0))
    return spec_in, spec_out, jax.ShapeDtypeStruct((rows, cols), BF16)


def _rope_tables(seq, theta, rot_dim):
    half = rot_dim // 2
    inv = jnp.exp(-math.log(theta) * jnp.arange(half, dtype=F32) * (2.0 / rot_dim))
    ang = jnp.arange(seq, dtype=F32)[:, None] * inv[None, :]
    cos, sin = jnp.cos(ang), jnp.sin(ang)
    pad = LANE - rot_dim
    a = jnp.concatenate([cos, cos, jnp.ones((seq, pad), F32)], axis=1)
    bp = jnp.concatenate([jnp.zeros((seq, half), F32), sin, jnp.zeros((seq, pad), F32)], axis=1)
    bm = jnp.concatenate([-sin, jnp.zeros((seq, half + pad), F32)], axis=1)
    return a, bp, bm, half


def _retention_kernel(cast_in_ref, q_ref, k_ref, v_ref, g_ref, cos_ref, sp_ref, sm_ref, hmask_ref,
                      idec_ref, qdec_ref, kdec_ref, cdec_ref, bmask_ref, o_ref, cast_out_ref,
                      state_ref):
    half = RET_QK_DIM // 2
    cast_out_ref[...] = cast_in_ref[...].astype(BF16)

    @pl.when(pl.program_id(1) == 0)
    def _():
        state_ref[...] = jnp.zeros_like(state_ref)

    cos, sp, sm = cos_ref[...], sp_ref[...], sm_ref[...]

    def rot(x):
        tiles = []
        for t in range(x.shape[1] // LANE):
            xt = x[:, t * LANE:(t + 1) * LANE]
            tiles.append(xt * cos + pltpu.roll(xt, half, 1) * sp
                         + pltpu.roll(xt, LANE - half, 1) * sm)
        return jnp.concatenate(tiles, axis=1)

    q = rot(q_ref[...].astype(F32))
    k = rot(k_ref[...].astype(F32)) * (RET_QK_DIM ** -0.5)
    v = v_ref[...].astype(F32)
    kt = k.T.astype(BF16)
    state = state_ref[...]
    cross = jnp.dot(q.astype(BF16), state.astype(BF16), preferred_element_type=F32) * qdec_ref[...]
    for h in range(RET_HEADS):
        cols = slice(h * RET_V_DIM, (h + 1) * RET_V_DIM)
        qm = (q * hmask_ref[h:h + 1, :]).astype(BF16)
        scores = jnp.dot(qm, kt, preferred_element_type=F32) * idec_ref[h]
        inner = jnp.dot(scores.astype(BF16), v[:, cols].astype(BF16), preferred_element_type=F32)
        o = inner + cross[:, cols]
        o = o * lax.rsqrt(jnp.mean(o * o, axis=-1, keepdims=True) + NORM_EPS)
        gate = g_ref[:, cols].astype(F32)
        o_ref[:, cols] = (gate * jax.nn.sigmoid(gate) * o).astype(o_ref.dtype)
    vd = (v * kdec_ref[...]).astype(BF16)
    upd = jnp.dot(kt, vd, preferred_element_type=F32)
    state_ref[...] = state * cdec_ref[...] + upd * bmask_ref[...]


def _retention_tables(seq):
    h, c, dv, dk = RET_HEADS, RET_CHUNK, RET_V_DIM, RET_QK_DIM
    half = dk // 2
    inv = jnp.exp(-math.log(RET_ROPE_THETA) * jnp.arange(half, dtype=F32) * (2.0 / dk))
    ang = jnp.arange(seq, dtype=F32)[:, None] * inv[None, :]
    cos1, sin1, zero = jnp.cos(ang), jnp.sin(ang), jnp.zeros((seq, half), F32)
    reps = LANE // dk
    cos = jnp.tile(jnp.concatenate([cos1, cos1], axis=1), (1, reps))
    sp = jnp.tile(jnp.concatenate([zero, sin1], axis=1), (1, reps))
    sm = jnp.tile(jnp.concatenate([-sin1, zero], axis=1), (1, reps))
    log_gamma = jnp.log1p(-jnp.exp2(-5.0 - jnp.arange(h, dtype=F32)))
    idx = jnp.arange(c, dtype=F32)
    rel = idx[:, None] - idx[None, :]
    idec = jnp.where(rel[None] >= 0,
                     jnp.exp(jnp.maximum(rel, 0.0)[None] * log_gamma[:, None, None]), 0.0)
    qdec = jnp.exp((idx + 1.0)[:, None] * log_gamma[None, :])
    kdec = jnp.exp((c - 1.0 - idx)[:, None] * log_gamma[None, :])
    cdec = jnp.exp(c * log_gamma)[None, :]
    rep = lambda a: jnp.repeat(a, dv, axis=1)
    lane_head = np.arange(h * dk) // dk
    hmask = jnp.asarray(lane_head[None, :] == np.arange(h)[:, None], F32)
    bmask = jnp.asarray(lane_head[:, None] == (np.arange(h * dv) // dv)[None, :], F32)
    return cos, sp, sm, hmask, idec, rep(qdec), rep(kdec), rep(cdec), bmask


def retention(proj, batch, seq, tables, cast):
    c = RET_CHUNK
    nch = seq // c
    qw = RET_HEADS * RET_QK_DIM
    col = lambda name: _NAT[name][0] // _NAT[name][1]
    row = lambda b, n: b * nch + n
    const2 = lambda b, n: (0, 0)
    cos, sp, sm, hmask, idec, qdec, kdec, cdec, bmask = tables
    tab = pl.BlockSpec((c, LANE), lambda b, n: (n, 0))
    cast_in, cast_out, cast_shape = _cast_specs(cast, (batch, nch))
    return pl.pallas_call(
        _retention_kernel,
        grid=(batch, nch),
        in_specs=[cast_in,
                  pl.BlockSpec((c, qw), lambda b, n: (row(b, n), col("rq"))),
                  pl.BlockSpec((c, qw), lambda b, n: (row(b, n), col("rk"))),
                  pl.BlockSpec((c, RET_OUT), lambda b, n: (row(b, n), col("rv"))),
                  pl.BlockSpec((c, RET_OUT), lambda b, n: (row(b, n), col("rg"))),
                  tab, tab, tab,
                  pl.BlockSpec(hmask.shape, const2),
                  pl.BlockSpec(idec.shape, lambda b, n: (0, 0, 0)),
                  pl.BlockSpec(qdec.shape, const2),
                  pl.BlockSpec(kdec.shape, const2),
                  pl.BlockSpec(cdec.shape, const2),
                  pl.BlockSpec(bmask.shape, const2)],
        out_specs=[pl.BlockSpec((c, RET_OUT), lambda b, n: (row(b, n), 0)), cast_out],
        out_shape=[jax.ShapeDtypeStruct((batch * seq, RET_OUT), BF16), cast_shape],
        scratch_shapes=[pltpu.VMEM((qw, RET_OUT), F32)],
        compiler_params=_cparams(("arbitrary", "arbitrary")),
        name="retention",
    )(cast[0], proj, proj, proj, proj, cos, sp, sm, hmask, idec, qdec, kdec, cdec, bmask)


LOG2E = math.log2(math.e)
M_INIT = float(np.finfo(np.float32).min)


def _flash_update(s, v_c, m_ref, l_ref, acc_ref, h):
    reps = s.shape[1] // LANE

    def fold(x, op):
        tiles = [x[:, r * LANE:(r + 1) * LANE] for r in range(reps)]
        while len(tiles) > 1:
            tiles = [op(a, b) for a, b in zip(tiles[::2], tiles[1::2])]
        return tiles[0]

    m_prev = m_ref[h]
    m_new = jnp.maximum(m_prev, jnp.max(fold(s, jnp.maximum), axis=1, keepdims=True))
    p = jnp.exp2(s - jnp.tile(m_new, (1, reps)))
    alpha = jnp.exp2(m_prev - m_new)
    l_ref[h] = alpha * l_ref[h] + jnp.sum(fold(p, jnp.add), axis=1, keepdims=True)
    m_ref[h] = m_new
    acc_ref[h] = acc_ref[h] * alpha + jnp.dot(p.astype(BF16), v_c, preferred_element_type=F32)


INT_MIN = -2 ** 31
IDX_GROUP = 8
ROW_BLOCK = 64
PASSES_PER_CHECK = 4
COUNT_ROWS = 32
KV_PER_LOOP = 4


def _dsa_kernel(*refs, tq, topk, nqi):
    qi_refs = refs[:nqi]
    (w_ref, kidx_ref, q_ref, k_ref, v_ref, o_ref,
     qis_ref, wb_ref, lg_ref, key_ref, bias_ref, qs_ref, m_ref, l_ref, acc_ref) = refs[nqi:]
    tk = tq
    i = pl.program_id(1)
    nchunks = i + 1
    reps = tk // LANE
    ngroups = IDX_HEADS // IDX_GROUP
    group = DSA_HEADS // DSA_KV_HEADS
    d = DSA_HEAD_DIM
    heads_per_ref = IDX_HEADS // nqi

    w = w_ref[...].astype(F32) * (IDX_HEADS ** -0.5)
    for h in range(IDX_HEADS):
        r, t = divmod(h, heads_per_ref)
        qis_ref[h * tq:(h + 1) * tq, :] = qi_refs[r][:, t * LANE:(t + 1) * LANE]
        wb_ref[h] = jnp.broadcast_to(w[:, h:h + 1], (tq, LANE))

    def score_chunk(c, diagonal):
        r0 = pl.multiple_of(c * tk, tk)
        kc = kidx_ref[pl.ds(r0, tk), :]
        for g in range(ngroups):
            lg_ref[g] = lax.dot_general(
                qis_ref[g * IDX_GROUP * tq:(g + 1) * IDX_GROUP * tq, :], kc,
                (((1,), (1,)), ((), ())), preferred_element_type=F32)
        for rb in range(tq // ROW_BLOCK):
            q0 = rb * ROW_BLOCK
            rows = slice(q0, q0 + ROW_BLOCK)
            acc = jnp.zeros((ROW_BLOCK, tk), F32)
            for h in range(IDX_HEADS):
                g, hg = divmod(h, IDX_GROUP)
                lg = lg_ref[g, hg * tq + q0:hg * tq + q0 + ROW_BLOCK, :]
                acc = acc + jnp.maximum(lg, 0.0) * jnp.tile(wb_ref[h, rows, :], (1, reps))
            bits = pltpu.bitcast(acc, jnp.int32)
            key = bits ^ (lax.shift_right_arithmetic(bits, 31) & 0x7FFFFFFF)
            if diagonal:
                kpos = lax.broadcasted_iota(jnp.int32, (ROW_BLOCK, tk), 1)
                qpos = q0 + lax.broadcasted_iota(jnp.int32, (ROW_BLOCK, tk), 0)
                key = jnp.where(kpos <= qpos, key, INT_MIN)
            key_ref[c, rows, :] = key
        bias_ref[c] = pltpu.bitcast(key_ref[c].T, F32)

    def score_body(c, _):
        score_chunk(c, False)
        return 0

    lax.fori_loop(0, i, score_body, 0)
    score_chunk(i, True)

    def count_ge(cand):
        def body(c, cnt):
            ge = (pltpu.bitcast(bias_ref[c], jnp.int32) >= cand).astype(jnp.int32)
            return cnt + jnp.sum(ge.reshape(tk // COUNT_ROWS, COUNT_ROWS, tq), axis=0)
        cnt = lax.fori_loop(0, nchunks, body, jnp.zeros((COUNT_ROWS, tq), jnp.int32))
        return jnp.sum(cnt, axis=0, keepdims=True)

    def bit_cond(state):
        b, _, cnt = state
        return jnp.logical_and(b < 32, jnp.max(jnp.abs(cnt - topk)) > 0)

    def bit_body(state):
        b, t, cnt = state
        for u in range(PASSES_PER_CHECK):
            cand = t ^ lax.shift_left(jnp.int32(1), 31 - (b + u))
            n = count_ge(cand)
            take = n >= topk
            t, cnt = jnp.where(take, cand, t), jnp.where(take, n, cnt)
        return b + PASSES_PER_CHECK, t, cnt

    t0 = jnp.full((1, tq), INT_MIN, jnp.int32)
    c0 = jnp.full((1, tq), nchunks * tk, jnp.int32)
    _, thr_row, _ = lax.while_loop(bit_cond, bit_body, (jnp.int32(0), t0, c0))
    thr_row = jnp.maximum(thr_row, INT_MIN + 1)
    thr = jnp.broadcast_to(thr_row, (LANE, tq)).T
    thr = jnp.tile(thr, (1, reps))

    def bias_body(c, _):
        bias_ref[c] = jnp.where(key_ref[c] >= thr, 0.0, -jnp.inf)
        return 0

    lax.fori_loop(0, nchunks, bias_body, 0)

    @pl.when(nchunks < key_ref.shape[0])
    def _():
        bias_ref[nchunks] = jnp.full((tq, tk), -jnp.inf, F32)

    hpl = KV_PER_LOOP * group
    for g0 in range(0, DSA_KV_HEADS, KV_PER_LOOP):
        for hh in range(hpl):
            h = g0 * group + hh
            qs_ref[hh * tq:(hh + 1) * tq, :] = q_ref[:, h * d:(h + 1) * d]
        m_ref[...] = jnp.full_like(m_ref, M_INIT)
        l_ref[...] = jnp.zeros_like(l_ref)
        acc_ref[...] = jnp.zeros_like(acc_ref)

        def att_body(c2, _, g0=g0):
            r0 = pl.multiple_of(c2 * 2 * tk, 2 * tk)
            bias = jnp.concatenate([bias_ref[2 * c2], bias_ref[2 * c2 + 1]], axis=1)
            for gg in range(KV_PER_LOOP):
                g = g0 + gg
                kc = k_ref[pl.ds(r0, 2 * tk), g * d:(g + 1) * d]
                vc = v_ref[pl.ds(r0, 2 * tk), g * d:(g + 1) * d]
                s_all = lax.dot_general(qs_ref[gg * group * tq:(gg + 1) * group * tq, :], kc,
                                        (((1,), (1,)), ((), ())), preferred_element_type=F32)
                for hh in range(group):
                    _flash_update(s_all[hh * tq:(hh + 1) * tq] + bias, vc, m_ref, l_ref, acc_ref,
                                  gg * group + hh)
            return 0

        lax.fori_loop(0, (nchunks + 1) // 2, att_body, 0)
        for hh in range(hpl):
            h = g0 * group + hh
            o_ref[:, h * d:(h + 1) * d] = (acc_ref[hh] / l_ref[hh]).astype(o_ref.dtype)


def dsa_attention(proj, tail, batch, seq, *, tq=256):
    nq = seq // tq
    assert nq % 2 == 0
    topk = min(DSA_TOPK, seq // 4)
    hpl = KV_PER_LOOP * DSA_HEADS // DSA_KV_HEADS
    kvw = DSA_KV_HEADS * DSA_HEAD_DIM
    iqw = IDX_HEADS * IDX_HEAD_DIM
    nqi = iqw // QI_BLOCK
    col = lambda name, width: _NAT[name][0] // width
    qi_specs = [pl.BlockSpec((tq, QI_BLOCK), lambda b, i, r=r: (b * nq + i, col("iq", QI_BLOCK) + r))
                for r in range(nqi)]
    return pl.pallas_call(
        functools.partial(_dsa_kernel, tq=tq, topk=topk, nqi=nqi),
        grid=(batch, nq),
        in_specs=qi_specs + [
            pl.BlockSpec((tq, LANE), lambda b, i: (b * nq + i, _TAIL["iw"][0] // LANE)),
            pl.BlockSpec((seq, IDX_HEAD_DIM), lambda b, i: (b, col("ik", IDX_HEAD_DIM))),
            pl.BlockSpec((tq, DSA_OUT), lambda b, i: (b * nq + i, col("dq", DSA_OUT))),
            pl.BlockSpec((seq, kvw), lambda b, i: (b, col("dk", kvw))),
            pl.BlockSpec((seq, kvw), lambda b, i: (b, col("dv", kvw)))],
        out_specs=pl.BlockSpec((tq, DSA_OUT), lambda b, i: (b * nq + i, 0)),
        out_shape=jax.ShapeDtypeStruct((batch * seq, DSA_OUT), BF16),
        scratch_shapes=[pltpu.VMEM((IDX_HEADS * tq, IDX_HEAD_DIM), BF16),
                        pltpu.VMEM((IDX_HEADS, tq, LANE), F32),
                        pltpu.VMEM((IDX_HEADS // IDX_GROUP, IDX_GROUP * tq, tq), F32),
                        pltpu.VMEM((nq, tq, tq), jnp.int32),
                        pltpu.VMEM((nq, tq, tq), F32),
                        pltpu.VMEM((hpl * tq, DSA_HEAD_DIM), BF16),
                        pltpu.VMEM((hpl, tq, LANE), F32),
                        pltpu.VMEM((hpl, tq, LANE), F32),
                        pltpu.VMEM((hpl, tq, DSA_HEAD_DIM), F32)],
        compiler_params=_cparams(("parallel", "arbitrary")),
        name="dsa_attention",
    )(*([proj] * nqi), tail, proj, proj, proj, proj)


def _mla_kernel(cast_in_ref, q_ref, kn_ref, kpe_ref, v_ref, o_ref, cast_out_ref, m_ref, l_ref,
                acc_ref, *, tq, hb):
    tk = tq
    i = pl.program_id(2)
    cast_out_ref[...] = cast_in_ref[...].astype(BF16)
    m_ref[...] = jnp.full_like(m_ref, M_INIT)
    l_ref[...] = jnp.zeros_like(l_ref)
    acc_ref[...] = jnp.zeros_like(acc_ref)
    causal = (lax.broadcasted_iota(jnp.int32, (tq, tk), 1)
              <= lax.broadcasted_iota(jnp.int32, (tq, tk), 0))

    def step(c, masked):
        r0 = pl.multiple_of(c * tk, tk)
        kpe = kpe_ref[pl.ds(r0, tk), :]
        for h in range(hb):
            cols = slice(h * MLA_NOPE_DIM, (h + 1) * MLA_NOPE_DIM)
            kf = jnp.concatenate([kn_ref[pl.ds(r0, tk), cols], kpe], axis=1)
            s = lax.dot_general(q_ref[:, h * MLA_QK_PAD:(h + 1) * MLA_QK_PAD], kf,
                                (((1,), (1,)), ((), ())), preferred_element_type=F32)
            if masked:
                s = jnp.where(causal, s, -jnp.inf)
            _flash_update(s, v_ref[pl.ds(r0, tk), cols], m_ref, l_ref, acc_ref, h)

    def body(c, _):
        step(c, False)
        return 0

    lax.fori_loop(0, i, body, 0)
    step(i, True)
    for h in range(hb):
        o_ref[:, h * MLA_V_DIM:(h + 1) * MLA_V_DIM] = (acc_ref[h] / l_ref[h]).astype(o_ref.dtype)


def mla_attention(q, kv, tail, batch, seq, cast, *, tq=512, hb=4):
    nq = seq // tq
    ng = MLA_HEADS // hb
    cast_in, cast_out, cast_shape = _cast_specs(cast, (batch, ng, nq))
    return pl.pallas_call(
        functools.partial(_mla_kernel, tq=tq, hb=hb),
        grid=(batch, ng, nq),
        in_specs=[cast_in,
                  pl.BlockSpec((tq, hb * MLA_QK_PAD), lambda b, g, i: (b * nq + i, g)),
                  pl.BlockSpec((seq, hb * MLA_NOPE_DIM), lambda b, g, i: (b, g)),
                  pl.BlockSpec((seq, LANE), lambda b, g, i: (b, _TAIL["kr"][0] // LANE)),
                  pl.BlockSpec((seq, hb * MLA_V_DIM), lambda b, g, i: (b, ng + g))],
        out_specs=[pl.BlockSpec((tq, hb * MLA_V_DIM), lambda b, g, i: (b * nq + i, g)), cast_out],
        out_shape=[jax.ShapeDtypeStruct((batch * seq, MLA_OUT), BF16), cast_shape],
        scratch_shapes=[pltpu.VMEM((hb, tq, LANE), F32), pltpu.VMEM((hb, tq, LANE), F32),
                        pltpu.VMEM((hb, tq, MLA_V_DIM), F32)],
        compiler_params=_cparams(("arbitrary", "arbitrary", "arbitrary")),
        name="mla_attention",
    )(cast[0], q, kv, tail, kv)


def _tail_w_in(w):
    parts, pos = [], 0
    for name in sorted(_TAIL, key=lambda s: _TAIL[s][0]):
        off, width = _TAIL[name]
        assert off == pos
        a = w[..., _NAT[name][0]:_NAT[name][0] + _NAT[name][1]]
        parts.append(a)
        if a.shape[-1] < width:
            parts.append(jnp.zeros((*w.shape[:-1], width - a.shape[-1]), w.dtype))
        pos += width
    assert pos == TAIL_WIDTH
    return jnp.concatenate(parts, axis=-1).astype(BF16)


def _relayout_w_uq(w):
    dep, r = w.shape[:2]
    a = w.reshape(dep, r, MLA_HEADS, MLA_NOPE_DIM + MLA_ROPE_DIM)
    a = jnp.pad(a, ((0, 0), (0, 0), (0, 0), (0, MLA_QK_PAD - a.shape[3])))
    return a.reshape(dep, r, MLA_HEADS * MLA_QK_PAD).astype(BF16)


def _relayout_w_ukv(w):
    dep, r = w.shape[:2]
    a = w.reshape(dep, r, MLA_HEADS, 2, MLA_NOPE_DIM).transpose(0, 1, 3, 2, 4)
    return a.reshape(dep, r, 2 * MLA_HEADS * MLA_NOPE_DIM).astype(BF16)


def _main_tile_modes():
    modes = [(1.0, None)] * (MAIN_WIDTH // LANE)

    def fill(name, mode):
        off, width = _NAT[name]
        for t in range(off // LANE, (off + width) // LANE):
            modes[t] = mode

    fill("dq", (DSA_HEAD_DIM ** -0.5 * LOG2E, 0))
    fill("dk", (1.0, 0))
    fill("iq", (IDX_HEAD_DIM ** -0.5, 0))
    fill("ik", (1.0, 0))
    return modes


def _tail_tile_modes():
    modes = [(1.0, None)] * (TAIL_WIDTH // LANE)
    modes[_TAIL["kr"][0] // LANE] = (1.0, 0)
    return modes


def kernel(x, attn_norm, w_in, mla_q_norm, mla_kv_norm, w_uq, w_ukv, w_o, mlp_norm, w_up,
           w_down, final_norm):
    batch, seq, d_model = x.shape
    depth = w_in.shape[0]
    m = batch * seq

    ret_tables = _retention_tables(seq)
    part_tab = _rope_tables(seq, PARTIAL_ROPE_THETA, PARTIAL_ROT_DIM)
    mla_tab = _rope_tables(seq, MLA_ROPE_THETA, MLA_ROPE_DIM)
    mla_scale = (MLA_NOPE_DIM + MLA_ROPE_DIM) ** -0.5 * LOG2E
    uq_modes = [(mla_scale, None), (mla_scale, 0)] * MLA_HEADS

    w_in_b = w_in.astype(BF16)
    w_tail_b = _tail_w_in(w_in)
    w_uq_b, w_ukv_b = _relayout_w_uq(w_uq), _relayout_w_ukv(w_ukv)

    h = x.reshape(m, d_model)
    for l in range(depth):
        u = rmsnorm(h, attn_norm[l], BF16)
        proj = matmul(u, w_in_b, BF16, layer=l, name="mm_in", n_out=MAIN_WIDTH, bn=MAIN_BN,
                      tile_modes=_main_tile_modes(), rope_tabs=(part_tab,), seq=seq)
        tail = matmul(u, w_tail_b, BF16, layer=l, name="mm_in_tail", bn=TAIL_WIDTH,
                      tile_modes=_tail_tile_modes(), rope_tabs=(mla_tab,), seq=seq)

        ret, w_o_l = retention(proj, batch, seq, ret_tables, cast=(w_o, l))
        dsa = dsa_attention(proj, tail, batch, seq)

        q_full = matmul(tail, w_uq_b, BF16, layer=l, name="mm_uq", x_cols=_TAIL["cq"],
                        prenorm=mla_q_norm[l], tile_modes=uq_modes, rope_tabs=(mla_tab,), seq=seq)
        kv_up = matmul(tail, w_ukv_b, BF16, layer=l, name="mm_ukv", x_cols=_TAIL["ckv"],
                       prenorm=mla_kv_norm[l])
        mla, w_up_l = mla_attention(q_full, kv_up, tail, batch, seq, cast=(w_up, l))

        h = matmul((ret, dsa, mla), w_o_l, F32, res=h, name="mm_o")
        u = rmsnorm(h, mlp_norm[l], BF16)
        up, w_down_l = matmul(u, w_up_l, BF16, act="relu2", name="mm_up", cast=(w_down, l))
        h = matmul(up, w_down_l, F32, res=h, bn=512, name="mm_down")
    out = rmsnorm(h, final_norm, x.dtype)
    return out.reshape(batch, seq, d_model)
```

```python
import functools
import math

import numpy as np
import jax
import jax.numpy as jnp
from jax import lax
from jax.experimental import pallas as pl
from jax.experimental.pallas import tpu as pltpu

F32 = jnp.float32
BF16 = jnp.bfloat16

RET_HEADS = 8
RET_QK_DIM = 64
RET_V_DIM = 128
RET_CHUNK = 128
RET_ROPE_THETA = 10000.0
DSA_HEADS = 12
DSA_KV_HEADS = 4
DSA_HEAD_DIM = 128
IDX_HEADS = 32
IDX_HEAD_DIM = 128
DSA_TOPK = 256
PARTIAL_ROT_DIM = 32
PARTIAL_ROPE_THETA = 500000.0
MLA_HEADS = 12
MLA_Q_LORA = 768
MLA_KV_LORA = 256
MLA_NOPE_DIM = 128
MLA_ROPE_DIM = 64
MLA_V_DIM = 128
MLA_ROPE_THETA = 10000.0
NORM_EPS = 1e-6

LANE = 128
VMEM_LIMIT = 60 * 1024 * 1024

RET_OUT = RET_HEADS * RET_V_DIM
DSA_OUT = DSA_HEADS * DSA_HEAD_DIM
MLA_OUT = MLA_HEADS * MLA_V_DIM
MLA_QK_PAD = 256

_NAT = {}
_off = 0
for _name, _w in (("rq", 512), ("rk", 512), ("rv", 1024), ("rg", 1024), ("dq", 1536),
                  ("dk", 512), ("dv", 512), ("iq", 4096), ("ik", 128), ("iw", 32),
                  ("cq", 768), ("ckv", 256), ("kr", 64)):
    _NAT[_name] = (_off, _w)
    _off += _w
IN_WIDTH = _off
MAIN_WIDTH = 10240
MAIN_BN = 1024
_TAIL = {"cq": (0, 768), "ckv": (768, 256), "iw": (1024, 128), "kr": (1152, 128)}
TAIL_WIDTH = 1280
QI_BLOCK = 512


def _cparams(sem):
    return pltpu.CompilerParams(dimension_semantics=sem, vmem_limit_bytes=VMEM_LIMIT)


def _rmsnorm_kernel(x_ref, g_ref, *rest):
    o_ref = rest[-1] if len(rest) == 1 else rest[1]
    if len(rest) == 3:
        rest[2][...] = rest[0][...].astype(BF16)
    x = x_ref[...].astype(F32)
    ms = jnp.mean(x * x, axis=-1, keepdims=True)
    o_ref[...] = (x * lax.rsqrt(ms + NORM_EPS) * g_ref[...]).astype(o_ref.dtype)


def rmsnorm(x, g, out_dtype, *, bm=256, cast=None):
    m, width = x.shape
    assert m % bm == 0
    in_specs = [pl.BlockSpec((bm, width), lambda i: (i, 0)),
                pl.BlockSpec((1, width), lambda i: (0, 0))]
    out_specs = pl.BlockSpec((bm, width), lambda i: (i, 0))
    out_shape = jax.ShapeDtypeStruct((m, width), out_dtype)
    args = [x, g.reshape(1, width).astype(F32)]
    if cast is not None:
        cast_in, cast_out, cast_shape = _cast_specs(cast, (m // bm,))
        in_specs.append(cast_in)
        args.append(cast[0])
        out_specs, out_shape = [out_specs, cast_out], [out_shape, cast_shape]
    return pl.pallas_call(
        _rmsnorm_kernel,
        grid=(m // bm,),
        in_specs=in_specs,
        out_specs=out_specs,
        out_shape=out_shape,
        compiler_params=_cparams(("arbitrary",)),
        name="rmsnorm",
    )(*args)


def _mm_kernel(*refs, nx, act, has_res, has_norm, nk, groups, halves, ntab, has_cast):
    x_refs, w_ref = refs[:nx], refs[nx]
    pos = nx + 1
    r_ref = refs[pos] if has_res else None
    pos += int(has_res)
    g_ref = refs[pos] if has_norm else None
    pos += int(has_norm)
    tab_refs = refs[pos:pos + 3 * ntab]
    pos += 3 * ntab
    cast_in_ref = refs[pos] if has_cast else None
    pos += int(has_cast)
    o_ref = refs[pos]
    pos += 1
    if has_cast:
        refs[pos][...] = cast_in_ref[...].astype(BF16)
        pos += 1
    scratch = refs[pos:]

    def store_tiles(acc, pattern):
        for t, (scale, ridx) in enumerate(pattern):
            x = acc[:, t * LANE:(t + 1) * LANE]
            if ridx is not None:
                a, bp, bm = (tab_refs[3 * ridx + k][...] for k in range(3))
                half = halves[ridx]
                x = x * a + pltpu.roll(x, half, 1) * bp + pltpu.roll(x, LANE - half, 1) * bm
            if scale != 1.0:
                x = x * scale
            o_ref[:, t * LANE:(t + 1) * LANE] = x.astype(o_ref.dtype)

    def epilogue(acc):
        if act == "relu2":
            acc = jnp.square(jnp.maximum(acc, 0.0))
        if has_res:
            acc = acc + r_ref[...]
        if groups is None:
            o_ref[...] = acc.astype(o_ref.dtype)
        elif len(groups) == 1:
            store_tiles(acc, groups[0][0])
        else:
            j = pl.program_id(1)
            for pattern, js in groups:
                cond = functools.reduce(jnp.logical_or, [j == jj for jj in js])
                pl.when(cond)(functools.partial(store_tiles, acc, pattern))

    part, k0 = None, 0
    for x_ref in x_refs:
        kw = x_ref.shape[1]
        x = x_ref[...]
        if g_ref is not None:
            xf = x.astype(F32)
            ms = jnp.mean(xf * xf, axis=-1, keepdims=True)
            x = (xf * lax.rsqrt(ms + NORM_EPS) * g_ref[...]).astype(BF16)
        d = jnp.dot(x, w_ref[k0:k0 + kw, :], preferred_element_type=F32)
        part = d if part is None else part + d
        k0 += kw
    if nk == 1:
        epilogue(part)
    else:
        acc_ref = scratch[0]
        k = pl.program_id(2)

        @pl.when(k == 0)
        def _():
            acc_ref[...] = part

        @pl.when(jnp.logical_and(k > 0, k < nk - 1))
        def _():
            acc_ref[...] += part

        @pl.when(k == nk - 1)
        def _():
            epilogue(acc_ref[...] + part)


def matmul(x, w, out_dtype, *, name, layer=None, act=None, res=None, bm=1024, bn=1024, bk=4096,
           n_out=None, tile_modes=None, rope_tabs=(), seq=None, cast=None, x_cols=None,
           prenorm=None):
    xs = x if isinstance(x, (tuple, list)) else (x,)
    m = xs[0].shape[0]
    kdim = sum(a.shape[1] for a in xs) if x_cols is None else x_cols[1]
    assert kdim == w.shape[-2]
    n = w.shape[-1] if n_out is None else n_out
    bm, bn, bk = min(bm, m), min(bn, n), min(bk, kdim)
    assert m % bm == 0 and n % bn == 0 and kdim % bk == 0
    nk = kdim // bk
    assert len(xs) == 1 or nk == 1
    if x_cols is not None:
        assert len(xs) == 1 and nk == 1 and x_cols[0] % kdim == 0
        xcb = x_cols[0] // kdim
        in_specs = [pl.BlockSpec((bm, kdim), lambda i, j, k: (i, xcb))]
    elif len(xs) == 1:
        in_specs = [pl.BlockSpec((bm, bk), lambda i, j, k: (i, k))]
    else:
        in_specs = [pl.BlockSpec((bm, a.shape[1]), lambda i, j, k: (i, 0)) for a in xs]
    if layer is None:
        in_specs.append(pl.BlockSpec((bk, bn), lambda i, j, k: (k, j)))
    else:
        in_specs.append(pl.BlockSpec((None, bk, bn), lambda i, j, k: (layer, k, j)))
    args = [*xs, w]
    if res is not None:
        in_specs.append(pl.BlockSpec((bm, bn), lambda i, j, k: (i, j)))
        args.append(res)
    if prenorm is not None:
        assert nk == 1 and len(xs) == 1
        in_specs.append(pl.BlockSpec((1, kdim), lambda i, j, k: (0, 0)))
        args.append(prenorm.reshape(1, kdim).astype(F32))
    groups = None
    if tile_modes is not None:
        assert len(tile_modes) == n // LANE and seq % bm == 0
        per_block = bn // LANE
        by_pattern = {}
        for j in range(n // bn):
            by_pattern.setdefault(tuple(tile_modes[j * per_block:(j + 1) * per_block]), []).append(j)
        groups = tuple((p, tuple(js)) for p, js in by_pattern.items())
        nseq = seq // bm
        for a, bp, bmt, _ in rope_tabs:
            in_specs += [pl.BlockSpec((bm, LANE), lambda i, j, k: (i % nseq, 0))] * 3
            args += [a, bp, bmt]
    out_specs = pl.BlockSpec((bm, bn), lambda i, j, k: (i, j))
    out_shape = jax.ShapeDtypeStruct((m, n), out_dtype)
    sem = ("parallel", "parallel", "arbitrary")
    if cast is not None:
        spec_in, spec_out, shape_out = _cast_specs(cast, (m // bm, n // bn, nk))
        in_specs.append(spec_in)
        args.append(cast[0])
        out_specs, out_shape = [out_specs, spec_out], [out_shape, shape_out]
        sem = ("arbitrary",) * 3
    scratch = [pltpu.VMEM((bm, bn), F32)] if nk > 1 else []
    return pl.pallas_call(
        functools.partial(_mm_kernel, nx=len(xs), act=act, has_res=res is not None,
                          has_norm=prenorm is not None, nk=nk,
                          groups=groups, halves=tuple(t[3] for t in rope_tabs),
                          ntab=len(rope_tabs), has_cast=cast is not None),
        grid=(m // bm, n // bn, nk),
        in_specs=in_specs,
        out_specs=out_specs,
        out_shape=out_shape,
        scratch_shapes=scratch,
        compiler_params=_cparams(sem),
        name=name,
    )(*args)


def _cast_specs(cast, grid):
    src, layer = cast
    _, rows, cols = src.shape
    nsteps = math.prod(grid)
    rb = 16 * max(1, -(-rows // (16 * nsteps)))
    while rows % rb:
        rb += 16
    nb = rows // rb

    def step(*ids):
        s = 0
        for g, i in zip(grid, ids):
            s = s * g + i
        return jnp.minimum(s, nb - 1)

    spec_in = pl.BlockSpec((None, rb, cols), lambda *ids: (layer, step(*ids), 0))
    spec_out = pl.BlockSpec((rb, cols), lambda *ids: (step(*ids), 0))
    return spec_in, spec_out, jax.ShapeDtypeStruct((rows, cols), BF16)


def _rope_tables(seq, theta, rot_dim):
    half = rot_dim // 2
    inv = jnp.exp(-math.log(theta) * jnp.arange(half, dtype=F32) * (2.0 / rot_dim))
    ang = jnp.arange(seq, dtype=F32)[:, None] * inv[None, :]
    cos, sin = jnp.cos(ang), jnp.sin(ang)
    pad = LANE - rot_dim
    a = jnp.concatenate([cos, cos, jnp.ones((seq, pad), F32)], axis=1)
    bp = jnp.concatenate([jnp.zeros((seq, half), F32), sin, jnp.zeros((seq, pad), F32)], axis=1)
    bm = jnp.concatenate([-sin, jnp.zeros((seq, half + pad), F32)], axis=1)
    return a, bp, bm, half


def _retention_kernel(cast_in_ref, q_ref, k_ref, v_ref, g_ref, cos_ref, sp_ref, sm_ref, hmask_ref,
                      idec_ref, qdec_ref, kdec_ref, cdec_ref, bmask_ref, o_ref, cast_out_ref,
                      state_ref):
    half = RET_QK_DIM // 2
    cast_out_ref[...] = cast_in_ref[...].astype(BF16)

    @pl.when(pl.program_id(1) == 0)
    def _():
        state_ref[...] = jnp.zeros_like(state_ref)

    cos, sp, sm = cos_ref[...], sp_ref[...], sm_ref[...]

    def rot(x):
        tiles = []
        for t in range(x.shape[1] // LANE):
            xt = x[:, t * LANE:(t + 1) * LANE]
            tiles.append(xt * cos + pltpu.roll(xt, half, 1) * sp
                         + pltpu.roll(xt, LANE - half, 1) * sm)
        return jnp.concatenate(tiles, axis=1)

    q = rot(q_ref[...].astype(F32))
    k = rot(k_ref[...].astype(F32)) * (RET_QK_DIM ** -0.5)
    v = v_ref[...].astype(F32)
    kt = k.T.astype(BF16)
    state = state_ref[...]
    cross = jnp.dot(q.astype(BF16), state.astype(BF16), preferred_element_type=F32) * qdec_ref[...]
    for h in range(RET_HEADS):
        cols = slice(h * RET_V_DIM, (h + 1) * RET_V_DIM)
        qm = (q * hmask_ref[h:h + 1, :]).astype(BF16)
        scores = jnp.dot(qm, kt, preferred_element_type=F32) * idec_ref[h]
        inner = jnp.dot(scores.astype(BF16), v[:, cols].astype(BF16), preferred_element_type=F32)
        o = inner + cross[:, cols]
        o = o * lax.rsqrt(jnp.mean(o * o, axis=-1, keepdims=True) + NORM_EPS)
        gate = g_ref[:, cols].astype(F32)
        o_ref[:, cols] = (gate * jax.nn.sigmoid(gate) * o).astype(o_ref.dtype)
    vd = (v * kdec_ref[...]).astype(BF16)
    upd = jnp.dot(kt, vd, preferred_element_type=F32)
    state_ref[...] = state * cdec_ref[...] + upd * bmask_ref[...]


def _retention_tables(seq):
    h, c, dv, dk = RET_HEADS, RET_CHUNK, RET_V_DIM, RET_QK_DIM
    half = dk // 2
    inv = jnp.exp(-math.log(RET_ROPE_THETA) * jnp.arange(half, dtype=F32) * (2.0 / dk))
    ang = jnp.arange(seq, dtype=F32)[:, None] * inv[None, :]
    cos1, sin1, zero = jnp.cos(ang), jnp.sin(ang), jnp.zeros((seq, half), F32)
    reps = LANE // dk
    cos = jnp.tile(jnp.concatenate([cos1, cos1], axis=1), (1, reps))
    sp = jnp.tile(jnp.concatenate([zero, sin1], axis=1), (1, reps))
    sm = jnp.tile(jnp.concatenate([-sin1, zero], axis=1), (1, reps))
    log_gamma = jnp.log1p(-jnp.exp2(-5.0 - jnp.arange(h, dtype=F32)))
    idx = jnp.arange(c, dtype=F32)
    rel = idx[:, None] - idx[None, :]
    idec = jnp.where(rel[None] >= 0,
                     jnp.exp(jnp.maximum(rel, 0.0)[None] * log_gamma[:, None, None]), 0.0)
    qdec = jnp.exp((idx + 1.0)[:, None] * log_gamma[None, :])
    kdec = jnp.exp((c - 1.0 - idx)[:, None] * log_gamma[None, :])
    cdec = jnp.exp(c * log_gamma)[None, :]
    rep = lambda a: jnp.repeat(a, dv, axis=1)
    lane_head = np.arange(h * dk) // dk
    hmask = jnp.asarray(lane_head[None, :] == np.arange(h)[:, None], F32)
    bmask = jnp.asarray(lane_head[:, None] == (np.arange(h * dv) // dv)[None, :], F32)
    return cos, sp, sm, hmask, idec, rep(qdec), rep(kdec), rep(cdec), bmask


def retention(proj, batch, seq, tables, cast):
    c = RET_CHUNK
    nch = seq // c
    qw = RET_HEADS * RET_QK_DIM
    col = lambda name: _NAT[name][0] // _NAT[name][1]
    row = lambda b, n: b * nch + n
    const2 = lambda b, n: (0, 0)
    cos, sp, sm, hmask, idec, qdec, kdec, cdec, bmask = tables
    tab = pl.BlockSpec((c, LANE), lambda b, n: (n, 0))
    cast_in, cast_out, cast_shape = _cast_specs(cast, (batch, nch))
    return pl.pallas_call(
        _retention_kernel,
        grid=(batch, nch),
        in_specs=[cast_in,
                  pl.BlockSpec((c, qw), lambda b, n: (row(b, n), col("rq"))),
                  pl.BlockSpec((c, qw), lambda b, n: (row(b, n), col("rk"))),
                  pl.BlockSpec((c, RET_OUT), lambda b, n: (row(b, n), col("rv"))),
                  pl.BlockSpec((c, RET_OUT), lambda b, n: (row(b, n), col("rg"))),
                  tab, tab, tab,
                  pl.BlockSpec(hmask.shape, const2),
                  pl.BlockSpec(idec.shape, lambda b, n: (0, 0, 0)),
                  pl.BlockSpec(qdec.shape, const2),
                  pl.BlockSpec(kdec.shape, const2),
                  pl.BlockSpec(cdec.shape, const2),
                  pl.BlockSpec(bmask.shape, const2)],
        out_specs=[pl.BlockSpec((c, RET_OUT), lambda b, n: (row(b, n), 0)), cast_out],
        out_shape=[jax.ShapeDtypeStruct((batch * seq, RET_OUT), BF16), cast_shape],
        scratch_shapes=[pltpu.VMEM((qw, RET_OUT), F32)],
        compiler_params=_cparams(("arbitrary", "arbitrary")),
        name="retention",
    )(cast[0], proj, proj, proj, proj, cos, sp, sm, hmask, idec, qdec, kdec, cdec, bmask)


LOG2E = math.log2(math.e)
M_INIT = float(np.finfo(np.float32).min)


def _flash_update(s, v_c, m_ref, l_ref, acc_ref, h):
    reps = s.shape[1] // LANE

    def fold(x, op):
        tiles = [x[:, r * LANE:(r + 1) * LANE] for r in range(reps)]
        while len(tiles) > 1:
            tiles = [op(a, b) for a, b in zip(tiles[::2], tiles[1::2])]
        return tiles[0]

    m_prev = m_ref[h]
    m_new = jnp.maximum(m_prev, jnp.max(fold(s, jnp.maximum), axis=1, keepdims=True))
    p = jnp.exp2(s - jnp.tile(m_new, (1, reps)))
    alpha = jnp.exp2(m_prev - m_new)
    l_ref[h] = alpha * l_ref[h] + jnp.sum(fold(p, jnp.add), axis=1, keepdims=True)
    m_ref[h] = m_new
    acc_ref[h] = acc_ref[h] * alpha + jnp.dot(p.astype(BF16), v_c, preferred_element_type=F32)


INT_MIN = -2 ** 31
IDX_GROUP = 8
ROW_BLOCK = 64
PASSES_PER_CHECK = 4
COUNT_ROWS = 32
KV_PER_LOOP = 4


def _dsa_kernel(*refs, tq, topk, nqi):
    qi_refs = refs[:nqi]
    (w_ref, kidx_ref, q_ref, k_ref, v_ref, o_ref,
     qis_ref, wb_ref, lg_ref, key_ref, bias_ref, qs_ref, m_ref, l_ref, acc_ref) = refs[nqi:]
    tk = tq
    i = pl.program_id(1)
    nchunks = i + 1
    reps = tk // LANE
    ngroups = IDX_HEADS // IDX_GROUP
    group = DSA_HEADS // DSA_KV_HEADS
    d = DSA_HEAD_DIM
    heads_per_ref = IDX_HEADS // nqi

    w = w_ref[...].astype(F32) * (IDX_HEADS ** -0.5)
    for h in range(IDX_HEADS):
        r, t = divmod(h, heads_per_ref)
        qis_ref[h * tq:(h + 1) * tq, :] = qi_refs[r][:, t * LANE:(t + 1) * LANE]
        wb_ref[h] = jnp.broadcast_to(w[:, h:h + 1], (tq, LANE))

    def score_chunk(c, diagonal):
        r0 = pl.multiple_of(c * tk, tk)
        kc = kidx_ref[pl.ds(r0, tk), :]
        for g in range(ngroups):
            lg_ref[g] = lax.dot_general(
                qis_ref[g * IDX_GROUP * tq:(g + 1) * IDX_GROUP * tq, :], kc,
                (((1,), (1,)), ((), ())), preferred_element_type=F32)
        for rb in range(tq // ROW_BLOCK):
            q0 = rb * ROW_BLOCK
            rows = slice(q0, q0 + ROW_BLOCK)
            acc = jnp.zeros((ROW_BLOCK, tk), F32)
            for h in range(IDX_HEADS):
                g, hg = divmod(h, IDX_GROUP)
                lg = lg_ref[g, hg * tq + q0:hg * tq + q0 + ROW_BLOCK, :]
                acc = acc + jnp.maximum(lg, 0.0) * jnp.tile(wb_ref[h, rows, :], (1, reps))
            bits = pltpu.bitcast(acc, jnp.int32)
            key = bits ^ (lax.shift_right_arithmetic(bits, 31) & 0x7FFFFFFF)
            if diagonal:
                kpos = lax.broadcasted_iota(jnp.int32, (ROW_BLOCK, tk), 1)
                qpos = q0 + lax.broadcasted_iota(jnp.int32, (ROW_BLOCK, tk), 0)
                key = jnp.where(kpos <= qpos, key, INT_MIN)
            key_ref[c, rows, :] = key
        bias_ref[c] = pltpu.bitcast(key_ref[c].T, F32)

    def score_body(c, _):
        score_chunk(c, False)
        return 0

    lax.fori_loop(0, i, score_body, 0)
    score_chunk(i, True)

    def count_ge(cand):
        def body(c, cnt):
            ge = (pltpu.bitcast(bias_ref[c], jnp.int32) >= cand).astype(jnp.int32)
            return cnt + jnp.sum(ge.reshape(tk // COUNT_ROWS, COUNT_ROWS, tq), axis=0)
        cnt = lax.fori_loop(0, nchunks, body, jnp.zeros((COUNT_ROWS, tq), jnp.int32))
        return jnp.sum(cnt, axis=0, keepdims=True)

    def bit_cond(state):
        b, _, cnt = state
        return jnp.logical_and(b < 32, jnp.max(jnp.abs(cnt - topk)) > 0)

    def bit_body(state):
        b, t, cnt = state
        for u in range(PASSES_PER_CHECK):
            cand = t ^ lax.shift_left(jnp.int32(1), 31 - (b + u))
            n = count_ge(cand)
            take = n >= topk
            t, cnt = jnp.where(take, cand, t), jnp.where(take, n, cnt)
        return b + PASSES_PER_CHECK, t, cnt

    t0 = jnp.full((1, tq), INT_MIN, jnp.int32)
    c0 = jnp.full((1, tq), nchunks * tk, jnp.int32)
    _, thr_row, _ = lax.while_loop(bit_cond, bit_body, (jnp.int32(0), t0, c0))
    thr_row = jnp.maximum(thr_row, INT_MIN + 1)
    thr = jnp.broadcast_to(thr_row, (LANE, tq)).T
    thr = jnp.tile(thr, (1, reps))

    def bias_body(c, _):
        bias_ref[c] = jnp.where(key_ref[c] >= thr, 0.0, -jnp.inf)
        return 0

    lax.fori_loop(0, nchunks, bias_body, 0)

    @pl.when(nchunks < key_ref.shape[0])
    def _():
        bias_ref[nchunks] = jnp.full((tq, tk), -jnp.inf, F32)

    hpl = KV_PER_LOOP * group
    for g0 in range(0, DSA_KV_HEADS, KV_PER_LOOP):
        for hh in range(hpl):
            h = g0 * group + hh
            qs_ref[hh * tq:(hh + 1) * tq, :] = q_ref[:, h * d:(h + 1) * d]
        m_ref[...] = jnp.full_like(m_ref, M_INIT)
        l_ref[...] = jnp.zeros_like(l_ref)
        acc_ref[...] = jnp.zeros_like(acc_ref)

        def att_body(c2, _, g0=g0):
            r0 = pl.multiple_of(c2 * 2 * tk, 2 * tk)
            bias = jnp.concatenate([bias_ref[2 * c2], bias_ref[2 * c2 + 1]], axis=1)
            for gg in range(KV_PER_LOOP):
                g = g0 + gg
                kc = k_ref[pl.ds(r0, 2 * tk), g * d:(g + 1) * d]
                vc = v_ref[pl.ds(r0, 2 * tk), g * d:(g + 1) * d]
                s_all = lax.dot_general(qs_ref[gg * group * tq:(gg + 1) * group * tq, :], kc,
                                        (((1,), (1,)), ((), ())), preferred_element_type=F32)
                for hh in range(group):
                    _flash_update(s_all[hh * tq:(hh + 1) * tq] + bias, vc, m_ref, l_ref, acc_ref,
                                  gg * group + hh)
            return 0

        lax.fori_loop(0, (nchunks + 1) // 2, att_body, 0)
        for hh in range(hpl):
            h = g0 * group + hh
            o_ref[:, h * d:(h + 1) * d] = (acc_ref[hh] / l_ref[hh]).astype(o_ref.dtype)


def dsa_attention(proj, tail, batch, seq, *, tq=256):
    nq = seq // tq
    assert nq % 2 == 0
    topk = min(DSA_TOPK, seq // 4)
    hpl = KV_PER_LOOP * DSA_HEADS // DSA_KV_HEADS
    kvw = DSA_KV_HEADS * DSA_HEAD_DIM
    iqw = IDX_HEADS * IDX_HEAD_DIM
    nqi = iqw // QI_BLOCK
    col = lambda name, width: _NAT[name][0] // width
    qi_specs = [pl.BlockSpec((tq, QI_BLOCK), lambda b, i, r=r: (b * nq + i, col("iq", QI_BLOCK) + r))
                for r in range(nqi)]
    return pl.pallas_call(
        functools.partial(_dsa_kernel, tq=tq, topk=topk, nqi=nqi),
        grid=(batch, nq),
        in_specs=qi_specs + [
            pl.BlockSpec((tq, LANE), lambda b, i: (b * nq + i, _TAIL["iw"][0] // LANE)),
            pl.BlockSpec((seq, IDX_HEAD_DIM), lambda b, i: (b, col("ik", IDX_HEAD_DIM))),
            pl.BlockSpec((tq, DSA_OUT), lambda b, i: (b * nq + i, col("dq", DSA_OUT))),
            pl.BlockSpec((seq, kvw), lambda b, i: (b, col("dk", kvw))),
            pl.BlockSpec((seq, kvw), lambda b, i: (b, col("dv", kvw)))],
        out_specs=pl.BlockSpec((tq, DSA_OUT), lambda b, i: (b * nq + i, 0)),
        out_shape=jax.ShapeDtypeStruct((batch * seq, DSA_OUT), BF16),
        scratch_shapes=[pltpu.VMEM((IDX_HEADS * tq, IDX_HEAD_DIM), BF16),
                        pltpu.VMEM((IDX_HEADS, tq, LANE), F32),
                        pltpu.VMEM((IDX_HEADS // IDX_GROUP, IDX_GROUP * tq, tq), F32),
                        pltpu.VMEM((nq, tq, tq), jnp.int32),
                        pltpu.VMEM((nq, tq, tq), F32),
                        pltpu.VMEM((hpl * tq, DSA_HEAD_DIM), BF16),
                        pltpu.VMEM((hpl, tq, LANE), F32),
                        pltpu.VMEM((hpl, tq, LANE), F32),
                        pltpu.VMEM((hpl, tq, DSA_HEAD_DIM), F32)],
        compiler_params=_cparams(("parallel", "arbitrary")),
        name="dsa_attention",
    )(*([proj] * nqi), tail, proj, proj, proj, proj)


def _mla_kernel(cast_in_ref, q_ref, kn_ref, kpe_ref, v_ref, o_ref, cast_out_ref, m_ref, l_ref,
                acc_ref, *, tq, hb):
    tk = tq
    i = pl.program_id(2)
    cast_out_ref[...] = cast_in_ref[...].astype(BF16)
    m_ref[...] = jnp.full_like(m_ref, M_INIT)
    l_ref[...] = jnp.zeros_like(l_ref)
    acc_ref[...] = jnp.zeros_like(acc_ref)
    causal = (lax.broadcasted_iota(jnp.int32, (tq, tk), 1)
              <= lax.broadcasted_iota(jnp.int32, (tq, tk), 0))

    def step(c, masked):
        r0 = pl.multiple_of(c * tk, tk)
        kpe = kpe_ref[pl.ds(r0, tk), :]
        for h in range(hb):
            cols = slice(h * MLA_NOPE_DIM, (h + 1) * MLA_NOPE_DIM)
            kf = jnp.concatenate([kn_ref[pl.ds(r0, tk), cols], kpe], axis=1)
            s = lax.dot_general(q_ref[:, h * MLA_QK_PAD:(h + 1) * MLA_QK_PAD], kf,
                                (((1,), (1,)), ((), ())), preferred_element_type=F32)
            if masked:
                s = jnp.where(causal, s, -jnp.inf)
            _flash_update(s, v_ref[pl.ds(r0, tk), cols], m_ref, l_ref, acc_ref, h)

    def body(c, _):
        step(c, False)
        return 0

    lax.fori_loop(0, i, body, 0)
    step(i, True)
    for h in range(hb):
        o_ref[:, h * MLA_V_DIM:(h + 1) * MLA_V_DIM] = (acc_ref[h] / l_ref[h]).astype(o_ref.dtype)


def mla_attention(q, kv, tail, batch, seq, cast, *, tq=512, hb=4):
    nq = seq // tq
    ng = MLA_HEADS // hb
    cast_in, cast_out, cast_shape = _cast_specs(cast, (batch, ng, nq))
    return pl.pallas_call(
        functools.partial(_mla_kernel, tq=tq, hb=hb),
        grid=(batch, ng, nq),
        in_specs=[cast_in,
                  pl.BlockSpec((tq, hb * MLA_QK_PAD), lambda b, g, i: (b * nq + i, g)),
                  pl.BlockSpec((seq, hb * MLA_NOPE_DIM), lambda b, g, i: (b, g)),
                  pl.BlockSpec((seq, LANE), lambda b, g, i: (b, _TAIL["kr"][0] // LANE)),
                  pl.BlockSpec((seq, hb * MLA_V_DIM), lambda b, g, i: (b, ng + g))],
        out_specs=[pl.BlockSpec((tq, hb * MLA_V_DIM), lambda b, g, i: (b * nq + i, g)), cast_out],
        out_shape=[jax.ShapeDtypeStruct((batch * seq, MLA_OUT), BF16), cast_shape],
        scratch_shapes=[pltpu.VMEM((hb, tq, LANE), F32), pltpu.VMEM((hb, tq, LANE), F32),
                        pltpu.VMEM((hb, tq, MLA_V_DIM), F32)],
        compiler_params=_cparams(("arbitrary", "arbitrary", "arbitrary")),
        name="mla_attention",
    )(cast[0], q, kv, tail, kv)


def _tail_w_in(w):
    parts, pos = [], 0
    for name in sorted(_TAIL, key=lambda s: _TAIL[s][0]):
        off, width = _TAIL[name]
        assert off == pos
        a = w[..., _NAT[name][0]:_NAT[name][0] + _NAT[name][1]]
        parts.append(a)
        if a.shape[-1] < width:
            parts.append(jnp.zeros((*w.shape[:-1], width - a.shape[-1]), w.dtype))
        pos += width
    assert pos == TAIL_WIDTH
    return jnp.concatenate(parts, axis=-1).astype(BF16)


def _relayout_w_uq(w):
    dep, r = w.shape[:2]
    a = w.reshape(dep, r, MLA_HEADS, MLA_NOPE_DIM + MLA_ROPE_DIM)
    a = jnp.pad(a, ((0, 0), (0, 0), (0, 0), (0, MLA_QK_PAD - a.shape[3])))
    return a.reshape(dep, r, MLA_HEADS * MLA_QK_PAD).astype(BF16)


def _relayout_w_ukv(w):
    dep, r = w.shape[:2]
    a = w.reshape(dep, r, MLA_HEADS, 2, MLA_NOPE_DIM).transpose(0, 1, 3, 2, 4)
    return a.reshape(dep, r, 2 * MLA_HEADS * MLA_NOPE_DIM).astype(BF16)


def _main_tile_modes():
    modes = [(1.0, None)] * (MAIN_WIDTH // LANE)

    def fill(name, mode):
        off, width = _NAT[name]
        for t in range(off // LANE, (off + width) // LANE):
            modes[t] = mode

    fill("dq", (DSA_HEAD_DIM ** -0.5 * LOG2E, 0))
    fill("dk", (1.0, 0))
    fill("iq", (IDX_HEAD_DIM ** -0.5, 0))
    fill("ik", (1.0, 0))
    return modes


def _tail_tile_modes():
    modes = [(1.0, None)] * (TAIL_WIDTH // LANE)
    modes[_TAIL["kr"][0] // LANE] = (1.0, 0)
    return modes


def kernel(x, attn_norm, w_in, mla_q_norm, mla_kv_norm, w_uq, w_ukv, w_o, mlp_norm, w_up,
           w_down, final_norm):
    batch, seq, d_model = x.shape
    depth = w_in.shape[0]
    m = batch * seq

    ret_tables = _retention_tables(seq)
    part_tab = _rope_tables(seq, PARTIAL_ROPE_THETA, PARTIAL_ROT_DIM)
    mla_tab = _rope_tables(seq, MLA_ROPE_THETA, MLA_ROPE_DIM)
    mla_scale = (MLA_NOPE_DIM + MLA_ROPE_DIM) ** -0.5 * LOG2E
    uq_modes = [(mla_scale, None), (mla_scale, 0)] * MLA_HEADS

    w_uq_b, w_ukv_b = _relayout_w_uq(w_uq), _relayout_w_ukv(w_ukv)

    h = x.reshape(m, d_model)
    u, w_in_l = rmsnorm(h, attn_norm[0], BF16, cast=(w_in, 0))
    for l in range(depth):
        if l > 0:
            u = rmsnorm(h, attn_norm[l], BF16)
        proj = matmul(u, w_in_l, BF16, name="mm_in", n_out=MAIN_WIDTH, bn=MAIN_BN,
                      tile_modes=_main_tile_modes(), rope_tabs=(part_tab,), seq=seq)
        tail = matmul(u, _tail_w_in(w_in_l), BF16, name="mm_in_tail", bn=TAIL_WIDTH,
                      tile_modes=_tail_tile_modes(), rope_tabs=(mla_tab,), seq=seq)

        ret, w_o_l = retention(proj, batch, seq, ret_tables, cast=(w_o, l))
        dsa = dsa_attention(proj, tail, batch, seq)

        q_full = matmul(tail, w_uq_b, BF16, layer=l, name="mm_uq", x_cols=_TAIL["cq"],
                        prenorm=mla_q_norm[l], tile_modes=uq_modes, rope_tabs=(mla_tab,), seq=seq)
        kv_up = matmul(tail, w_ukv_b, BF16, layer=l, name="mm_ukv", x_cols=_TAIL["ckv"],
                       prenorm=mla_kv_norm[l])
        mla, w_up_l = mla_attention(q_full, kv_up, tail, batch, seq, cast=(w_up, l))

        h = matmul((ret, dsa, mla), w_o_l, F32, res=h, name="mm_o")
        u = rmsnorm(h, mlp_norm[l], BF16)
        up, w_down_l = matmul(u, w_up_l, BF16, act="relu2", name="mm_up", cast=(w_down, l))
        if l + 1 < depth:
            h, w_in_l = matmul(up, w_down_l, F32, res=h, bn=512, name="mm_down",
                               cast=(w_in, l + 1))
        else:
            h = matmul(up, w_down_l, F32, res=h, bn=512, name="mm_down")
    out = rmsnorm(h, final_norm, x.dtype)
    return out.reshape(batch, seq, d_model)
```

```python
import functools
import math

import numpy as np
import jax
import jax.numpy as jnp
from jax import lax
from jax.experimental import pallas as pl
from jax.experimental.pallas import tpu as pltpu

F32 = jnp.float32
BF16 = jnp.bfloat16

RET_HEADS = 8
RET_QK_DIM = 64
RET_V_DIM = 128
RET_CHUNK = 128
RET_ROPE_THETA = 10000.0
DSA_HEADS = 12
DSA_KV_HEADS = 4
DSA_HEAD_DIM = 128
IDX_HEADS = 32
IDX_HEAD_DIM = 128
DSA_TOPK = 256
PARTIAL_ROT_DIM = 32
PARTIAL_ROPE_THETA = 500000.0
MLA_HEADS = 12
MLA_Q_LORA = 768
MLA_KV_LORA = 256
MLA_NOPE_DIM = 128
MLA_ROPE_DIM = 64
MLA_V_DIM = 128
MLA_ROPE_THETA = 10000.0
NORM_EPS = 1e-6

LANE = 128
VMEM_LIMIT = 60 * 1024 * 1024

RET_OUT = RET_HEADS * RET_V_DIM
DSA_OUT = DSA_HEADS * DSA_HEAD_DIM
MLA_OUT = MLA_HEADS * MLA_V_DIM
MLA_QK_PAD = 256

_NAT = {}
_off = 0
for _name, _w in (("rq", 512), ("rk", 512), ("rv", 1024), ("rg", 1024), ("dq", 1536),
                  ("dk", 512), ("dv", 512), ("iq", 4096), ("ik", 128), ("iw", 32),
                  ("cq", 768), ("ckv", 256), ("kr", 64)):
    _NAT[_name] = (_off, _w)
    _off += _w
IN_WIDTH = _off
MAIN_WIDTH = 10240
MAIN_BN = 1024
_TAIL = {"cq": (0, 768), "ckv": (768, 256), "iw": (1024, 128), "kr": (1152, 128)}
TAIL_WIDTH = 1280
QI_BLOCK = 512
DOWN_BK = 2048
NORM_ROWS = 32
NORM_ONCE_MIN = 2 ** 21


def _cparams(sem):
    return pltpu.CompilerParams(dimension_semantics=sem, vmem_limit_bytes=VMEM_LIMIT)


def _rmsnorm_kernel(x_ref, g_ref, *rest):
    o_ref = rest[-1] if len(rest) == 1 else rest[1]
    if len(rest) == 3:
        rest[2][...] = rest[0][...].astype(BF16)
    x = x_ref[...].astype(F32)
    ms = jnp.mean(x * x, axis=-1, keepdims=True)
    o_ref[...] = (x * lax.rsqrt(ms + NORM_EPS) * g_ref[...]).astype(o_ref.dtype)


def rmsnorm(x, g, out_dtype, *, bm=256, cast=None):
    m, width = x.shape
    assert m % bm == 0
    in_specs = [pl.BlockSpec((bm, width), lambda i: (i, 0)),
                pl.BlockSpec((1, width), lambda i: (0, 0))]
    out_specs = pl.BlockSpec((bm, width), lambda i: (i, 0))
    out_shape = jax.ShapeDtypeStruct((m, width), out_dtype)
    args = [x, g.reshape(1, width).astype(F32)]
    if cast is not None:
        cast_in, cast_out, cast_shape = _cast_specs(cast, (m // bm,))
        in_specs.append(cast_in)
        args.append(cast[0])
        out_specs, out_shape = [out_specs, cast_out], [out_shape, cast_shape]
    return pl.pallas_call(
        _rmsnorm_kernel,
        grid=(m // bm,),
        in_specs=in_specs,
        out_specs=out_specs,
        out_shape=out_shape,
        compiler_params=_cparams(("arbitrary",)),
        name="rmsnorm",
    )(*args)


def _mm_kernel(*refs, nx, act, has_res, has_norm, norm_once, nk, groups, halves, ntab, has_cast,
               emit_bf16, w_t):
    refs = list(refs)
    take = lambda n=1: [refs.pop(0) for _ in range(n)]
    x_refs = take(nx)
    (w_ref,) = take()
    r_ref = take()[0] if has_res else None
    g_ref = take()[0] if has_norm else None
    tab_refs = take(3 * ntab)
    cast_in_ref = take()[0] if has_cast else None
    (o_ref,) = take()
    ob_ref = take()[0] if emit_bf16 else None
    if has_cast:
        take()[0][...] = cast_in_ref[...].astype(BF16)
    acc_ref = take()[0] if nk > 1 else None
    xn_ref = take()[0] if norm_once else None

    def store_tiles(acc, pattern):
        for t, (scale, ridx) in enumerate(pattern):
            x = acc[:, t * LANE:(t + 1) * LANE]
            if ridx is not None:
                a, bp, bm = (tab_refs[3 * ridx + k][...] for k in range(3))
                half = halves[ridx]
                x = x * a + pltpu.roll(x, half, 1) * bp + pltpu.roll(x, LANE - half, 1) * bm
            if scale != 1.0:
                x = x * scale
            o_ref[:, t * LANE:(t + 1) * LANE] = x.astype(o_ref.dtype)

    def epilogue(acc):
        if act == "relu2":
            acc = jnp.square(jnp.maximum(acc, 0.0))
        if has_res:
            acc = acc + r_ref[...]
        if emit_bf16:
            ob_ref[...] = acc.astype(BF16)
        if groups is None:
            o_ref[...] = acc.astype(o_ref.dtype)
        elif len(groups) == 1:
            store_tiles(acc, groups[0][0])
        else:
            j = pl.program_id(1)
            for pattern, js in groups:
                cond = functools.reduce(jnp.logical_or, [j == jj for jj in js])
                pl.when(cond)(functools.partial(store_tiles, acc, pattern))

    def normalised(x):
        xf = x.astype(F32)
        ms = jnp.mean(xf * xf, axis=-1, keepdims=True)
        return (xf * lax.rsqrt(ms + NORM_EPS) * g_ref[...]).astype(BF16)

    if norm_once:
        @pl.when(pl.program_id(1) == 0)
        def _():
            def rows_body(r, _):
                rows = pl.ds(pl.multiple_of(r * NORM_ROWS, NORM_ROWS), NORM_ROWS)
                xn_ref[rows, :] = normalised(x_refs[0][rows, :])
                return 0
            lax.fori_loop(0, xn_ref.shape[0] // NORM_ROWS, rows_body, 0)

    if nk > 1:
        @pl.when(pl.program_id(2) == 0)
        def _():
            acc_ref[...] = jnp.zeros_like(acc_ref)

    part, k0 = None, 0
    for x_ref in x_refs:
        kw = x_ref.shape[1]
        if norm_once:
            x = xn_ref[...]
        elif has_norm:
            x = normalised(x_ref[...])
        else:
            x = x_ref[...]
        if w_t:
            d = lax.dot_general(x, w_ref[...], (((1,), (1,)), ((), ())),
                                preferred_element_type=F32)
        else:
            d = jnp.dot(x, w_ref[k0:k0 + kw, :], preferred_element_type=F32)
        part = d if part is None else part + d
        k0 += kw
    if nk == 1:
        epilogue(part)
    else:
        acc_ref[...] += part

        @pl.when(pl.program_id(2) == nk - 1)
        def _():
            epilogue(acc_ref[...])


def matmul(x, w, out_dtype, *, name, layer=None, act=None, res=None, bm=1024, bn=1024, bk=4096,
           n_out=None, tile_modes=None, rope_tabs=(), seq=None, cast=None, x_cols=None,
           prenorm=None, w_t=False, emit_bf16=False):
    xs = x if isinstance(x, (tuple, list)) else (x,)
    m = xs[0].shape[0]
    kdim = sum(a.shape[1] for a in xs) if x_cols is None else x_cols[1]
    assert kdim == w.shape[-1 if w_t else -2]
    n = w.shape[-2 if w_t else -1] if n_out is None else n_out
    bm, bn, bk = min(bm, m), min(bn, n), min(bk, kdim)
    assert m % bm == 0 and n % bn == 0 and kdim % bk == 0
    nk = kdim // bk
    assert len(xs) == 1 or nk == 1
    if x_cols is not None:
        assert len(xs) == 1 and nk == 1 and x_cols[0] % kdim == 0
        xcb = x_cols[0] // kdim
        in_specs = [pl.BlockSpec((bm, kdim), lambda i, j, k: (i, xcb))]
    elif len(xs) == 1:
        in_specs = [pl.BlockSpec((bm, bk), lambda i, j, k: (i, k))]
    else:
        in_specs = [pl.BlockSpec((bm, a.shape[1]), lambda i, j, k: (i, 0)) for a in xs]
    if w_t:
        assert layer is None and len(xs) == 1
        in_specs.append(pl.BlockSpec((bn, bk), lambda i, j, k: (j, k)))
    elif layer is None:
        in_specs.append(pl.BlockSpec((bk, bn), lambda i, j, k: (k, j)))
    else:
        in_specs.append(pl.BlockSpec((None, bk, bn), lambda i, j, k: (layer, k, j)))
    args = [*xs, w]
    if res is not None:
        in_specs.append(pl.BlockSpec((bm, bn), lambda i, j, k: (i, j)))
        args.append(res)
    if prenorm is not None:
        assert nk == 1 and len(xs) == 1
        in_specs.append(pl.BlockSpec((1, kdim), lambda i, j, k: (0, 0)))
        args.append(prenorm.reshape(1, kdim).astype(F32))
    groups = None
    if tile_modes is not None:
        assert len(tile_modes) == n // LANE and seq % bm == 0
        per_block = bn // LANE
        by_pattern = {}
        for j in range(n // bn):
            by_pattern.setdefault(tuple(tile_modes[j * per_block:(j + 1) * per_block]), []).append(j)
        groups = tuple((p, tuple(js)) for p, js in by_pattern.items())
        nseq = seq // bm
        for a, bp, bmt, _ in rope_tabs:
            in_specs += [pl.BlockSpec((bm, LANE), lambda i, j, k: (i % nseq, 0))] * 3
            args += [a, bp, bmt]
    tile_spec = pl.BlockSpec((bm, bn), lambda i, j, k: (i, j))
    out_specs = [tile_spec]
    out_shape = [jax.ShapeDtypeStruct((m, n), out_dtype)]
    if emit_bf16:
        out_specs.append(tile_spec)
        out_shape.append(jax.ShapeDtypeStruct((m, n), BF16))
    norm_once = prenorm is not None and n // bn > 1 and bm * kdim >= NORM_ONCE_MIN
    ordered = cast is not None or norm_once
    sem = ("arbitrary",) * 3 if ordered else ("parallel", "parallel", "arbitrary")
    if cast is not None:
        spec_in, spec_out, shape_out = _cast_specs(cast, (m // bm, n // bn, nk))
        in_specs.append(spec_in)
        args.append(cast[0])
        out_specs.append(spec_out)
        out_shape.append(shape_out)
    scratch = [pltpu.VMEM((bm, bn), F32)] if nk > 1 else []
    if norm_once:
        scratch.append(pltpu.VMEM((bm, kdim), BF16))
    outs = pl.pallas_call(
        functools.partial(_mm_kernel, nx=len(xs), act=act, has_res=res is not None,
                          has_norm=prenorm is not None, norm_once=norm_once, nk=nk,
                          groups=groups, halves=tuple(t[3] for t in rope_tabs),
                          ntab=len(rope_tabs), has_cast=cast is not None, emit_bf16=emit_bf16,
                          w_t=w_t),
        grid=(m // bm, n // bn, nk),
        in_specs=in_specs,
        out_specs=out_specs,
        out_shape=out_shape,
        scratch_shapes=scratch,
        compiler_params=_cparams(sem),
        name=name,
    )(*args)
    return outs[0] if len(outs) == 1 else tuple(outs)


def _cast_specs(cast, grid):
    src, layer = cast
    _, rows, cols = src.shape
    nsteps = math.prod(grid)
    rb = 16 * max(1, -(-rows // (16 * nsteps)))
    while rows % rb:
        rb += 16
    nb = rows // rb

    def step(*ids):
        s = 0
        for g, i in zip(grid, ids):
            s = s * g + i
        return jnp.minimum(s, nb - 1)

    spec_in = pl.BlockSpec((None, rb, cols), lambda *ids: (layer, step(*ids), 0))
    spec_out = pl.BlockSpec((rb, cols), lambda *ids: (step(*ids), 0))
    return spec_in, spec_out, jax.ShapeDtypeStruct((rows, cols), BF16)


def _rope_tables(seq, theta, rot_dim):
    half = rot_dim // 2
    inv = jnp.exp(-math.log(theta) * jnp.arange(half, dtype=F32) * (2.0 / rot_dim))
    ang = jnp.arange(seq, dtype=F32)[:, None] * inv[None, :]
    cos, sin = jnp.cos(ang), jnp.sin(ang)
    pad = LANE - rot_dim
    a = jnp.concatenate([cos, cos, jnp.ones((seq, pad), F32)], axis=1)
    bp = jnp.concatenate([jnp.zeros((seq, half), F32), sin, jnp.zeros((seq, pad), F32)], axis=1)
    bm = jnp.concatenate([-sin, jnp.zeros((seq, half + pad), F32)], axis=1)
    return a, bp, bm, half


def _retention_kernel(cast_in_ref, q_ref, k_ref, v_ref, g_ref, cos_ref, sp_ref, sm_ref, hmask_ref,
                      idec_ref, qdec_ref, kdec_ref, cdec_ref, bmask_ref, o_ref, cast_out_ref,
                      state_ref):
    half = RET_QK_DIM // 2
    cast_out_ref[...] = cast_in_ref[...].astype(BF16)

    @pl.when(pl.program_id(1) == 0)
    def _():
        state_ref[...] = jnp.zeros_like(state_ref)

    cos, sp, sm = cos_ref[...], sp_ref[...], sm_ref[...]

    def rot(x):
        tiles = []
        for t in range(x.shape[1] // LANE):
            xt = x[:, t * LANE:(t + 1) * LANE]
            tiles.append(xt * cos + pltpu.roll(xt, half, 1) * sp
                         + pltpu.roll(xt, LANE - half, 1) * sm)
        return jnp.concatenate(tiles, axis=1)

    q = rot(q_ref[...].astype(F32))
    k = rot(k_ref[...].astype(F32)) * (RET_QK_DIM ** -0.5)
    v = v_ref[...].astype(F32)
    kt = k.T.astype(BF16)
    state = state_ref[...]
    cross = jnp.dot(q.astype(BF16), state.astype(BF16), preferred_element_type=F32) * qdec_ref[...]
    for h in range(RET_HEADS):
        cols = slice(h * RET_V_DIM, (h + 1) * RET_V_DIM)
        qm = (q * hmask_ref[h:h + 1, :]).astype(BF16)
        scores = jnp.dot(qm, kt, preferred_element_type=F32) * idec_ref[h]
        inner = jnp.dot(scores.astype(BF16), v[:, cols].astype(BF16), preferred_element_type=F32)
        o = inner + cross[:, cols]
        o = o * lax.rsqrt(jnp.mean(o * o, axis=-1, keepdims=True) + NORM_EPS)
        gate = g_ref[:, cols].astype(F32)
        o_ref[:, cols] = (gate * jax.nn.sigmoid(gate) * o).astype(o_ref.dtype)
    vd = (v * kdec_ref[...]).astype(BF16)
    upd = jnp.dot(kt, vd, preferred_element_type=F32)
    state_ref[...] = state * cdec_ref[...] + upd * bmask_ref[...]


def _retention_tables(seq):
    h, c, dv, dk = RET_HEADS, RET_CHUNK, RET_V_DIM, RET_QK_DIM
    half = dk // 2
    inv = jnp.exp(-math.log(RET_ROPE_THETA) * jnp.arange(half, dtype=F32) * (2.0 / dk))
    ang = jnp.arange(seq, dtype=F32)[:, None] * inv[None, :]
    cos1, sin1, zero = jnp.cos(ang), jnp.sin(ang), jnp.zeros((seq, half), F32)
    reps = LANE // dk
    cos = jnp.tile(jnp.concatenate([cos1, cos1], axis=1), (1, reps))
    sp = jnp.tile(jnp.concatenate([zero, sin1], axis=1), (1, reps))
    sm = jnp.tile(jnp.concatenate([-sin1, zero], axis=1), (1, reps))
    log_gamma = jnp.log1p(-jnp.exp2(-5.0 - jnp.arange(h, dtype=F32)))
    idx = jnp.arange(c, dtype=F32)
    rel = idx[:, None] - idx[None, :]
    idec = jnp.where(rel[None] >= 0,
                     jnp.exp(jnp.maximum(rel, 0.0)[None] * log_gamma[:, None, None]), 0.0)
    qdec = jnp.exp((idx + 1.0)[:, None] * log_gamma[None, :])
    kdec = jnp.exp((c - 1.0 - idx)[:, None] * log_gamma[None, :])
    cdec = jnp.exp(c * log_gamma)[None, :]
    rep = lambda a: jnp.repeat(a, dv, axis=1)
    lane_head = np.arange(h * dk) // dk
    hmask = jnp.asarray(lane_head[None, :] == np.arange(h)[:, None], F32)
    bmask = jnp.asarray(lane_head[:, None] == (np.arange(h * dv) // dv)[None, :], F32)
    return cos, sp, sm, hmask, idec, rep(qdec), rep(kdec), rep(cdec), bmask


def retention(proj, batch, seq, tables, cast):
    c = RET_CHUNK
    nch = seq // c
    qw = RET_HEADS * RET_QK_DIM
    col = lambda name: _NAT[name][0] // _NAT[name][1]
    row = lambda b, n: b * nch + n
    const2 = lambda b, n: (0, 0)
    cos, sp, sm, hmask, idec, qdec, kdec, cdec, bmask = tables
    tab = pl.BlockSpec((c, LANE), lambda b, n: (n, 0))
    cast_in, cast_out, cast_shape = _cast_specs(cast, (batch, nch))
    return pl.pallas_call(
        _retention_kernel,
        grid=(batch, nch),
        in_specs=[cast_in,
                  pl.BlockSpec((c, qw), lambda b, n: (row(b, n), col("rq"))),
                  pl.BlockSpec((c, qw), lambda b, n: (row(b, n), col("rk"))),
                  pl.BlockSpec((c, RET_OUT), lambda b, n: (row(b, n), col("rv"))),
                  pl.BlockSpec((c, RET_OUT), lambda b, n: (row(b, n), col("rg"))),
                  tab, tab, tab,
                  pl.BlockSpec(hmask.shape, const2),
                  pl.BlockSpec(idec.shape, lambda b, n: (0, 0, 0)),
                  pl.BlockSpec(qdec.shape, const2),
                  pl.BlockSpec(kdec.shape, const2),
                  pl.BlockSpec(cdec.shape, const2),
                  pl.BlockSpec(bmask.shape, const2)],
        out_specs=[pl.BlockSpec((c, RET_OUT), lambda b, n: (row(b, n), 0)), cast_out],
        out_shape=[jax.ShapeDtypeStruct((batch * seq, RET_OUT), BF16), cast_shape],
        scratch_shapes=[pltpu.VMEM((qw, RET_OUT), F32)],
        compiler_params=_cparams(("arbitrary", "arbitrary")),
        name="retention",
    )(cast[0], proj, proj, proj, proj, cos, sp, sm, hmask, idec, qdec, kdec, cdec, bmask)


LOG2E = math.log2(math.e)
M_INIT = float(np.finfo(np.float32).min)


def _flash_update(s, v_c, m_ref, l_ref, acc_ref, h):
    reps = s.shape[1] // LANE

    def fold(x, op):
        tiles = [x[:, r * LANE:(r + 1) * LANE] for r in range(reps)]
        while len(tiles) > 1:
            tiles = [op(a, b) for a, b in zip(tiles[::2], tiles[1::2])]
        return tiles[0]

    m_prev = m_ref[h]
    m_new = jnp.maximum(m_prev, jnp.max(fold(s, jnp.maximum), axis=1, keepdims=True))
    p = jnp.exp2(s - jnp.tile(m_new, (1, reps)))
    alpha = jnp.exp2(m_prev - m_new)
    l_ref[h] = alpha * l_ref[h] + jnp.sum(fold(p, jnp.add), axis=1, keepdims=True)
    m_ref[h] = m_new
    acc_ref[h] = acc_ref[h] * alpha + jnp.dot(p.astype(BF16), v_c, preferred_element_type=F32)


INT_MIN = -2 ** 31
IDX_GROUP = 8
ROW_BLOCK = 64
PASSES_PER_CHECK = 4
COUNT_ROWS = 32
KV_PER_LOOP = 4


def _dsa_kernel(*refs, tq, topk, nqi):
    qi_refs = refs[:nqi]
    (w_ref, kidx_ref, q_ref, k_ref, v_ref, o_ref,
     qis_ref, wb_ref, lg_ref, key_ref, bias_ref, qs_ref, m_ref, l_ref, acc_ref) = refs[nqi:]
    tk = tq
    i = pl.program_id(1)
    nchunks = i + 1
    reps = tk // LANE
    ngroups = IDX_HEADS // IDX_GROUP
    group = DSA_HEADS // DSA_KV_HEADS
    d = DSA_HEAD_DIM
    heads_per_ref = IDX_HEADS // nqi

    w = w_ref[...].astype(F32) * (IDX_HEADS ** -0.5)
    for h in range(IDX_HEADS):
        r, t = divmod(h, heads_per_ref)
        qis_ref[h * tq:(h + 1) * tq, :] = qi_refs[r][:, t * LANE:(t + 1) * LANE]
        wb_ref[h] = jnp.broadcast_to(w[:, h:h + 1], (tq, LANE))

    def score_chunk(c, diagonal):
        r0 = pl.multiple_of(c * tk, tk)
        kc = kidx_ref[pl.ds(r0, tk), :]
        for g in range(ngroups):
            lg_ref[g] = lax.dot_general(
                qis_ref[g * IDX_GROUP * tq:(g + 1) * IDX_GROUP * tq, :], kc,
                (((1,), (1,)), ((), ())), preferred_element_type=F32)
        for rb in range(tq // ROW_BLOCK):
            q0 = rb * ROW_BLOCK
            rows = slice(q0, q0 + ROW_BLOCK)
            acc = jnp.zeros((ROW_BLOCK, tk), F32)
            for h in range(IDX_HEADS):
                g, hg = divmod(h, IDX_GROUP)
                lg = lg_ref[g, hg * tq + q0:hg * tq + q0 + ROW_BLOCK, :]
                acc = acc + jnp.maximum(lg, 0.0) * jnp.tile(wb_ref[h, rows, :], (1, reps))
            bits = pltpu.bitcast(acc, jnp.int32)
            key = bits ^ (lax.shift_right_arithmetic(bits, 31) & 0x7FFFFFFF)
            if diagonal:
                kpos = lax.broadcasted_iota(jnp.int32, (ROW_BLOCK, tk), 1)
                qpos = q0 + lax.broadcasted_iota(jnp.int32, (ROW_BLOCK, tk), 0)
                key = jnp.where(kpos <= qpos, key, INT_MIN)
            key_ref[c, rows, :] = key
        bias_ref[c] = pltpu.bitcast(key_ref[c].T, F32)

    def score_body(c, _):
        score_chunk(c, False)
        return 0

    lax.fori_loop(0, i, score_body, 0)
    score_chunk(i, True)

    def count_ge(cand):
        def body(c, cnt):
            ge = (pltpu.bitcast(bias_ref[c], jnp.int32) >= cand).astype(jnp.int32)
            return cnt + jnp.sum(ge.reshape(tk // COUNT_ROWS, COUNT_ROWS, tq), axis=0)
        cnt = lax.fori_loop(0, nchunks, body, jnp.zeros((COUNT_ROWS, tq), jnp.int32))
        return jnp.sum(cnt, axis=0, keepdims=True)

    def bit_cond(state):
        b, _, cnt = state
        return jnp.logical_and(b < 32, jnp.max(jnp.abs(cnt - topk)) > 0)

    def bit_body(state):
        b, t, cnt = state
        for u in range(PASSES_PER_CHECK):
            cand = t ^ lax.shift_left(jnp.int32(1), 31 - (b + u))
            n = count_ge(cand)
            take = n >= topk
            t, cnt = jnp.where(take, cand, t), jnp.where(take, n, cnt)
        return b + PASSES_PER_CHECK, t, cnt

    t0 = jnp.full((1, tq), INT_MIN, jnp.int32)
    c0 = jnp.full((1, tq), nchunks * tk, jnp.int32)
    _, thr_row, _ = lax.while_loop(bit_cond, bit_body, (jnp.int32(0), t0, c0))
    thr_row = jnp.maximum(thr_row, INT_MIN + 1)
    thr = jnp.broadcast_to(thr_row, (LANE, tq)).T
    thr = jnp.tile(thr, (1, reps))

    def bias_body(c, _):
        bias_ref[c] = jnp.where(key_ref[c] >= thr, 0.0, -jnp.inf)
        return 0

    lax.fori_loop(0, nchunks, bias_body, 0)

    @pl.when(nchunks < key_ref.shape[0])
    def _():
        bias_ref[nchunks] = jnp.full((tq, tk), -jnp.inf, F32)

    hpl = KV_PER_LOOP * group
    for g0 in range(0, DSA_KV_HEADS, KV_PER_LOOP):
        for hh in range(hpl):
            h = g0 * group + hh
            qs_ref[hh * tq:(hh + 1) * tq, :] = q_ref[:, h * d:(h + 1) * d]
        m_ref[...] = jnp.full_like(m_ref, M_INIT)
        l_ref[...] = jnp.zeros_like(l_ref)
        acc_ref[...] = jnp.zeros_like(acc_ref)

        def att_body(c2, _, g0=g0):
            r0 = pl.multiple_of(c2 * 2 * tk, 2 * tk)
            bias = jnp.concatenate([bias_ref[2 * c2], bias_ref[2 * c2 + 1]], axis=1)
            for gg in range(KV_PER_LOOP):
                g = g0 + gg
                kc = k_ref[pl.ds(r0, 2 * tk), g * d:(g + 1) * d]
                vc = v_ref[pl.ds(r0, 2 * tk), g * d:(g + 1) * d]
                s_all = lax.dot_general(qs_ref[gg * group * tq:(gg + 1) * group * tq, :], kc,
                                        (((1,), (1,)), ((), ())), preferred_element_type=F32)
                for hh in range(group):
                    _flash_update(s_all[hh * tq:(hh + 1) * tq] + bias, vc, m_ref, l_ref, acc_ref,
                                  gg * group + hh)
            return 0

        lax.fori_loop(0, (nchunks + 1) // 2, att_body, 0)
        for hh in range(hpl):
            h = g0 * group + hh
            o_ref[:, h * d:(h + 1) * d] = (acc_ref[hh] / l_ref[hh]).astype(o_ref.dtype)


def dsa_attention(proj, tail, batch, seq, *, tq=256):
    nq = seq // tq
    assert nq % 2 == 0
    topk = min(DSA_TOPK, seq // 4)
    hpl = KV_PER_LOOP * DSA_HEADS // DSA_KV_HEADS
    kvw = DSA_KV_HEADS * DSA_HEAD_DIM
    iqw = IDX_HEADS * IDX_HEAD_DIM
    nqi = iqw // QI_BLOCK
    col = lambda name, width: _NAT[name][0] // width
    qi_specs = [pl.BlockSpec((tq, QI_BLOCK), lambda b, i, r=r: (b * nq + i, col("iq", QI_BLOCK) + r))
                for r in range(nqi)]
    return pl.pallas_call(
        functools.partial(_dsa_kernel, tq=tq, topk=topk, nqi=nqi),
        grid=(batch, nq),
        in_specs=qi_specs + [
            pl.BlockSpec((tq, LANE), lambda b, i: (b * nq + i, _TAIL["iw"][0] // LANE)),
            pl.BlockSpec((seq, IDX_HEAD_DIM), lambda b, i: (b, col("ik", IDX_HEAD_DIM))),
            pl.BlockSpec((tq, DSA_OUT), lambda b, i: (b * nq + i, col("dq", DSA_OUT))),
            pl.BlockSpec((seq, kvw), lambda b, i: (b, col("dk", kvw))),
            pl.BlockSpec((seq, kvw), lambda b, i: (b, col("dv", kvw)))],
        out_specs=pl.BlockSpec((tq, DSA_OUT), lambda b, i: (b * nq + i, 0)),
        out_shape=jax.ShapeDtypeStruct((batch * seq, DSA_OUT), BF16),
        scratch_shapes=[pltpu.VMEM((IDX_HEADS * tq, IDX_HEAD_DIM), BF16),
                        pltpu.VMEM((IDX_HEADS, tq, LANE), F32),
                        pltpu.VMEM((IDX_HEADS // IDX_GROUP, IDX_GROUP * tq, tq), F32),
                        pltpu.VMEM((nq, tq, tq), jnp.int32),
                        pltpu.VMEM((nq, tq, tq), F32),
                        pltpu.VMEM((hpl * tq, DSA_HEAD_DIM), BF16),
                        pltpu.VMEM((hpl, tq, LANE), F32),
                        pltpu.VMEM((hpl, tq, LANE), F32),
                        pltpu.VMEM((hpl, tq, DSA_HEAD_DIM), F32)],
        compiler_params=_cparams(("parallel", "arbitrary")),
        name="dsa_attention",
    )(*([proj] * nqi), tail, proj, proj, proj, proj)


def _mla_kernel(cast_in_ref, q_ref, kn_ref, kpe_ref, v_ref, o_ref, cast_out_ref, m_ref, l_ref,
                acc_ref, *, tq, hb):
    tk = tq
    i = pl.program_id(2)
    cast_out_ref[...] = cast_in_ref[...].astype(BF16)
    m_ref[...] = jnp.full_like(m_ref, M_INIT)
    l_ref[...] = jnp.zeros_like(l_ref)
    acc_ref[...] = jnp.zeros_like(acc_ref)
    causal = (lax.broadcasted_iota(jnp.int32, (tq, tk), 1)
              <= lax.broadcasted_iota(jnp.int32, (tq, tk), 0))

    def step(c, masked):
        r0 = pl.multiple_of(c * tk, tk)
        kpe = kpe_ref[pl.ds(r0, tk), :]
        for h in range(hb):
            cols = slice(h * MLA_NOPE_DIM, (h + 1) * MLA_NOPE_DIM)
            kf = jnp.concatenate([kn_ref[pl.ds(r0, tk), cols], kpe], axis=1)
            s = lax.dot_general(q_ref[:, h * MLA_QK_PAD:(h + 1) * MLA_QK_PAD], kf,
                                (((1,), (1,)), ((), ())), preferred_element_type=F32)
            if masked:
                s = jnp.where(causal, s, -jnp.inf)
            _flash_update(s, v_ref[pl.ds(r0, tk), cols], m_ref, l_ref, acc_ref, h)

    def body(c, _):
        step(c, False)
        return 0

    lax.fori_loop(0, i, body, 0)
    step(i, True)
    for h in range(hb):
        o_ref[:, h * MLA_V_DIM:(h + 1) * MLA_V_DIM] = (acc_ref[h] / l_ref[h]).astype(o_ref.dtype)


def mla_attention(q, kv, tail, batch, seq, cast, *, tq=512, hb=4):
    nq = seq // tq
    ng = MLA_HEADS // hb
    cast_in, cast_out, cast_shape = _cast_specs(cast, (batch, ng, nq))
    return pl.pallas_call(
        functools.partial(_mla_kernel, tq=tq, hb=hb),
        grid=(batch, ng, nq),
        in_specs=[cast_in,
                  pl.BlockSpec((tq, hb * MLA_QK_PAD), lambda b, g, i: (b * nq + i, g)),
                  pl.BlockSpec((seq, hb * MLA_NOPE_DIM), lambda b, g, i: (b, g)),
                  pl.BlockSpec((seq, LANE), lambda b, g, i: (b, _TAIL["kr"][0] // LANE)),
                  pl.BlockSpec((seq, hb * MLA_V_DIM), lambda b, g, i: (b, ng + g))],
        out_specs=[pl.BlockSpec((tq, hb * MLA_V_DIM), lambda b, g, i: (b * nq + i, g)), cast_out],
        out_shape=[jax.ShapeDtypeStruct((batch * seq, MLA_OUT), BF16), cast_shape],
        scratch_shapes=[pltpu.VMEM((hb, tq, LANE), F32), pltpu.VMEM((hb, tq, LANE), F32),
                        pltpu.VMEM((hb, tq, MLA_V_DIM), F32)],
        compiler_params=_cparams(("arbitrary", "arbitrary", "arbitrary")),
        name="mla_attention",
    )(cast[0], q, kv, tail, kv)


def _tail_w_in(wt):
    parts, pos = [], 0
    for name in sorted(_TAIL, key=lambda s: _TAIL[s][0]):
        off, width = _TAIL[name]
        assert off == pos
        a = wt[_NAT[name][0]:_NAT[name][0] + _NAT[name][1]]
        parts.append(a)
        if a.shape[0] < width:
            parts.append(jnp.zeros((width - a.shape[0], wt.shape[1]), wt.dtype))
        pos += width
    assert pos == TAIL_WIDTH
    return jnp.concatenate(parts, axis=0)


def _relayout_w_uq(w):
    dep, r = w.shape[:2]
    a = w.reshape(dep, r, MLA_HEADS, MLA_NOPE_DIM + MLA_ROPE_DIM)
    a = jnp.pad(a, ((0, 0), (0, 0), (0, 0), (0, MLA_QK_PAD - a.shape[3])))
    return a.reshape(dep, r, MLA_HEADS * MLA_QK_PAD).astype(BF16)


def _relayout_w_ukv(w):
    dep, r = w.shape[:2]
    a = w.reshape(dep, r, MLA_HEADS, 2, MLA_NOPE_DIM).transpose(0, 1, 3, 2, 4)
    return a.reshape(dep, r, 2 * MLA_HEADS * MLA_NOPE_DIM).astype(BF16)


def _main_tile_modes():
    modes = [(1.0, None)] * (MAIN_WIDTH // LANE)

    def fill(name, mode):
        off, width = _NAT[name]
        for t in range(off // LANE, (off + width) // LANE):
            modes[t] = mode

    fill("dq", (DSA_HEAD_DIM ** -0.5 * LOG2E, 0))
    fill("dk", (1.0, 0))
    fill("iq", (IDX_HEAD_DIM ** -0.5, 0))
    fill("ik", (1.0, 0))
    return modes


def _tail_tile_modes():
    modes = [(1.0, None)] * (TAIL_WIDTH // LANE)
    modes[_TAIL["kr"][0] // LANE] = (1.0, 0)
    return modes


def kernel(x, attn_norm, w_in, mla_q_norm, mla_kv_norm, w_uq, w_ukv, w_o, mlp_norm, w_up,
           w_down, final_norm):
    batch, seq, d_model = x.shape
    depth = w_in.shape[0]
    m = batch * seq

    ret_tables = _retention_tables(seq)
    part_tab = _rope_tables(seq, PARTIAL_ROPE_THETA, PARTIAL_ROT_DIM)
    mla_tab = _rope_tables(seq, MLA_ROPE_THETA, MLA_ROPE_DIM)
    mla_scale = (MLA_NOPE_DIM + MLA_ROPE_DIM) ** -0.5 * LOG2E
    uq_modes = [(mla_scale, None), (mla_scale, 0)] * MLA_HEADS

    w_uq_b, w_ukv_b = _relayout_w_uq(w_uq), _relayout_w_ukv(w_ukv)
    w_in_t = jnp.swapaxes(w_in, 1, 2)

    h = x.reshape(m, d_model)
    xin, w_in_l = rmsnorm(h, attn_norm[0], BF16, cast=(w_in_t, 0))
    for l in range(depth):
        pre = None if l == 0 else attn_norm[l]
        proj = matmul(xin, w_in_l, BF16, name="mm_in", w_t=True, n_out=MAIN_WIDTH, bn=MAIN_BN,
                      prenorm=pre, tile_modes=_main_tile_modes(), rope_tabs=(part_tab,), seq=seq)
        tail = matmul(xin, _tail_w_in(w_in_l), BF16, name="mm_in_tail", w_t=True, bn=TAIL_WIDTH,
                      prenorm=pre, tile_modes=_tail_tile_modes(), rope_tabs=(mla_tab,), seq=seq)

        ret, w_o_l = retention(proj, batch, seq, ret_tables, cast=(w_o, l))
        dsa = dsa_attention(proj, tail, batch, seq)

        q_full = matmul(tail, w_uq_b, BF16, layer=l, name="mm_uq", x_cols=_TAIL["cq"],
                        prenorm=mla_q_norm[l], tile_modes=uq_modes, rope_tabs=(mla_tab,), seq=seq)
        kv_up = matmul(tail, w_ukv_b, BF16, layer=l, name="mm_ukv", x_cols=_TAIL["ckv"],
                       prenorm=mla_kv_norm[l])
        mla, w_up_l = mla_attention(q_full, kv_up, tail, batch, seq, cast=(w_up, l))

        h, hb = matmul((ret, dsa, mla), w_o_l, F32, res=h, name="mm_o", emit_bf16=True)
        up, w_down_l = matmul(hb, w_up_l, BF16, act="relu2", name="mm_up", prenorm=mlp_norm[l],
                              cast=(w_down, l))
        if l + 1 < depth:
            h, xin, w_in_l = matmul(up, w_down_l, F32, res=h, bk=DOWN_BK, name="mm_down",
                                    emit_bf16=True, cast=(w_in_t, l + 1))
        else:
            h = matmul(up, w_down_l, F32, res=h, bk=DOWN_BK, name="mm_down")
    out = rmsnorm(h, final_norm, x.dtype)
    return out.reshape(batch, seq, d_model)
```

```python
import functools
import math

import numpy as np
import jax
import jax.numpy as jnp
from jax import lax
from jax.experimental import pallas as pl
from jax.experimental.pallas import tpu as pltpu

F32 = jnp.float32
BF16 = jnp.bfloat16

RET_HEADS = 8
RET_QK_DIM = 64
RET_V_DIM = 128
RET_CHUNK = 128
RET_ROPE_THETA = 10000.0
DSA_HEADS = 12
DSA_KV_HEADS = 4
DSA_HEAD_DIM = 128
IDX_HEADS = 32
IDX_HEAD_DIM = 128
DSA_TOPK = 256
PARTIAL_ROT_DIM = 32
PARTIAL_ROPE_THETA = 500000.0
MLA_HEADS = 12
MLA_Q_LORA = 768
MLA_KV_LORA = 256
MLA_NOPE_DIM = 128
MLA_ROPE_DIM = 64
MLA_V_DIM = 128
MLA_ROPE_THETA = 10000.0
NORM_EPS = 1e-6

LANE = 128
VMEM_LIMIT = 60 * 1024 * 1024

RET_OUT = RET_HEADS * RET_V_DIM
DSA_OUT = DSA_HEADS * DSA_HEAD_DIM
MLA_OUT = MLA_HEADS * MLA_V_DIM
MLA_QK_PAD = 256

_NAT = {}
_off = 0
for _name, _w in (("rq", 512), ("rk", 512), ("rv", 1024), ("rg", 1024), ("dq", 1536),
                  ("dk", 512), ("dv", 512), ("iq", 4096), ("ik", 128), ("iw", 32),
                  ("cq", 768), ("ckv", 256), ("kr", 64)):
    _NAT[_name] = (_off, _w)
    _off += _w
IN_WIDTH = _off
MAIN_WIDTH = 10240
MAIN_BN = 1024
_TAIL = {"cq": (0, 768), "ckv": (768, 256), "iw": (1024, 128), "kr": (1152, 128)}
TAIL_WIDTH = 1280
QI_BLOCK = 512
DOWN_BK = 2048
NORM_ROWS = 64
NORM_ONCE_MIN = 2 ** 21


def _cparams(sem):
    return pltpu.CompilerParams(dimension_semantics=sem, vmem_limit_bytes=VMEM_LIMIT)


def _rmsnorm_kernel(x_ref, g_ref, *rest):
    o_ref = rest[-1] if len(rest) == 1 else rest[1]
    if len(rest) == 3:
        rest[2][...] = rest[0][...].astype(BF16)
    x = x_ref[...].astype(F32)
    ms = jnp.mean(x * x, axis=-1, keepdims=True)
    o_ref[...] = (x * lax.rsqrt(ms + NORM_EPS) * g_ref[...]).astype(o_ref.dtype)


def rmsnorm(x, g, out_dtype, *, bm=256, cast=None):
    m, width = x.shape
    assert m % bm == 0
    in_specs = [pl.BlockSpec((bm, width), lambda i: (i, 0)),
                pl.BlockSpec((1, width), lambda i: (0, 0))]
    out_specs = pl.BlockSpec((bm, width), lambda i: (i, 0))
    out_shape = jax.ShapeDtypeStruct((m, width), out_dtype)
    args = [x, g.reshape(1, width).astype(F32)]
    if cast is not None:
        cast_in, cast_out, cast_shape = _cast_specs(cast, (m // bm,))
        in_specs.append(cast_in)
        args.append(cast[0])
        out_specs, out_shape = [out_specs, cast_out], [out_shape, cast_shape]
    return pl.pallas_call(
        _rmsnorm_kernel,
        grid=(m // bm,),
        in_specs=in_specs,
        out_specs=out_specs,
        out_shape=out_shape,
        compiler_params=_cparams(("arbitrary",)),
        name="rmsnorm",
    )(*args)


def _mm_kernel(*refs, nx, act, has_res, has_norm, norm_once, nk, groups, halves, ntab, has_cast,
               emit_bf16, w_t):
    refs = list(refs)
    take = lambda n=1: [refs.pop(0) for _ in range(n)]
    x_refs = take(nx)
    (w_ref,) = take()
    r_ref = take()[0] if has_res else None
    g_ref = take()[0] if has_norm else None
    tab_refs = take(3 * ntab)
    cast_in_ref = take()[0] if has_cast else None
    (o_ref,) = take()
    ob_ref = take()[0] if emit_bf16 else None
    if has_cast:
        take()[0][...] = cast_in_ref[...].astype(BF16)
    acc_ref = take()[0] if nk > 1 else None
    xn_ref = take()[0] if norm_once else None

    def store_tiles(acc, pattern):
        for t, (scale, ridx) in enumerate(pattern):
            x = acc[:, t * LANE:(t + 1) * LANE]
            if ridx is not None:
                a, bp, bm = (tab_refs[3 * ridx + k][...] for k in range(3))
                half = halves[ridx]
                x = x * a + pltpu.roll(x, half, 1) * bp + pltpu.roll(x, LANE - half, 1) * bm
            if scale != 1.0:
                x = x * scale
            o_ref[:, t * LANE:(t + 1) * LANE] = x.astype(o_ref.dtype)

    def epilogue(acc):
        if act == "relu2":
            acc = jnp.square(jnp.maximum(acc, 0.0))
        if has_res:
            acc = acc + r_ref[...]
        if emit_bf16:
            ob_ref[...] = acc.astype(BF16)
        if groups is None:
            o_ref[...] = acc.astype(o_ref.dtype)
        elif len(groups) == 1:
            store_tiles(acc, groups[0][0])
        else:
            j = pl.program_id(1)
            for pattern, js in groups:
                cond = functools.reduce(jnp.logical_or, [j == jj for jj in js])
                pl.when(cond)(functools.partial(store_tiles, acc, pattern))

    def normalised(x):
        xf = x.astype(F32)
        ms = jnp.mean(xf * xf, axis=-1, keepdims=True)
        return (xf * lax.rsqrt(ms + NORM_EPS) * g_ref[...]).astype(BF16)

    if norm_once:
        @pl.when(pl.program_id(1) == 0)
        def _():
            def rows_body(r, _):
                rows = pl.ds(pl.multiple_of(r * NORM_ROWS, NORM_ROWS), NORM_ROWS)
                xn_ref[rows, :] = normalised(x_refs[0][rows, :])
                return 0
            lax.fori_loop(0, xn_ref.shape[0] // NORM_ROWS, rows_body, 0)

    if nk > 1:
        @pl.when(pl.program_id(2) == 0)
        def _():
            acc_ref[...] = jnp.zeros_like(acc_ref)

    part, k0 = None, 0
    for x_ref in x_refs:
        kw = x_ref.shape[1]
        if norm_once:
            x = xn_ref[...]
        elif has_norm:
            x = normalised(x_ref[...])
        else:
            x = x_ref[...]
        if w_t:
            d = lax.dot_general(x, w_ref[...], (((1,), (1,)), ((), ())),
                                preferred_element_type=F32)
        else:
            d = jnp.dot(x, w_ref[k0:k0 + kw, :], preferred_element_type=F32)
        part = d if part is None else part + d
        k0 += kw
    if nk == 1:
        epilogue(part)
    else:
        acc_ref[...] += part

        @pl.when(pl.program_id(2) == nk - 1)
        def _():
            epilogue(acc_ref[...])


def matmul(x, w, out_dtype, *, name, layer=None, act=None, res=None, bm=1024, bn=1024, bk=4096,
           n_out=None, tile_modes=None, rope_tabs=(), seq=None, cast=None, x_cols=None,
           prenorm=None, w_t=False, emit_bf16=False):
    xs = x if isinstance(x, (tuple, list)) else (x,)
    m = xs[0].shape[0]
    kdim = sum(a.shape[1] for a in xs) if x_cols is None else x_cols[1]
    assert kdim == w.shape[-1 if w_t else -2]
    n = w.shape[-2 if w_t else -1] if n_out is None else n_out
    bm, bn, bk = min(bm, m), min(bn, n), min(bk, kdim)
    assert m % bm == 0 and n % bn == 0 and kdim % bk == 0
    nk = kdim // bk
    assert len(xs) == 1 or nk == 1
    if x_cols is not None:
        assert len(xs) == 1 and nk == 1 and x_cols[0] % kdim == 0
        xcb = x_cols[0] // kdim
        in_specs = [pl.BlockSpec((bm, kdim), lambda i, j, k: (i, xcb))]
    elif len(xs) == 1:
        in_specs = [pl.BlockSpec((bm, bk), lambda i, j, k: (i, k))]
    else:
        in_specs = [pl.BlockSpec((bm, a.shape[1]), lambda i, j, k: (i, 0)) for a in xs]
    if w_t:
        assert layer is None and len(xs) == 1
        in_specs.append(pl.BlockSpec((bn, bk), lambda i, j, k: (j, k)))
    elif layer is None:
        in_specs.append(pl.BlockSpec((bk, bn), lambda i, j, k: (k, j)))
    else:
        in_specs.append(pl.BlockSpec((None, bk, bn), lambda i, j, k: (layer, k, j)))
    args = [*xs, w]
    if res is not None:
        in_specs.append(pl.BlockSpec((bm, bn), lambda i, j, k: (i, j)))
        args.append(res)
    if prenorm is not None:
        assert nk == 1 and len(xs) == 1
        in_specs.append(pl.BlockSpec((1, kdim), lambda i, j, k: (0, 0)))
        args.append(prenorm.reshape(1, kdim).astype(F32))
    groups = None
    if tile_modes is not None:
        assert len(tile_modes) == n // LANE and seq % bm == 0
        per_block = bn // LANE
        by_pattern = {}
        for j in range(n // bn):
            by_pattern.setdefault(tuple(tile_modes[j * per_block:(j + 1) * per_block]), []).append(j)
        groups = tuple((p, tuple(js)) for p, js in by_pattern.items())
        nseq = seq // bm
        for a, bp, bmt, _ in rope_tabs:
            in_specs += [pl.BlockSpec((bm, LANE), lambda i, j, k: (i % nseq, 0))] * 3
            args += [a, bp, bmt]
    tile_spec = pl.BlockSpec((bm, bn), lambda i, j, k: (i, j))
    out_specs = [tile_spec]
    out_shape = [jax.ShapeDtypeStruct((m, n), out_dtype)]
    if emit_bf16:
        out_specs.append(tile_spec)
        out_shape.append(jax.ShapeDtypeStruct((m, n), BF16))
    norm_once = prenorm is not None and n // bn > 1 and bm * kdim >= NORM_ONCE_MIN
    ordered = cast is not None or norm_once
    sem = ("arbitrary",) * 3 if ordered else ("parallel", "parallel", "arbitrary")
    if cast is not None:
        spec_in, spec_out, shape_out = _cast_specs(cast, (m // bm, n // bn, nk))
        in_specs.append(spec_in)
        args.append(cast[0])
        out_specs.append(spec_out)
        out_shape.append(shape_out)
    scratch = [pltpu.VMEM((bm, bn), F32)] if nk > 1 else []
    if norm_once:
        scratch.append(pltpu.VMEM((bm, kdim), BF16))
    outs = pl.pallas_call(
        functools.partial(_mm_kernel, nx=len(xs), act=act, has_res=res is not None,
                          has_norm=prenorm is not None, norm_once=norm_once, nk=nk,
                          groups=groups, halves=tuple(t[3] for t in rope_tabs),
                          ntab=len(rope_tabs), has_cast=cast is not None, emit_bf16=emit_bf16,
                          w_t=w_t),
        grid=(m // bm, n // bn, nk),
        in_specs=in_specs,
        out_specs=out_specs,
        out_shape=out_shape,
        scratch_shapes=scratch,
        compiler_params=_cparams(sem),
        name=name,
    )(*args)
    return outs[0] if len(outs) == 1 else tuple(outs)


def _cast_specs(cast, grid):
    src, layer = cast
    _, rows, cols = src.shape
    nsteps = math.prod(grid)
    rb = 16 * max(1, -(-rows // (16 * nsteps)))
    while rows % rb:
        rb += 16
    nb = rows // rb

    def step(*ids):
        s = 0
        for g, i in zip(grid, ids):
            s = s * g + i
        return jnp.minimum(s, nb - 1)

    spec_in = pl.BlockSpec((None, rb, cols), lambda *ids: (layer, step(*ids), 0))
    spec_out = pl.BlockSpec((rb, cols), lambda *ids: (step(*ids), 0))
    return spec_in, spec_out, jax.ShapeDtypeStruct((rows, cols), BF16)


def _rope_tables(seq, theta, rot_dim):
    half = rot_dim // 2
    inv = jnp.exp(-math.log(theta) * jnp.arange(half, dtype=F32) * (2.0 / rot_dim))
    ang = jnp.arange(seq, dtype=F32)[:, None] * inv[None, :]
    cos, sin = jnp.cos(ang), jnp.sin(ang)
    pad = LANE - rot_dim
    a = jnp.concatenate([cos, cos, jnp.ones((seq, pad), F32)], axis=1)
    bp = jnp.concatenate([jnp.zeros((seq, half), F32), sin, jnp.zeros((seq, pad), F32)], axis=1)
    bm = jnp.concatenate([-sin, jnp.zeros((seq, half + pad), F32)], axis=1)
    return a, bp, bm, half


def _retention_kernel(cast_in_ref, q_ref, k_ref, v_ref, g_ref, cos_ref, sp_ref, sm_ref, hmask_ref,
                      idec_ref, qdec_ref, kdec_ref, cdec_ref, bmask_ref, o_ref, cast_out_ref,
                      state_ref):
    half = RET_QK_DIM // 2
    cast_out_ref[...] = cast_in_ref[...].astype(BF16)

    @pl.when(pl.program_id(1) == 0)
    def _():
        state_ref[...] = jnp.zeros_like(state_ref)

    cos, sp, sm = cos_ref[...], sp_ref[...], sm_ref[...]

    def rot(x):
        tiles = []
        for t in range(x.shape[1] // LANE):
            xt = x[:, t * LANE:(t + 1) * LANE]
            tiles.append(xt * cos + pltpu.roll(xt, half, 1) * sp
                         + pltpu.roll(xt, LANE - half, 1) * sm)
        return jnp.concatenate(tiles, axis=1)

    q = rot(q_ref[...].astype(F32))
    k = rot(k_ref[...].astype(F32)) * (RET_QK_DIM ** -0.5)
    v = v_ref[...].astype(F32)
    kt = k.T.astype(BF16)
    state = state_ref[...]
    cross = jnp.dot(q.astype(BF16), state.astype(BF16), preferred_element_type=F32) * qdec_ref[...]
    for h in range(RET_HEADS):
        cols = slice(h * RET_V_DIM, (h + 1) * RET_V_DIM)
        qm = (q * hmask_ref[h:h + 1, :]).astype(BF16)
        scores = jnp.dot(qm, kt, preferred_element_type=F32) * idec_ref[h]
        inner = jnp.dot(scores.astype(BF16), v[:, cols].astype(BF16), preferred_element_type=F32)
        o = inner + cross[:, cols]
        o = o * lax.rsqrt(jnp.mean(o * o, axis=-1, keepdims=True) + NORM_EPS)
        gate = g_ref[:, cols].astype(F32)
        o_ref[:, cols] = (gate * jax.nn.sigmoid(gate) * o).astype(o_ref.dtype)
    vd = (v * kdec_ref[...]).astype(BF16)
    upd = jnp.dot(kt, vd, preferred_element_type=F32)
    state_ref[...] = state * cdec_ref[...] + upd * bmask_ref[...]


def _retention_tables(seq):
    h, c, dv, dk = RET_HEADS, RET_CHUNK, RET_V_DIM, RET_QK_DIM
    half = dk // 2
    inv = jnp.exp(-math.log(RET_ROPE_THETA) * jnp.arange(half, dtype=F32) * (2.0 / dk))
    ang = jnp.arange(seq, dtype=F32)[:, None] * inv[None, :]
    cos1, sin1, zero = jnp.cos(ang), jnp.sin(ang), jnp.zeros((seq, half), F32)
    reps = LANE // dk
    cos = jnp.tile(jnp.concatenate([cos1, cos1], axis=1), (1, reps))
    sp = jnp.tile(jnp.concatenate([zero, sin1], axis=1), (1, reps))
    sm = jnp.tile(jnp.concatenate([-sin1, zero], axis=1), (1, reps))
    log_gamma = jnp.log1p(-jnp.exp2(-5.0 - jnp.arange(h, dtype=F32)))
    idx = jnp.arange(c, dtype=F32)
    rel = idx[:, None] - idx[None, :]
    idec = jnp.where(rel[None] >= 0,
                     jnp.exp(jnp.maximum(rel, 0.0)[None] * log_gamma[:, None, None]), 0.0)
    qdec = jnp.exp((idx + 1.0)[:, None] * log_gamma[None, :])
    kdec = jnp.exp((c - 1.0 - idx)[:, None] * log_gamma[None, :])
    cdec = jnp.exp(c * log_gamma)[None, :]
    rep = lambda a: jnp.repeat(a, dv, axis=1)
    lane_head = np.arange(h * dk) // dk
    hmask = jnp.asarray(lane_head[None, :] == np.arange(h)[:, None], F32)
    bmask = jnp.asarray(lane_head[:, None] == (np.arange(h * dv) // dv)[None, :], F32)
    return cos, sp, sm, hmask, idec, rep(qdec), rep(kdec), rep(cdec), bmask


def retention(proj, batch, seq, tables, cast):
    c = RET_CHUNK
    nch = seq // c
    qw = RET_HEADS * RET_QK_DIM
    col = lambda name: _NAT[name][0] // _NAT[name][1]
    row = lambda b, n: b * nch + n
    const2 = lambda b, n: (0, 0)
    cos, sp, sm, hmask, idec, qdec, kdec, cdec, bmask = tables
    tab = pl.BlockSpec((c, LANE), lambda b, n: (n, 0))
    cast_in, cast_out, cast_shape = _cast_specs(cast, (batch, nch))
    return pl.pallas_call(
        _retention_kernel,
        grid=(batch, nch),
        in_specs=[cast_in,
                  pl.BlockSpec((c, qw), lambda b, n: (row(b, n), col("rq"))),
                  pl.BlockSpec((c, qw), lambda b, n: (row(b, n), col("rk"))),
                  pl.BlockSpec((c, RET_OUT), lambda b, n: (row(b, n), col("rv"))),
                  pl.BlockSpec((c, RET_OUT), lambda b, n: (row(b, n), col("rg"))),
                  tab, tab, tab,
                  pl.BlockSpec(hmask.shape, const2),
                  pl.BlockSpec(idec.shape, lambda b, n: (0, 0, 0)),
                  pl.BlockSpec(qdec.shape, const2),
                  pl.BlockSpec(kdec.shape, const2),
                  pl.BlockSpec(cdec.shape, const2),
                  pl.BlockSpec(bmask.shape, const2)],
        out_specs=[pl.BlockSpec((c, RET_OUT), lambda b, n: (row(b, n), 0)), cast_out],
        out_shape=[jax.ShapeDtypeStruct((batch * seq, RET_OUT), BF16), cast_shape],
        scratch_shapes=[pltpu.VMEM((qw, RET_OUT), F32)],
        compiler_params=_cparams(("arbitrary", "arbitrary")),
        name="retention",
    )(cast[0], proj, proj, proj, proj, cos, sp, sm, hmask, idec, qdec, kdec, cdec, bmask)


LOG2E = math.log2(math.e)
M_INIT = float(np.finfo(np.float32).min)


def _flash_update(s, v_c, m_ref, l_ref, acc_ref, h):
    reps = s.shape[1] // LANE

    def fold(x, op):
        tiles = [x[:, r * LANE:(r + 1) * LANE] for r in range(reps)]
        while len(tiles) > 1:
            tiles = [op(a, b) for a, b in zip(tiles[::2], tiles[1::2])]
        return tiles[0]

    m_prev = m_ref[h]
    m_new = jnp.maximum(m_prev, jnp.max(fold(s, jnp.maximum), axis=1, keepdims=True))
    p = jnp.exp2(s - jnp.tile(m_new, (1, reps)))
    alpha = jnp.exp2(m_prev - m_new)
    l_ref[h] = alpha * l_ref[h] + jnp.sum(fold(p, jnp.add), axis=1, keepdims=True)
    m_ref[h] = m_new
    acc_ref[h] = acc_ref[h] * alpha + jnp.dot(p.astype(BF16), v_c, preferred_element_type=F32)


INT_MIN = -2 ** 31
ROW_BLOCK = 64
PASSES_PER_CHECK = 4
COUNT_ROWS = 32
KV_PER_LOOP = 4


def _dsa_kernel(*refs, tq, topk, nqi):
    qi_refs = refs[:nqi]
    (w_ref, kidx_ref, q_ref, k_ref, v_ref, o_ref,
     wb_ref, lg_ref, key_ref, bias_ref, qs_ref, m_ref, l_ref, acc_ref) = refs[nqi:]
    tk = tq
    i = pl.program_id(1)
    nchunks = i + 1
    reps = tk // LANE
    group = DSA_HEADS // DSA_KV_HEADS
    d = DSA_HEAD_DIM
    heads_per_ref = IDX_HEADS // nqi

    w = w_ref[...].astype(F32) * (IDX_HEADS ** -0.5)
    for h in range(IDX_HEADS):
        wb_ref[h] = jnp.broadcast_to(w[:, h:h + 1], (tq, LANE))

    def score_chunk(c, diagonal):
        r0 = pl.multiple_of(c * tk, tk)
        kc = kidx_ref[pl.ds(r0, tk), :]
        for h in range(IDX_HEADS):
            r, t = divmod(h, heads_per_ref)
            lg_ref[h] = lax.dot_general(qi_refs[r][:, t * LANE:(t + 1) * LANE], kc,
                                        (((1,), (1,)), ((), ())), preferred_element_type=F32)
        for rb in range(tq // ROW_BLOCK):
            q0 = rb * ROW_BLOCK
            rows = slice(q0, q0 + ROW_BLOCK)
            acc = jnp.zeros((ROW_BLOCK, tk), F32)
            for h in range(IDX_HEADS):
                acc = acc + jnp.maximum(lg_ref[h, rows, :], 0.0) * jnp.tile(wb_ref[h, rows, :],
                                                                            (1, reps))
            bits = pltpu.bitcast(acc, jnp.int32)
            key = bits ^ (lax.shift_right_arithmetic(bits, 31) & 0x7FFFFFFF)
            if diagonal:
                kpos = lax.broadcasted_iota(jnp.int32, (ROW_BLOCK, tk), 1)
                qpos = q0 + lax.broadcasted_iota(jnp.int32, (ROW_BLOCK, tk), 0)
                key = jnp.where(kpos <= qpos, key, INT_MIN)
            key_ref[c, rows, :] = key
        bias_ref[c] = pltpu.bitcast(key_ref[c].T, F32)

    def score_body(c, _):
        score_chunk(c, False)
        return 0

    lax.fori_loop(0, i, score_body, 0)
    score_chunk(i, True)

    def count_ge(cand):
        def body(c, cnt):
            ge = (pltpu.bitcast(bias_ref[c], jnp.int32) >= cand).astype(jnp.int32)
            return cnt + jnp.sum(ge.reshape(tk // COUNT_ROWS, COUNT_ROWS, tq), axis=0)
        cnt = lax.fori_loop(0, nchunks, body, jnp.zeros((COUNT_ROWS, tq), jnp.int32))
        return jnp.sum(cnt, axis=0, keepdims=True)

    def bit_cond(state):
        b, _, cnt = state
        return jnp.logical_and(b < 32, jnp.max(jnp.abs(cnt - topk)) > 0)

    def bit_body(state):
        b, t, cnt = state
        for u in range(PASSES_PER_CHECK):
            cand = t ^ lax.shift_left(jnp.int32(1), 31 - (b + u))
            n = count_ge(cand)
            take = n >= topk
            t, cnt = jnp.where(take, cand, t), jnp.where(take, n, cnt)
        return b + PASSES_PER_CHECK, t, cnt

    t0 = jnp.full((1, tq), INT_MIN, jnp.int32)
    c0 = jnp.full((1, tq), nchunks * tk, jnp.int32)
    _, thr_row, _ = lax.while_loop(bit_cond, bit_body, (jnp.int32(0), t0, c0))
    thr_row = jnp.maximum(thr_row, INT_MIN + 1)
    thr = jnp.broadcast_to(thr_row, (LANE, tq)).T
    thr = jnp.tile(thr, (1, reps))

    def bias_body(c, _):
        bias_ref[c] = jnp.where(key_ref[c] >= thr, 0.0, -jnp.inf)
        return 0

    lax.fori_loop(0, nchunks, bias_body, 0)

    @pl.when(nchunks < key_ref.shape[0])
    def _():
        bias_ref[nchunks] = jnp.full((tq, tk), -jnp.inf, F32)

    hpl = KV_PER_LOOP * group
    for g0 in range(0, DSA_KV_HEADS, KV_PER_LOOP):
        for hh in range(hpl):
            h = g0 * group + hh
            qs_ref[hh * tq:(hh + 1) * tq, :] = q_ref[:, h * d:(h + 1) * d]
        m_ref[...] = jnp.full_like(m_ref, M_INIT)
        l_ref[...] = jnp.zeros_like(l_ref)
        acc_ref[...] = jnp.zeros_like(acc_ref)

        def att_body(c2, _, g0=g0):
            r0 = pl.multiple_of(c2 * 2 * tk, 2 * tk)
            bias = jnp.concatenate([bias_ref[2 * c2], bias_ref[2 * c2 + 1]], axis=1)
            for gg in range(KV_PER_LOOP):
                g = g0 + gg
                kc = k_ref[pl.ds(r0, 2 * tk), g * d:(g + 1) * d]
                vc = v_ref[pl.ds(r0, 2 * tk), g * d:(g + 1) * d]
                s_all = lax.dot_general(qs_ref[gg * group * tq:(gg + 1) * group * tq, :], kc,
                                        (((1,), (1,)), ((), ())), preferred_element_type=F32)
                _flash_update(s_all + jnp.tile(bias, (group, 1)), vc, m_ref, l_ref, acc_ref, gg)
            return 0

        lax.fori_loop(0, (nchunks + 1) // 2, att_body, 0)
        for hh in range(hpl):
            h = g0 * group + hh
            gg, rows = hh // group, slice((hh % group) * tq, (hh % group + 1) * tq)
            o_ref[:, h * d:(h + 1) * d] = (acc_ref[gg, rows, :]
                                           / l_ref[gg, rows, :]).astype(o_ref.dtype)


def dsa_attention(proj, tail, batch, seq, *, tq=256):
    nq = seq // tq
    assert nq % 2 == 0
    topk = min(DSA_TOPK, seq // 4)
    hpl = KV_PER_LOOP * DSA_HEADS // DSA_KV_HEADS
    gtq = DSA_HEADS // DSA_KV_HEADS * tq
    kvw = DSA_KV_HEADS * DSA_HEAD_DIM
    iqw = IDX_HEADS * IDX_HEAD_DIM
    nqi = iqw // QI_BLOCK
    col = lambda name, width: _NAT[name][0] // width
    qi_specs = [pl.BlockSpec((tq, QI_BLOCK), lambda b, i, r=r: (b * nq + i, col("iq", QI_BLOCK) + r))
                for r in range(nqi)]
    return pl.pallas_call(
        functools.partial(_dsa_kernel, tq=tq, topk=topk, nqi=nqi),
        grid=(batch, nq),
        in_specs=qi_specs + [
            pl.BlockSpec((tq, LANE), lambda b, i: (b * nq + i, _TAIL["iw"][0] // LANE)),
            pl.BlockSpec((seq, IDX_HEAD_DIM), lambda b, i: (b, col("ik", IDX_HEAD_DIM))),
            pl.BlockSpec((tq, DSA_OUT), lambda b, i: (b * nq + i, col("dq", DSA_OUT))),
            pl.BlockSpec((seq, kvw), lambda b, i: (b, col("dk", kvw))),
            pl.BlockSpec((seq, kvw), lambda b, i: (b, col("dv", kvw)))],
        out_specs=pl.BlockSpec((tq, DSA_OUT), lambda b, i: (b * nq + i, 0)),
        out_shape=jax.ShapeDtypeStruct((batch * seq, DSA_OUT), BF16),
        scratch_shapes=[pltpu.VMEM((IDX_HEADS, tq, LANE), F32),
                        pltpu.VMEM((IDX_HEADS, tq, tq), F32),
                        pltpu.VMEM((nq, tq, tq), jnp.int32),
                        pltpu.VMEM((nq, tq, tq), F32),
                        pltpu.VMEM((hpl * tq, DSA_HEAD_DIM), BF16),
                        pltpu.VMEM((KV_PER_LOOP, gtq, LANE), F32),
                        pltpu.VMEM((KV_PER_LOOP, gtq, LANE), F32),
                        pltpu.VMEM((KV_PER_LOOP, gtq, DSA_HEAD_DIM), F32)],
        compiler_params=_cparams(("parallel", "arbitrary")),
        name="dsa_attention",
    )(*([proj] * nqi), tail, proj, proj, proj, proj)


def _mla_kernel(cast_in_ref, q_ref, kn_ref, kpe_ref, v_ref, o_ref, cast_out_ref, m_ref, l_ref,
                acc_ref, *, tq, hb):
    tk = tq
    i = pl.program_id(2)
    cast_out_ref[...] = cast_in_ref[...].astype(BF16)
    m_ref[...] = jnp.full_like(m_ref, M_INIT)
    l_ref[...] = jnp.zeros_like(l_ref)
    acc_ref[...] = jnp.zeros_like(acc_ref)
    causal = (lax.broadcasted_iota(jnp.int32, (tq, tk), 1)
              <= lax.broadcasted_iota(jnp.int32, (tq, tk), 0))

    def step(c, masked):
        r0 = pl.multiple_of(c * tk, tk)
        kpe = kpe_ref[pl.ds(r0, tk), :]
        for h in range(hb):
            cols = slice(h * MLA_NOPE_DIM, (h + 1) * MLA_NOPE_DIM)
            kf = jnp.concatenate([kn_ref[pl.ds(r0, tk), cols], kpe], axis=1)
            s = lax.dot_general(q_ref[:, h * MLA_QK_PAD:(h + 1) * MLA_QK_PAD], kf,
                                (((1,), (1,)), ((), ())), preferred_element_type=F32)
            if masked:
                s = jnp.where(causal, s, -jnp.inf)
            _flash_update(s, v_ref[pl.ds(r0, tk), cols], m_ref, l_ref, acc_ref, h)

    def body(c, _):
        step(c, False)
        return 0

    lax.fori_loop(0, i, body, 0)
    step(i, True)
    for h in range(hb):
        o_ref[:, h * MLA_V_DIM:(h + 1) * MLA_V_DIM] = (acc_ref[h] / l_ref[h]).astype(o_ref.dtype)


def mla_attention(q, kv, tail, batch, seq, cast, *, tq=512, hb=4):
    nq = seq // tq
    ng = MLA_HEADS // hb
    cast_in, cast_out, cast_shape = _cast_specs(cast, (batch, ng, nq))
    return pl.pallas_call(
        functools.partial(_mla_kernel, tq=tq, hb=hb),
        grid=(batch, ng, nq),
        in_specs=[cast_in,
                  pl.BlockSpec((tq, hb * MLA_QK_PAD), lambda b, g, i: (b * nq + i, g)),
                  pl.BlockSpec((seq, hb * MLA_NOPE_DIM), lambda b, g, i: (b, g)),
                  pl.BlockSpec((seq, LANE), lambda b, g, i: (b, _TAIL["kr"][0] // LANE)),
                  pl.BlockSpec((seq, hb * MLA_V_DIM), lambda b, g, i: (b, ng + g))],
        out_specs=[pl.BlockSpec((tq, hb * MLA_V_DIM), lambda b, g, i: (b * nq + i, g)), cast_out],
        out_shape=[jax.ShapeDtypeStruct((batch * seq, MLA_OUT), BF16), cast_shape],
        scratch_shapes=[pltpu.VMEM((hb, tq, LANE), F32), pltpu.VMEM((hb, tq, LANE), F32),
                        pltpu.VMEM((hb, tq, MLA_V_DIM), F32)],
        compiler_params=_cparams(("arbitrary", "arbitrary", "arbitrary")),
        name="mla_attention",
    )(cast[0], q, kv, tail, kv)


def _tail_w_in(wt):
    parts, pos = [], 0
    for name in sorted(_TAIL, key=lambda s: _TAIL[s][0]):
        off, width = _TAIL[name]
        assert off == pos
        a = wt[_NAT[name][0]:_NAT[name][0] + _NAT[name][1]]
        parts.append(a)
        if a.shape[0] < width:
            parts.append(jnp.zeros((width - a.shape[0], wt.shape[1]), wt.dtype))
        pos += width
    assert pos == TAIL_WIDTH
    return jnp.concatenate(parts, axis=0)


def _relayout_w_uq(w):
    dep, r = w.shape[:2]
    a = w.reshape(dep, r, MLA_HEADS, MLA_NOPE_DIM + MLA_ROPE_DIM)
    a = jnp.pad(a, ((0, 0), (0, 0), (0, 0), (0, MLA_QK_PAD - a.shape[3])))
    return a.reshape(dep, r, MLA_HEADS * MLA_QK_PAD).astype(BF16)


def _relayout_w_ukv(w):
    dep, r = w.shape[:2]
    a = w.reshape(dep, r, MLA_HEADS, 2, MLA_NOPE_DIM).transpose(0, 1, 3, 2, 4)
    return a.reshape(dep, r, 2 * MLA_HEADS * MLA_NOPE_DIM).astype(BF16)


def _main_tile_modes():
    modes = [(1.0, None)] * (MAIN_WIDTH // LANE)

    def fill(name, mode):
        off, width = _NAT[name]
        for t in range(off // LANE, (off + width) // LANE):
            modes[t] = mode

    fill("dq", (DSA_HEAD_DIM ** -0.5 * LOG2E, 0))
    fill("dk", (1.0, 0))
    fill("iq", (IDX_HEAD_DIM ** -0.5, 0))
    fill("ik", (1.0, 0))
    return modes


def _tail_tile_modes():
    modes = [(1.0, None)] * (TAIL_WIDTH // LANE)
    modes[_TAIL["kr"][0] // LANE] = (1.0, 0)
    return modes


def kernel(x, attn_norm, w_in, mla_q_norm, mla_kv_norm, w_uq, w_ukv, w_o, mlp_norm, w_up,
           w_down, final_norm):
    batch, seq, d_model = x.shape
    depth = w_in.shape[0]
    m = batch * seq

    ret_tables = _retention_tables(seq)
    part_tab = _rope_tables(seq, PARTIAL_ROPE_THETA, PARTIAL_ROT_DIM)
    mla_tab = _rope_tables(seq, MLA_ROPE_THETA, MLA_ROPE_DIM)
    mla_scale = (MLA_NOPE_DIM + MLA_ROPE_DIM) ** -0.5 * LOG2E
    uq_modes = [(mla_scale, None), (mla_scale, 0)] * MLA_HEADS

    w_uq_b, w_ukv_b = _relayout_w_uq(w_uq), _relayout_w_ukv(w_ukv)
    w_in_t = jnp.swapaxes(w_in, 1, 2)

    h = x.reshape(m, d_model)
    xin, w_in_l = rmsnorm(h, attn_norm[0], BF16, cast=(w_in_t, 0))
    for l in range(depth):
        pre = None if l == 0 else attn_norm[l]
        proj = matmul(xin, w_in_l, BF16, name="mm_in", w_t=True, n_out=MAIN_WIDTH, bn=MAIN_BN,
                      prenorm=pre, tile_modes=_main_tile_modes(), rope_tabs=(part_tab,), seq=seq)
        tail = matmul(xin, _tail_w_in(w_in_l), BF16, name="mm_in_tail", w_t=True, bn=TAIL_WIDTH,
                      prenorm=pre, tile_modes=_tail_tile_modes(), rope_tabs=(mla_tab,), seq=seq)

        ret, w_o_l = retention(proj, batch, seq, ret_tables, cast=(w_o, l))
        dsa = dsa_attention(proj, tail, batch, seq)

        q_full = matmul(tail, w_uq_b, BF16, layer=l, name="mm_uq", x_cols=_TAIL["cq"],
                        prenorm=mla_q_norm[l], tile_modes=uq_modes, rope_tabs=(mla_tab,), seq=seq)
        kv_up = matmul(tail, w_ukv_b, BF16, layer=l, name="mm_ukv", x_cols=_TAIL["ckv"],
                       prenorm=mla_kv_norm[l])
        mla, w_up_l = mla_attention(q_full, kv_up, tail, batch, seq, cast=(w_up, l))

        h, hb = matmul((ret, dsa, mla), w_o_l, F32, res=h, name="mm_o", emit_bf16=True)
        up, w_down_l = matmul(hb, w_up_l, BF16, act="relu2", name="mm_up", prenorm=mlp_norm[l],
                              cast=(w_down, l))
        if l + 1 < depth:
            h, xin, w_in_l = matmul(up, w_down_l, F32, res=h, bk=DOWN_BK, name="mm_down",
                                    emit_bf16=True, cast=(w_in_t, l + 1))
        else:
            h = matmul(up, w_down_l, F32, res=h, bk=DOWN_BK, name="mm_down")
    out = rmsnorm(h, final_norm, x.dtype)
    return out.reshape(batch, seq, d_model)
```

```python
import functools
import math

import numpy as np
import jax
import jax.numpy as jnp
from jax import lax
from jax.experimental import pallas as pl
from jax.experimental.pallas import tpu as pltpu

F32 = jnp.float32
BF16 = jnp.bfloat16

RET_HEADS = 8
RET_QK_DIM = 64
RET_V_DIM = 128
RET_CHUNK = 128
RET_ROPE_THETA = 10000.0
DSA_HEADS = 12
DSA_KV_HEADS = 4
DSA_HEAD_DIM = 128
IDX_HEADS = 32
IDX_HEAD_DIM = 128
DSA_TOPK = 256
PARTIAL_ROT_DIM = 32
PARTIAL_ROPE_THETA = 500000.0
MLA_HEADS = 12
MLA_Q_LORA = 768
MLA_KV_LORA = 256
MLA_NOPE_DIM = 128
MLA_ROPE_DIM = 64
MLA_V_DIM = 128
MLA_ROPE_THETA = 10000.0
NORM_EPS = 1e-6

LANE = 128
VMEM_LIMIT = 60 * 1024 * 1024

RET_OUT = RET_HEADS * RET_V_DIM
DSA_OUT = DSA_HEADS * DSA_HEAD_DIM
MLA_OUT = MLA_HEADS * MLA_V_DIM
MLA_QK_PAD = 256

_NAT = {}
_off = 0
for _name, _w in (("rq", 512), ("rk", 512), ("rv", 1024), ("rg", 1024), ("dq", 1536),
                  ("dk", 512), ("dv", 512), ("iq", 4096), ("ik", 128), ("iw", 32),
                  ("cq", 768), ("ckv", 256), ("kr", 64)):
    _NAT[_name] = (_off, _w)
    _off += _w
IN_WIDTH = _off
MAIN_WIDTH = 10240
MAIN_BN = 1024
_TAIL = {"cq": (0, 768), "ckv": (768, 256), "iw": (1024, 128), "kr": (1152, 128)}
TAIL_WIDTH = 1280
QI_BLOCK = 512
DOWN_BK = 2048
NORM_ROWS = 64
NORM_ONCE_MIN = 2 ** 21


def _cparams(sem):
    return pltpu.CompilerParams(dimension_semantics=sem, vmem_limit_bytes=VMEM_LIMIT)


def _rmsnorm_kernel(x_ref, g_ref, *rest):
    o_ref = rest[-1] if len(rest) == 1 else rest[1]
    if len(rest) == 3:
        rest[2][...] = rest[0][...].astype(BF16)
    x = x_ref[...].astype(F32)
    ms = jnp.mean(x * x, axis=-1, keepdims=True)
    o_ref[...] = (x * lax.rsqrt(ms + NORM_EPS) * g_ref[...]).astype(o_ref.dtype)


def rmsnorm(x, g, out_dtype, *, bm=256, cast=None):
    m, width = x.shape
    assert m % bm == 0
    in_specs = [pl.BlockSpec((bm, width), lambda i: (i, 0)),
                pl.BlockSpec((1, width), lambda i: (0, 0))]
    out_specs = pl.BlockSpec((bm, width), lambda i: (i, 0))
    out_shape = jax.ShapeDtypeStruct((m, width), out_dtype)
    args = [x, g.reshape(1, width).astype(F32)]
    if cast is not None:
        cast_in, cast_out, cast_shape = _cast_specs(cast, (m // bm,))
        in_specs.append(cast_in)
        args.append(cast[0])
        out_specs, out_shape = [out_specs, cast_out], [out_shape, cast_shape]
    return pl.pallas_call(
        _rmsnorm_kernel,
        grid=(m // bm,),
        in_specs=in_specs,
        out_specs=out_specs,
        out_shape=out_shape,
        compiler_params=_cparams(("arbitrary",)),
        name="rmsnorm",
    )(*args)


def _mm_kernel(*refs, nx, act, has_res, has_norm, norm_once, nk, groups, halves, ntab, has_cast,
               emit_bf16, w_t):
    refs = list(refs)
    take = lambda n=1: [refs.pop(0) for _ in range(n)]
    x_refs = take(nx)
    (w_ref,) = take()
    r_ref = take()[0] if has_res else None
    g_ref = take()[0] if has_norm else None
    tab_refs = take(3 * ntab)
    cast_in_ref = take()[0] if has_cast else None
    (o_ref,) = take()
    ob_ref = take()[0] if emit_bf16 else None
    if has_cast:
        take()[0][...] = cast_in_ref[...].astype(BF16)
    acc_ref = take()[0] if nk > 1 else None
    xn_ref = take()[0] if norm_once else None

    def store_tiles(acc, pattern):
        for t, (scale, ridx) in enumerate(pattern):
            x = acc[:, t * LANE:(t + 1) * LANE]
            if ridx is not None:
                a, bp, bm = (tab_refs[3 * ridx + k][...] for k in range(3))
                half = halves[ridx]
                x = x * a + pltpu.roll(x, half, 1) * bp + pltpu.roll(x, LANE - half, 1) * bm
            if scale != 1.0:
                x = x * scale
            o_ref[:, t * LANE:(t + 1) * LANE] = x.astype(o_ref.dtype)

    def epilogue(acc):
        if act == "relu2":
            acc = jnp.square(jnp.maximum(acc, 0.0))
        if has_res:
            acc = acc + r_ref[...]
        if emit_bf16:
            ob_ref[...] = acc.astype(BF16)
        if groups is None:
            o_ref[...] = acc.astype(o_ref.dtype)
        elif len(groups) == 1:
            store_tiles(acc, groups[0][0])
        else:
            j = pl.program_id(1)
            for pattern, js in groups:
                cond = functools.reduce(jnp.logical_or, [j == jj for jj in js])
                pl.when(cond)(functools.partial(store_tiles, acc, pattern))

    def normalised(x):
        xf = x.astype(F32)
        ms = jnp.mean(xf * xf, axis=-1, keepdims=True)
        return (xf * lax.rsqrt(ms + NORM_EPS) * g_ref[...]).astype(BF16)

    if norm_once:
        @pl.when(pl.program_id(1) == 0)
        def _():
            def rows_body(r, _):
                rows = pl.ds(pl.multiple_of(r * NORM_ROWS, NORM_ROWS), NORM_ROWS)
                xn_ref[rows, :] = normalised(x_refs[0][rows, :])
                return 0
            lax.fori_loop(0, xn_ref.shape[0] // NORM_ROWS, rows_body, 0)

    if nk > 1:
        @pl.when(pl.program_id(2) == 0)
        def _():
            acc_ref[...] = jnp.zeros_like(acc_ref)

    part, k0 = None, 0
    for x_ref in x_refs:
        kw = x_ref.shape[1]
        if norm_once:
            x = xn_ref[...]
        elif has_norm:
            x = normalised(x_ref[...])
        else:
            x = x_ref[...]
        if w_t:
            d = lax.dot_general(x, w_ref[...], (((1,), (1,)), ((), ())),
                                preferred_element_type=F32)
        else:
            d = jnp.dot(x, w_ref[k0:k0 + kw, :], preferred_element_type=F32)
        part = d if part is None else part + d
        k0 += kw
    if nk == 1:
        epilogue(part)
    else:
        acc_ref[...] += part

        @pl.when(pl.program_id(2) == nk - 1)
        def _():
            epilogue(acc_ref[...])


def matmul(x, w, out_dtype, *, name, layer=None, act=None, res=None, bm=1024, bn=1024, bk=4096,
           n_out=None, tile_modes=None, rope_tabs=(), seq=None, cast=None, x_cols=None,
           prenorm=None, w_t=False, emit_bf16=False):
    xs = x if isinstance(x, (tuple, list)) else (x,)
    m = xs[0].shape[0]
    kdim = sum(a.shape[1] for a in xs) if x_cols is None else x_cols[1]
    assert kdim == w.shape[-1 if w_t else -2]
    n = w.shape[-2 if w_t else -1] if n_out is None else n_out
    bm, bn, bk = min(bm, m), min(bn, n), min(bk, kdim)
    assert m % bm == 0 and n % bn == 0 and kdim % bk == 0
    nk = kdim // bk
    assert len(xs) == 1 or nk == 1
    if x_cols is not None:
        assert len(xs) == 1 and nk == 1 and x_cols[0] % kdim == 0
        xcb = x_cols[0] // kdim
        in_specs = [pl.BlockSpec((bm, kdim), lambda i, j, k: (i, xcb))]
    elif len(xs) == 1:
        in_specs = [pl.BlockSpec((bm, bk), lambda i, j, k: (i, k))]
    else:
        in_specs = [pl.BlockSpec((bm, a.shape[1]), lambda i, j, k: (i, 0)) for a in xs]
    if w_t:
        assert layer is None and len(xs) == 1
        in_specs.append(pl.BlockSpec((bn, bk), lambda i, j, k: (j, k)))
    elif layer is None:
        in_specs.append(pl.BlockSpec((bk, bn), lambda i, j, k: (k, j)))
    else:
        in_specs.append(pl.BlockSpec((None, bk, bn), lambda i, j, k: (layer, k, j)))
    args = [*xs, w]
    if res is not None:
        in_specs.append(pl.BlockSpec((bm, bn), lambda i, j, k: (i, j)))
        args.append(res)
    if prenorm is not None:
        assert nk == 1 and len(xs) == 1
        in_specs.append(pl.BlockSpec((1, kdim), lambda i, j, k: (0, 0)))
        args.append(prenorm.reshape(1, kdim).astype(F32))
    groups = None
    if tile_modes is not None:
        assert len(tile_modes) == n // LANE and seq % bm == 0
        per_block = bn // LANE
        by_pattern = {}
        for j in range(n // bn):
            by_pattern.setdefault(tuple(tile_modes[j * per_block:(j + 1) * per_block]), []).append(j)
        groups = tuple((p, tuple(js)) for p, js in by_pattern.items())
        nseq = seq // bm
        for a, bp, bmt, _ in rope_tabs:
            in_specs += [pl.BlockSpec((bm, LANE), lambda i, j, k: (i % nseq, 0))] * 3
            args += [a, bp, bmt]
    tile_spec = pl.BlockSpec((bm, bn), lambda i, j, k: (i, j))
    out_specs = [tile_spec]
    out_shape = [jax.ShapeDtypeStruct((m, n), out_dtype)]
    if emit_bf16:
        out_specs.append(tile_spec)
        out_shape.append(jax.ShapeDtypeStruct((m, n), BF16))
    norm_once = prenorm is not None and n // bn > 1 and bm * kdim >= NORM_ONCE_MIN
    ordered = cast is not None or norm_once
    sem = ("arbitrary",) * 3 if ordered else ("parallel", "parallel", "arbitrary")
    if cast is not None:
        spec_in, spec_out, shape_out = _cast_specs(cast, (m // bm, n // bn, nk))
        in_specs.append(spec_in)
        args.append(cast[0])
        out_specs.append(spec_out)
        out_shape.append(shape_out)
    scratch = [pltpu.VMEM((bm, bn), F32)] if nk > 1 else []
    if norm_once:
        scratch.append(pltpu.VMEM((bm, kdim), BF16))
    outs = pl.pallas_call(
        functools.partial(_mm_kernel, nx=len(xs), act=act, has_res=res is not None,
                          has_norm=prenorm is not None, norm_once=norm_once, nk=nk,
                          groups=groups, halves=tuple(t[3] for t in rope_tabs),
                          ntab=len(rope_tabs), has_cast=cast is not None, emit_bf16=emit_bf16,
                          w_t=w_t),
        grid=(m // bm, n // bn, nk),
        in_specs=in_specs,
        out_specs=out_specs,
        out_shape=out_shape,
        scratch_shapes=scratch,
        compiler_params=_cparams(sem),
        name=name,
    )(*args)
    return outs[0] if len(outs) == 1 else tuple(outs)


def _cast_specs(cast, grid):
    src, layer = cast
    _, rows, cols = src.shape
    nsteps = math.prod(grid)
    rb = 16 * max(1, -(-rows // (16 * nsteps)))
    while rows % rb:
        rb += 16
    nb = rows // rb

    def step(*ids):
        s = 0
        for g, i in zip(grid, ids):
            s = s * g + i
        return jnp.minimum(s, nb - 1)

    spec_in = pl.BlockSpec((None, rb, cols), lambda *ids: (layer, step(*ids), 0))
    spec_out = pl.BlockSpec((rb, cols), lambda *ids: (step(*ids), 0))
    return spec_in, spec_out, jax.ShapeDtypeStruct((rows, cols), BF16)


def _rope_tables(seq, theta, rot_dim):
    half = rot_dim // 2
    inv = jnp.exp(-math.log(theta) * jnp.arange(half, dtype=F32) * (2.0 / rot_dim))
    ang = jnp.arange(seq, dtype=F32)[:, None] * inv[None, :]
    cos, sin = jnp.cos(ang), jnp.sin(ang)
    pad = LANE - rot_dim
    a = jnp.concatenate([cos, cos, jnp.ones((seq, pad), F32)], axis=1)
    bp = jnp.concatenate([jnp.zeros((seq, half), F32), sin, jnp.zeros((seq, pad), F32)], axis=1)
    bm = jnp.concatenate([-sin, jnp.zeros((seq, half + pad), F32)], axis=1)
    return a, bp, bm, half


def _retention_kernel(cast_in_ref, q_ref, k_ref, v_ref, g_ref, cos_ref, sp_ref, sm_ref, hmask_ref,
                      idec_ref, qdec_ref, kdec_ref, cdec_ref, bmask_ref, o_ref, cast_out_ref,
                      state_ref):
    half = RET_QK_DIM // 2
    cast_out_ref[...] = cast_in_ref[...].astype(BF16)

    @pl.when(pl.program_id(0) == 0)
    def _():
        state_ref[...] = jnp.zeros_like(state_ref)

    cos, sp, sm = cos_ref[...], sp_ref[...], sm_ref[...]

    def rot(x):
        tiles = []
        for t in range(x.shape[1] // LANE):
            xt = x[:, t * LANE:(t + 1) * LANE]
            tiles.append(xt * cos + pltpu.roll(xt, half, 1) * sp
                         + pltpu.roll(xt, LANE - half, 1) * sm)
        return jnp.concatenate(tiles, axis=1)

    for b in range(q_ref.shape[0]):
        q = rot(q_ref[b].astype(F32))
        k = rot(k_ref[b].astype(F32)) * (RET_QK_DIM ** -0.5)
        v = v_ref[b].astype(F32)
        kt = k.T.astype(BF16)
        state = state_ref[b]
        cross = jnp.dot(q.astype(BF16), state.astype(BF16),
                        preferred_element_type=F32) * qdec_ref[...]
        for h in range(RET_HEADS):
            cols = slice(h * RET_V_DIM, (h + 1) * RET_V_DIM)
            qm = (q * hmask_ref[h:h + 1, :]).astype(BF16)
            scores = jnp.dot(qm, kt, preferred_element_type=F32) * idec_ref[h]
            inner = jnp.dot(scores.astype(BF16), v[:, cols].astype(BF16),
                            preferred_element_type=F32)
            o = inner + cross[:, cols]
            o = o * lax.rsqrt(jnp.mean(o * o, axis=-1, keepdims=True) + NORM_EPS)
            gate = g_ref[b, :, cols].astype(F32)
            o_ref[b, :, cols] = (gate * jax.nn.sigmoid(gate) * o).astype(o_ref.dtype)
        vd = (v * kdec_ref[...]).astype(BF16)
        upd = jnp.dot(kt, vd, preferred_element_type=F32)
        state_ref[b] = state * cdec_ref[...] + upd * bmask_ref[...]


def _retention_tables(seq):
    h, c, dv, dk = RET_HEADS, RET_CHUNK, RET_V_DIM, RET_QK_DIM
    half = dk // 2
    inv = jnp.exp(-math.log(RET_ROPE_THETA) * jnp.arange(half, dtype=F32) * (2.0 / dk))
    ang = jnp.arange(seq, dtype=F32)[:, None] * inv[None, :]
    cos1, sin1, zero = jnp.cos(ang), jnp.sin(ang), jnp.zeros((seq, half), F32)
    reps = LANE // dk
    cos = jnp.tile(jnp.concatenate([cos1, cos1], axis=1), (1, reps))
    sp = jnp.tile(jnp.concatenate([zero, sin1], axis=1), (1, reps))
    sm = jnp.tile(jnp.concatenate([-sin1, zero], axis=1), (1, reps))
    log_gamma = jnp.log1p(-jnp.exp2(-5.0 - jnp.arange(h, dtype=F32)))
    idx = jnp.arange(c, dtype=F32)
    rel = idx[:, None] - idx[None, :]
    idec = jnp.where(rel[None] >= 0,
                     jnp.exp(jnp.maximum(rel, 0.0)[None] * log_gamma[:, None, None]), 0.0)
    qdec = jnp.exp((idx + 1.0)[:, None] * log_gamma[None, :])
    kdec = jnp.exp((c - 1.0 - idx)[:, None] * log_gamma[None, :])
    cdec = jnp.exp(c * log_gamma)[None, :]
    rep = lambda a: jnp.repeat(a, dv, axis=1)
    lane_head = np.arange(h * dk) // dk
    hmask = jnp.asarray(lane_head[None, :] == np.arange(h)[:, None], F32)
    bmask = jnp.asarray(lane_head[:, None] == (np.arange(h * dv) // dv)[None, :], F32)
    return cos, sp, sm, hmask, idec, rep(qdec), rep(kdec), rep(cdec), bmask


def retention(proj, batch, seq, tables, cast):
    c = RET_CHUNK
    nch = seq // c
    qw = RET_HEADS * RET_QK_DIM
    col = lambda name: _NAT[name][0] // _NAT[name][1]
    const2 = lambda n: (0, 0)
    cos, sp, sm, hmask, idec, qdec, kdec, cdec, bmask = tables
    tab = pl.BlockSpec((c, LANE), lambda n: (n, 0))
    cast_in, cast_out, cast_shape = _cast_specs(cast, (nch,))
    proj3 = proj.reshape(batch, seq, proj.shape[1])
    seg = lambda name, width: pl.BlockSpec((batch, c, width), lambda n: (0, n, col(name)))
    out, w_cast = pl.pallas_call(
        _retention_kernel,
        grid=(nch,),
        in_specs=[cast_in, seg("rq", qw), seg("rk", qw), seg("rv", RET_OUT), seg("rg", RET_OUT),
                  tab, tab, tab,
                  pl.BlockSpec(hmask.shape, const2),
                  pl.BlockSpec(idec.shape, lambda n: (0, 0, 0)),
                  pl.BlockSpec(qdec.shape, const2),
                  pl.BlockSpec(kdec.shape, const2),
                  pl.BlockSpec(cdec.shape, const2),
                  pl.BlockSpec(bmask.shape, const2)],
        out_specs=[pl.BlockSpec((batch, c, RET_OUT), lambda n: (0, n, 0)), cast_out],
        out_shape=[jax.ShapeDtypeStruct((batch, seq, RET_OUT), BF16), cast_shape],
        scratch_shapes=[pltpu.VMEM((batch, qw, RET_OUT), F32)],
        compiler_params=_cparams(("arbitrary",)),
        name="retention",
    )(cast[0], proj3, proj3, proj3, proj3, cos, sp, sm, hmask, idec, qdec, kdec, cdec, bmask)
    return out.reshape(batch * seq, RET_OUT), w_cast


LOG2E = math.log2(math.e)
M_INIT = float(np.finfo(np.float32).min)


def _flash_update(s, v_c, m_ref, l_ref, acc_ref, h):
    reps = s.shape[1] // LANE

    def fold(x, op):
        tiles = [x[:, r * LANE:(r + 1) * LANE] for r in range(reps)]
        while len(tiles) > 1:
            tiles = [op(a, b) for a, b in zip(tiles[::2], tiles[1::2])]
        return tiles[0]

    m_prev = m_ref[h]
    m_new = jnp.maximum(m_prev, jnp.max(fold(s, jnp.maximum), axis=1, keepdims=True))
    p = jnp.exp2(s - jnp.tile(m_new, (1, reps)))
    alpha = jnp.exp2(m_prev - m_new)
    l_ref[h] = alpha * l_ref[h] + jnp.sum(fold(p, jnp.add), axis=1, keepdims=True)
    m_ref[h] = m_new
    acc_ref[h] = acc_ref[h] * alpha + jnp.dot(p.astype(BF16), v_c, preferred_element_type=F32)


INT_MIN = -2 ** 31
ROW_BLOCK = 64
PASSES_PER_CHECK = 4
COUNT_ROWS = 32
KV_PER_LOOP = 4


def _dsa_kernel(*refs, tq, topk, nqi):
    qi_refs = refs[:nqi]
    (w_ref, kidx_ref, q_ref, k_ref, v_ref, o_ref,
     wb_ref, lg_ref, key_ref, bias_ref, qs_ref, m_ref, l_ref, acc_ref) = refs[nqi:]
    tk = tq
    i = pl.program_id(1)
    nchunks = i + 1
    reps = tk // LANE
    group = DSA_HEADS // DSA_KV_HEADS
    d = DSA_HEAD_DIM
    heads_per_ref = IDX_HEADS // nqi

    w = w_ref[...].astype(F32) * (IDX_HEADS ** -0.5)
    for h in range(IDX_HEADS):
        wb_ref[h] = jnp.broadcast_to(w[:, h:h + 1], (tq, LANE))

    def score_chunk(c, diagonal):
        r0 = pl.multiple_of(c * tk, tk)
        kc = kidx_ref[pl.ds(r0, tk), :]
        for h in range(IDX_HEADS):
            r, t = divmod(h, heads_per_ref)
            lg_ref[h] = lax.dot_general(qi_refs[r][:, t * LANE:(t + 1) * LANE], kc,
                                        (((1,), (1,)), ((), ())), preferred_element_type=F32)
        for rb in range(tq // ROW_BLOCK):
            q0 = rb * ROW_BLOCK
            rows = slice(q0, q0 + ROW_BLOCK)
            acc = jnp.zeros((ROW_BLOCK, tk), F32)
            for h in range(IDX_HEADS):
                acc = acc + jnp.maximum(lg_ref[h, rows, :], 0.0) * jnp.tile(wb_ref[h, rows, :],
                                                                            (1, reps))
            bits = pltpu.bitcast(acc, jnp.int32)
            key = bits ^ (lax.shift_right_arithmetic(bits, 31) & 0x7FFFFFFF)
            if diagonal:
                kpos = lax.broadcasted_iota(jnp.int32, (ROW_BLOCK, tk), 1)
                qpos = q0 + lax.broadcasted_iota(jnp.int32, (ROW_BLOCK, tk), 0)
                key = jnp.where(kpos <= qpos, key, INT_MIN)
            key_ref[c, rows, :] = key
        bias_ref[c] = pltpu.bitcast(key_ref[c].T, F32)

    def score_body(c, _):
        score_chunk(c, False)
        return 0

    lax.fori_loop(0, i, score_body, 0)
    score_chunk(i, True)

    def count_ge(cand):
        def body(c, cnt):
            ge = (pltpu.bitcast(bias_ref[c], jnp.int32) >= cand).astype(jnp.int32)
            return cnt + jnp.sum(ge.reshape(tk // COUNT_ROWS, COUNT_ROWS, tq), axis=0)
        cnt = lax.fori_loop(0, nchunks, body, jnp.zeros((COUNT_ROWS, tq), jnp.int32))
        return jnp.sum(cnt, axis=0, keepdims=True)

    def bit_cond(state):
        b, _, cnt = state
        return jnp.logical_and(b < 32, jnp.max(jnp.abs(cnt - topk)) > 0)

    def bit_body(state):
        b, t, cnt = state
        for u in range(PASSES_PER_CHECK):
            cand = t ^ lax.shift_left(jnp.int32(1), 31 - (b + u))
            n = count_ge(cand)
            take = n >= topk
            t, cnt = jnp.where(take, cand, t), jnp.where(take, n, cnt)
        return b + PASSES_PER_CHECK, t, cnt

    t0 = jnp.full((1, tq), INT_MIN, jnp.int32)
    c0 = jnp.full((1, tq), nchunks * tk, jnp.int32)
    _, thr_row, _ = lax.while_loop(bit_cond, bit_body, (jnp.int32(0), t0, c0))
    thr_row = jnp.maximum(thr_row, INT_MIN + 1)
    thr = jnp.broadcast_to(thr_row, (LANE, tq)).T
    thr = jnp.tile(thr, (1, reps))

    def bias_body(c, _):
        bias_ref[c] = jnp.where(key_ref[c] >= thr, 0.0, -jnp.inf)
        return 0

    lax.fori_loop(0, nchunks, bias_body, 0)

    @pl.when(nchunks < key_ref.shape[0])
    def _():
        bias_ref[nchunks] = jnp.full((tq, tk), -jnp.inf, F32)

    hpl = KV_PER_LOOP * group
    for g0 in range(0, DSA_KV_HEADS, KV_PER_LOOP):
        for hh in range(hpl):
            h = g0 * group + hh
            qs_ref[hh * tq:(hh + 1) * tq, :] = q_ref[:, h * d:(h + 1) * d]
        m_ref[...] = jnp.full_like(m_ref, M_INIT)
        l_ref[...] = jnp.zeros_like(l_ref)
        acc_ref[...] = jnp.zeros_like(acc_ref)

        def att_body(c2, _, g0=g0):
            r0 = pl.multiple_of(c2 * 2 * tk, 2 * tk)
            bias = jnp.concatenate([bias_ref[2 * c2], bias_ref[2 * c2 + 1]], axis=1)
            for gg in range(KV_PER_LOOP):
                g = g0 + gg
                kc = k_ref[pl.ds(r0, 2 * tk), g * d:(g + 1) * d]
                vc = v_ref[pl.ds(r0, 2 * tk), g * d:(g + 1) * d]
                s_all = lax.dot_general(qs_ref[gg * group * tq:(gg + 1) * group * tq, :], kc,
                                        (((1,), (1,)), ((), ())), preferred_element_type=F32)
                _flash_update(s_all + jnp.tile(bias, (group, 1)), vc, m_ref, l_ref, acc_ref, gg)
            return 0

        lax.fori_loop(0, (nchunks + 1) // 2, att_body, 0)
        for hh in range(hpl):
            h = g0 * group + hh
            gg, rows = hh // group, slice((hh % group) * tq, (hh % group + 1) * tq)
            o_ref[:, h * d:(h + 1) * d] = (acc_ref[gg, rows, :]
                                           / l_ref[gg, rows, :]).astype(o_ref.dtype)


def dsa_attention(proj, tail, batch, seq, *, tq=256):
    nq = seq // tq
    assert nq % 2 == 0
    topk = min(DSA_TOPK, seq // 4)
    hpl = KV_PER_LOOP * DSA_HEADS // DSA_KV_HEADS
    gtq = DSA_HEADS // DSA_KV_HEADS * tq
    kvw = DSA_KV_HEADS * DSA_HEAD_DIM
    iqw = IDX_HEADS * IDX_HEAD_DIM
    nqi = iqw // QI_BLOCK
    col = lambda name, width: _NAT[name][0] // width
    qi_specs = [pl.BlockSpec((tq, QI_BLOCK), lambda b, i, r=r: (b * nq + i, col("iq", QI_BLOCK) + r))
                for r in range(nqi)]
    return pl.pallas_call(
        functools.partial(_dsa_kernel, tq=tq, topk=topk, nqi=nqi),
        grid=(batch, nq),
        in_specs=qi_specs + [
            pl.BlockSpec((tq, LANE), lambda b, i: (b * nq + i, _TAIL["iw"][0] // LANE)),
            pl.BlockSpec((seq, IDX_HEAD_DIM), lambda b, i: (b, col("ik", IDX_HEAD_DIM))),
            pl.BlockSpec((tq, DSA_OUT), lambda b, i: (b * nq + i, col("dq", DSA_OUT))),
            pl.BlockSpec((seq, kvw), lambda b, i: (b, col("dk", kvw))),
            pl.BlockSpec((seq, kvw), lambda b, i: (b, col("dv", kvw)))],
        out_specs=pl.BlockSpec((tq, DSA_OUT), lambda b, i: (b * nq + i, 0)),
        out_shape=jax.ShapeDtypeStruct((batch * seq, DSA_OUT), BF16),
        scratch_shapes=[pltpu.VMEM((IDX_HEADS, tq, LANE), F32),
                        pltpu.VMEM((IDX_HEADS, tq, tq), F32),
                        pltpu.VMEM((nq, tq, tq), jnp.int32),
                        pltpu.VMEM((nq, tq, tq), F32),
                        pltpu.VMEM((hpl * tq, DSA_HEAD_DIM), BF16),
                        pltpu.VMEM((KV_PER_LOOP, gtq, LANE), F32),
                        pltpu.VMEM((KV_PER_LOOP, gtq, LANE), F32),
                        pltpu.VMEM((KV_PER_LOOP, gtq, DSA_HEAD_DIM), F32)],
        compiler_params=_cparams(("parallel", "arbitrary")),
        name="dsa_attention",
    )(*([proj] * nqi), tail, proj, proj, proj, proj)


def _mla_kernel(cast_in_ref, q_ref, kn_ref, kpe_ref, v_ref, o_ref, cast_out_ref, m_ref, l_ref,
                acc_ref, *, tq, hb):
    tk = tq
    i = pl.program_id(2)
    cast_out_ref[...] = cast_in_ref[...].astype(BF16)
    m_ref[...] = jnp.full_like(m_ref, M_INIT)
    l_ref[...] = jnp.zeros_like(l_ref)
    acc_ref[...] = jnp.zeros_like(acc_ref)
    causal = (lax.broadcasted_iota(jnp.int32, (tq, tk), 1)
              <= lax.broadcasted_iota(jnp.int32, (tq, tk), 0))

    def step(c, masked):
        r0 = pl.multiple_of(c * tk, tk)
        kpe = kpe_ref[pl.ds(r0, tk), :]
        for h in range(hb):
            cols = slice(h * MLA_NOPE_DIM, (h + 1) * MLA_NOPE_DIM)
            kf = jnp.concatenate([kn_ref[pl.ds(r0, tk), cols], kpe], axis=1)
            s = lax.dot_general(q_ref[:, h * MLA_QK_PAD:(h + 1) * MLA_QK_PAD], kf,
                                (((1,), (1,)), ((), ())), preferred_element_type=F32)
            if masked:
                s = jnp.where(causal, s, -jnp.inf)
            _flash_update(s, v_ref[pl.ds(r0, tk), cols], m_ref, l_ref, acc_ref, h)

    def body(c, _):
        step(c, False)
        return 0

    lax.fori_loop(0, i, body, 0)
    step(i, True)
    for h in range(hb):
        o_ref[:, h * MLA_V_DIM:(h + 1) * MLA_V_DIM] = (acc_ref[h] / l_ref[h]).astype(o_ref.dtype)


def mla_attention(q, kv, tail, batch, seq, cast, *, tq=512, hb=4):
    nq = seq // tq
    ng = MLA_HEADS // hb
    cast_in, cast_out, cast_shape = _cast_specs(cast, (batch, ng, nq))
    return pl.pallas_call(
        functools.partial(_mla_kernel, tq=tq, hb=hb),
        grid=(batch, ng, nq),
        in_specs=[cast_in,
                  pl.BlockSpec((tq, hb * MLA_QK_PAD), lambda b, g, i: (b * nq + i, g)),
                  pl.BlockSpec((seq, hb * MLA_NOPE_DIM), lambda b, g, i: (b, g)),
                  pl.BlockSpec((seq, LANE), lambda b, g, i: (b, _TAIL["kr"][0] // LANE)),
                  pl.BlockSpec((seq, hb * MLA_V_DIM), lambda b, g, i: (b, ng + g))],
        out_specs=[pl.BlockSpec((tq, hb * MLA_V_DIM), lambda b, g, i: (b * nq + i, g)), cast_out],
        out_shape=[jax.ShapeDtypeStruct((batch * seq, MLA_OUT), BF16), cast_shape],
        scratch_shapes=[pltpu.VMEM((hb, tq, LANE), F32), pltpu.VMEM((hb, tq, LANE), F32),
                        pltpu.VMEM((hb, tq, MLA_V_DIM), F32)],
        compiler_params=_cparams(("arbitrary", "arbitrary", "arbitrary")),
        name="mla_attention",
    )(cast[0], q, kv, tail, kv)


def _tail_w_in(wt):
    parts, pos = [], 0
    for name in sorted(_TAIL, key=lambda s: _TAIL[s][0]):
        off, width = _TAIL[name]
        assert off == pos
        a = wt[_NAT[name][0]:_NAT[name][0] + _NAT[name][1]]
        parts.append(a)
        if a.shape[0] < width:
            parts.append(jnp.zeros((width - a.shape[0], wt.shape[1]), wt.dtype))
        pos += width
    assert pos == TAIL_WIDTH
    return jnp.concatenate(parts, axis=0)


def _relayout_w_uq(w):
    dep, r = w.shape[:2]
    a = w.reshape(dep, r, MLA_HEADS, MLA_NOPE_DIM + MLA_ROPE_DIM)
    a = jnp.pad(a, ((0, 0), (0, 0), (0, 0), (0, MLA_QK_PAD - a.shape[3])))
    return a.reshape(dep, r, MLA_HEADS * MLA_QK_PAD).astype(BF16)


def _relayout_w_ukv(w):
    dep, r = w.shape[:2]
    a = w.reshape(dep, r, MLA_HEADS, 2, MLA_NOPE_DIM).transpose(0, 1, 3, 2, 4)
    return a.reshape(dep, r, 2 * MLA_HEADS * MLA_NOPE_DIM).astype(BF16)


def _main_tile_modes():
    modes = [(1.0, None)] * (MAIN_WIDTH // LANE)

    def fill(name, mode):
        off, width = _NAT[name]
        for t in range(off // LANE, (off + width) // LANE):
            modes[t] = mode

    fill("dq", (DSA_HEAD_DIM ** -0.5 * LOG2E, 0))
    fill("dk", (1.0, 0))
    fill("iq", (IDX_HEAD_DIM ** -0.5, 0))
    fill("ik", (1.0, 0))
    return modes


def _tail_tile_modes():
    modes = [(1.0, None)] * (TAIL_WIDTH // LANE)
    modes[_TAIL["kr"][0] // LANE] = (1.0, 0)
    return modes


def kernel(x, attn_norm, w_in, mla_q_norm, mla_kv_norm, w_uq, w_ukv, w_o, mlp_norm, w_up,
           w_down, final_norm):
    batch, seq, d_model = x.shape
    depth = w_in.shape[0]
    m = batch * seq

    ret_tables = _retention_tables(seq)
    part_tab = _rope_tables(seq, PARTIAL_ROPE_THETA, PARTIAL_ROT_DIM)
    mla_tab = _rope_tables(seq, MLA_ROPE_THETA, MLA_ROPE_DIM)
    mla_scale = (MLA_NOPE_DIM + MLA_ROPE_DIM) ** -0.5 * LOG2E
    uq_modes = [(mla_scale, None), (mla_scale, 0)] * MLA_HEADS

    w_uq_b, w_ukv_b = _relayout_w_uq(w_uq), _relayout_w_ukv(w_ukv)
    w_in_t = jnp.swapaxes(w_in, 1, 2)

    h = x.reshape(m, d_model)
    xin, w_in_l = rmsnorm(h, attn_norm[0], BF16, cast=(w_in_t, 0))
    for l in range(depth):
        pre = None if l == 0 else attn_norm[l]
        proj = matmul(xin, w_in_l, BF16, name="mm_in", w_t=True, n_out=MAIN_WIDTH, bn=MAIN_BN,
                      prenorm=pre, tile_modes=_main_tile_modes(), rope_tabs=(part_tab,), seq=seq)
        tail = matmul(xin, _tail_w_in(w_in_l), BF16, name="mm_in_tail", w_t=True, bn=TAIL_WIDTH,
                      prenorm=pre, tile_modes=_tail_tile_modes(), rope_tabs=(mla_tab,), seq=seq)

        ret, w_o_l = retention(proj, batch, seq, ret_tables, cast=(w_o, l))
        dsa = dsa_attention(proj, tail, batch, seq)

        q_full = matmul(tail, w_uq_b, BF16, layer=l, name="mm_uq", x_cols=_TAIL["cq"],
                        prenorm=mla_q_norm[l], tile_modes=uq_modes, rope_tabs=(mla_tab,), seq=seq)
        kv_up = matmul(tail, w_ukv_b, BF16, layer=l, name="mm_ukv", x_cols=_TAIL["ckv"],
                       prenorm=mla_kv_norm[l])
        mla, w_up_l = mla_attention(q_full, kv_up, tail, batch, seq, cast=(w_up, l))

        h, hb = matmul((ret, dsa, mla), w_o_l, F32, res=h, name="mm_o", emit_bf16=True)
        up, w_down_l = matmul(hb, w_up_l, BF16, act="relu2", name="mm_up", prenorm=mlp_norm[l],
                              cast=(w_down, l))
        if l + 1 < depth:
            h, xin, w_in_l = matmul(up, w_down_l, F32, res=h, bk=DOWN_BK, name="mm_down",
                                    emit_bf16=True, cast=(w_in_t, l + 1))
        else:
            h = matmul(up, w_down_l, F32, res=h, bk=DOWN_BK, name="mm_down")
    out = rmsnorm(h, final_norm, x.dtype)
    return out.reshape(batch, seq, d_model)
```

```python
import functools
import math

import numpy as np
import jax
import jax.numpy as jnp
from jax import lax
from jax.experimental import pallas as pl
from jax.experimental.pallas import tpu as pltpu

F32 = jnp.float32
BF16 = jnp.bfloat16

RET_HEADS = 8
RET_QK_DIM = 64
RET_V_DIM = 128
RET_CHUNK = 128
RET_ROPE_THETA = 10000.0
DSA_HEADS = 12
DSA_KV_HEADS = 4
DSA_HEAD_DIM = 128
IDX_HEADS = 32
IDX_HEAD_DIM = 128
DSA_TOPK = 256
PARTIAL_ROT_DIM = 32
PARTIAL_ROPE_THETA = 500000.0
MLA_HEADS = 12
MLA_Q_LORA = 768
MLA_KV_LORA = 256
MLA_NOPE_DIM = 128
MLA_ROPE_DIM = 64
MLA_V_DIM = 128
MLA_ROPE_THETA = 10000.0
NORM_EPS = 1e-6

LANE = 128
VMEM_LIMIT = 60 * 1024 * 1024

RET_OUT = RET_HEADS * RET_V_DIM
DSA_OUT = DSA_HEADS * DSA_HEAD_DIM
MLA_OUT = MLA_HEADS * MLA_V_DIM
MLA_QK_PAD = 256

_NAT = {}
_off = 0
for _name, _w in (("rq", 512), ("rk", 512), ("rv", 1024), ("rg", 1024), ("dq", 1536),
                  ("dk", 512), ("dv", 512), ("iq", 4096), ("ik", 128), ("iw", 32),
                  ("cq", 768), ("ckv", 256), ("kr", 64)):
    _NAT[_name] = (_off, _w)
    _off += _w
IN_WIDTH = _off
MAIN_WIDTH = 10240
MAIN_BN = 1024
_TAIL = {"cq": (0, 768), "ckv": (768, 256), "iw": (1024, 128), "kr": (1152, 128)}
TAIL_WIDTH = 1280
QI_BLOCK = 512
DOWN_BK = 2048
NORM_ROWS = 64
NORM_ONCE_MIN = 2 ** 21


def _cparams(sem):
    return pltpu.CompilerParams(dimension_semantics=sem, vmem_limit_bytes=VMEM_LIMIT)


def _rmsnorm_kernel(x_ref, g_ref, *rest):
    o_ref = rest[-1] if len(rest) == 1 else rest[1]
    if len(rest) == 3:
        rest[2][...] = rest[0][...].astype(BF16)
    x = x_ref[...].astype(F32)
    ms = jnp.mean(x * x, axis=-1, keepdims=True)
    o_ref[...] = (x * lax.rsqrt(ms + NORM_EPS) * g_ref[...]).astype(o_ref.dtype)


def rmsnorm(x, g, out_dtype, *, bm=256, cast=None):
    m, width = x.shape
    assert m % bm == 0
    in_specs = [pl.BlockSpec((bm, width), lambda i: (i, 0)),
                pl.BlockSpec((1, width), lambda i: (0, 0))]
    out_specs = pl.BlockSpec((bm, width), lambda i: (i, 0))
    out_shape = jax.ShapeDtypeStruct((m, width), out_dtype)
    args = [x, g.reshape(1, width).astype(F32)]
    if cast is not None:
        cast_in, cast_out, cast_shape = _cast_specs(cast, (m // bm,))
        in_specs.append(cast_in)
        args.append(cast[0])
        out_specs, out_shape = [out_specs, cast_out], [out_shape, cast_shape]
    return pl.pallas_call(
        _rmsnorm_kernel,
        grid=(m // bm,),
        in_specs=in_specs,
        out_specs=out_specs,
        out_shape=out_shape,
        compiler_params=_cparams(("arbitrary",)),
        name="rmsnorm",
    )(*args)


def _mm_kernel(*refs, nx, act, has_res, has_norm, norm_once, nk, groups, halves, ntab, has_cast,
               emit_bf16, w_t):
    refs = list(refs)
    take = lambda n=1: [refs.pop(0) for _ in range(n)]
    x_refs = take(nx)
    (w_ref,) = take()
    r_ref = take()[0] if has_res else None
    g_ref = take()[0] if has_norm else None
    tab_refs = take(3 * ntab)
    cast_in_ref = take()[0] if has_cast else None
    (o_ref,) = take()
    ob_ref = take()[0] if emit_bf16 else None
    if has_cast:
        take()[0][...] = cast_in_ref[...].astype(BF16)
    acc_ref = take()[0] if nk > 1 else None
    xn_ref = take()[0] if norm_once else None

    def store_tiles(acc, pattern):
        for t, (scale, ridx) in enumerate(pattern):
            x = acc[:, t * LANE:(t + 1) * LANE]
            if ridx is not None:
                a, bp, bm = (tab_refs[3 * ridx + k][...] for k in range(3))
                half = halves[ridx]
                x = x * a + pltpu.roll(x, half, 1) * bp + pltpu.roll(x, LANE - half, 1) * bm
            if scale != 1.0:
                x = x * scale
            o_ref[:, t * LANE:(t + 1) * LANE] = x.astype(o_ref.dtype)

    def epilogue(acc):
        if act == "relu2":
            acc = jnp.square(jnp.maximum(acc, 0.0))
        if has_res:
            acc = acc + r_ref[...]
        if emit_bf16:
            ob_ref[...] = acc.astype(BF16)
        if groups is None:
            o_ref[...] = acc.astype(o_ref.dtype)
        elif len(groups) == 1:
            store_tiles(acc, groups[0][0])
        else:
            j = pl.program_id(1)
            for pattern, js in groups:
                cond = functools.reduce(jnp.logical_or, [j == jj for jj in js])
                pl.when(cond)(functools.partial(store_tiles, acc, pattern))

    def normalised(x):
        xf = x.astype(F32)
        ms = jnp.mean(xf * xf, axis=-1, keepdims=True)
        return (xf * lax.rsqrt(ms + NORM_EPS) * g_ref[...]).astype(BF16)

    if norm_once:
        @pl.when(pl.program_id(1) == 0)
        def _():
            def rows_body(r, _):
                rows = pl.ds(pl.multiple_of(r * NORM_ROWS, NORM_ROWS), NORM_ROWS)
                xn_ref[rows, :] = normalised(x_refs[0][rows, :])
                return 0
            lax.fori_loop(0, xn_ref.shape[0] // NORM_ROWS, rows_body, 0)

    if nk > 1:
        @pl.when(pl.program_id(2) == 0)
        def _():
            acc_ref[...] = jnp.zeros_like(acc_ref)

    part, k0 = None, 0
    for x_ref in x_refs:
        kw = x_ref.shape[1]
        if norm_once:
            x = xn_ref[...]
        elif has_norm:
            x = normalised(x_ref[...])
        else:
            x = x_ref[...]
        if w_t:
            d = lax.dot_general(x, w_ref[...], (((1,), (1,)), ((), ())),
                                preferred_element_type=F32)
        else:
            d = jnp.dot(x, w_ref[k0:k0 + kw, :], preferred_element_type=F32)
        part = d if part is None else part + d
        k0 += kw
    if nk == 1:
        epilogue(part)
    else:
        acc_ref[...] += part

        @pl.when(pl.program_id(2) == nk - 1)
        def _():
            epilogue(acc_ref[...])


def matmul(x, w, out_dtype, *, name, layer=None, act=None, res=None, bm=1024, bn=1024, bk=4096,
           n_out=None, tile_modes=None, rope_tabs=(), seq=None, cast=None, x_cols=None,
           prenorm=None, w_t=False, emit_bf16=False):
    xs = x if isinstance(x, (tuple, list)) else (x,)
    m = xs[0].shape[0]
    kdim = sum(a.shape[1] for a in xs) if x_cols is None else x_cols[1]
    assert kdim == w.shape[-1 if w_t else -2]
    n = w.shape[-2 if w_t else -1] if n_out is None else n_out
    bm, bn, bk = min(bm, m), min(bn, n), min(bk, kdim)
    assert m % bm == 0 and n % bn == 0 and kdim % bk == 0
    nk = kdim // bk
    assert len(xs) == 1 or nk == 1
    if x_cols is not None:
        assert len(xs) == 1 and nk == 1 and x_cols[0] % kdim == 0
        xcb = x_cols[0] // kdim
        in_specs = [pl.BlockSpec((bm, kdim), lambda i, j, k: (i, xcb))]
    elif len(xs) == 1:
        in_specs = [pl.BlockSpec((bm, bk), lambda i, j, k: (i, k))]
    else:
        in_specs = [pl.BlockSpec((bm, a.shape[1]), lambda i, j, k: (i, 0)) for a in xs]
    if w_t:
        assert layer is None and len(xs) == 1
        in_specs.append(pl.BlockSpec((bn, bk), lambda i, j, k: (j, k)))
    elif layer is None:
        in_specs.append(pl.BlockSpec((bk, bn), lambda i, j, k: (k, j)))
    else:
        in_specs.append(pl.BlockSpec((None, bk, bn), lambda i, j, k: (layer, k, j)))
    args = [*xs, w]
    if res is not None:
        in_specs.append(pl.BlockSpec((bm, bn), lambda i, j, k: (i, j)))
        args.append(res)
    if prenorm is not None:
        assert nk == 1 and len(xs) == 1
        in_specs.append(pl.BlockSpec((1, kdim), lambda i, j, k: (0, 0)))
        args.append(prenorm.reshape(1, kdim).astype(F32))
    groups = None
    if tile_modes is not None:
        assert len(tile_modes) == n // LANE and seq % bm == 0
        per_block = bn // LANE
        by_pattern = {}
        for j in range(n // bn):
            by_pattern.setdefault(tuple(tile_modes[j * per_block:(j + 1) * per_block]), []).append(j)
        groups = tuple((p, tuple(js)) for p, js in by_pattern.items())
        nseq = seq // bm
        for a, bp, bmt, _ in rope_tabs:
            in_specs += [pl.BlockSpec((bm, LANE), lambda i, j, k: (i % nseq, 0))] * 3
            args += [a, bp, bmt]
    tile_spec = pl.BlockSpec((bm, bn), lambda i, j, k: (i, j))
    out_specs = [tile_spec]
    out_shape = [jax.ShapeDtypeStruct((m, n), out_dtype)]
    if emit_bf16:
        out_specs.append(tile_spec)
        out_shape.append(jax.ShapeDtypeStruct((m, n), BF16))
    norm_once = prenorm is not None and n // bn > 1 and bm * kdim >= NORM_ONCE_MIN
    ordered = cast is not None or norm_once
    sem = ("arbitrary",) * 3 if ordered else ("parallel", "parallel", "arbitrary")
    if cast is not None:
        spec_in, spec_out, shape_out = _cast_specs(cast, (m // bm, n // bn, nk))
        in_specs.append(spec_in)
        args.append(cast[0])
        out_specs.append(spec_out)
        out_shape.append(shape_out)
    scratch = [pltpu.VMEM((bm, bn), F32)] if nk > 1 else []
    if norm_once:
        scratch.append(pltpu.VMEM((bm, kdim), BF16))
    outs = pl.pallas_call(
        functools.partial(_mm_kernel, nx=len(xs), act=act, has_res=res is not None,
                          has_norm=prenorm is not None, norm_once=norm_once, nk=nk,
                          groups=groups, halves=tuple(t[3] for t in rope_tabs),
                          ntab=len(rope_tabs), has_cast=cast is not None, emit_bf16=emit_bf16,
                          w_t=w_t),
        grid=(m // bm, n // bn, nk),
        in_specs=in_specs,
        out_specs=out_specs,
        out_shape=out_shape,
        scratch_shapes=scratch,
        compiler_params=_cparams(sem),
        name=name,
    )(*args)
    return outs[0] if len(outs) == 1 else tuple(outs)


def _cast_specs(cast, grid):
    src, layer = cast
    _, rows, cols = src.shape
    nsteps = math.prod(grid)
    rb = 16 * max(1, -(-rows // (16 * nsteps)))
    while rows % rb:
        rb += 16
    nb = rows // rb

    def step(*ids):
        s = 0
        for g, i in zip(grid, ids):
            s = s * g + i
        return jnp.minimum(s, nb - 1)

    spec_in = pl.BlockSpec((None, rb, cols), lambda *ids: (layer, step(*ids), 0))
    spec_out = pl.BlockSpec((rb, cols), lambda *ids: (step(*ids), 0))
    return spec_in, spec_out, jax.ShapeDtypeStruct((rows, cols), BF16)


def _rope_tables(seq, theta, rot_dim):
    half = rot_dim // 2
    inv = jnp.exp(-math.log(theta) * jnp.arange(half, dtype=F32) * (2.0 / rot_dim))
    ang = jnp.arange(seq, dtype=F32)[:, None] * inv[None, :]
    cos, sin = jnp.cos(ang), jnp.sin(ang)
    pad = LANE - rot_dim
    a = jnp.concatenate([cos, cos, jnp.ones((seq, pad), F32)], axis=1)
    bp = jnp.concatenate([jnp.zeros((seq, half), F32), sin, jnp.zeros((seq, pad), F32)], axis=1)
    bm = jnp.concatenate([-sin, jnp.zeros((seq, half + pad), F32)], axis=1)
    return a, bp, bm, half


def _retention_kernel(cast_in_ref, q_ref, k_ref, v_ref, g_ref, cos_ref, sp_ref, sm_ref, hmask_ref,
                      idec_ref, qdec_ref, kdec_ref, cdec_ref, bmask_ref, o_ref, cast_out_ref,
                      state_ref):
    half = RET_QK_DIM // 2
    cast_out_ref[...] = cast_in_ref[...].astype(BF16)

    @pl.when(pl.program_id(0) == 0)
    def _():
        state_ref[...] = jnp.zeros_like(state_ref)

    cos, sp, sm = cos_ref[...], sp_ref[...], sm_ref[...]

    def rot(x):
        tiles = []
        for t in range(x.shape[1] // LANE):
            xt = x[:, t * LANE:(t + 1) * LANE]
            tiles.append(xt * cos + pltpu.roll(xt, half, 1) * sp
                         + pltpu.roll(xt, LANE - half, 1) * sm)
        return jnp.concatenate(tiles, axis=1)

    for b in range(q_ref.shape[0]):
        q = rot(q_ref[b].astype(F32))
        k = rot(k_ref[b].astype(F32)) * (RET_QK_DIM ** -0.5)
        v = v_ref[b].astype(F32)
        kt = k.T.astype(BF16)
        state = state_ref[b]
        cross = jnp.dot(q.astype(BF16), state.astype(BF16),
                        preferred_element_type=F32) * qdec_ref[...]
        for h in range(RET_HEADS):
            cols = slice(h * RET_V_DIM, (h + 1) * RET_V_DIM)
            qm = (q * hmask_ref[h:h + 1, :]).astype(BF16)
            scores = jnp.dot(qm, kt, preferred_element_type=F32) * idec_ref[h]
            inner = jnp.dot(scores.astype(BF16), v[:, cols].astype(BF16),
                            preferred_element_type=F32)
            o = inner + cross[:, cols]
            o = o * lax.rsqrt(jnp.mean(o * o, axis=-1, keepdims=True) + NORM_EPS)
            gate = g_ref[b, :, cols].astype(F32)
            o_ref[b, :, cols] = (gate * jax.nn.sigmoid(gate) * o).astype(o_ref.dtype)
        vd = (v * kdec_ref[...]).astype(BF16)
        upd = jnp.dot(kt, vd, preferred_element_type=F32)
        state_ref[b] = state * cdec_ref[...] + upd * bmask_ref[...]


def _retention_tables(seq):
    h, c, dv, dk = RET_HEADS, RET_CHUNK, RET_V_DIM, RET_QK_DIM
    half = dk // 2
    inv = jnp.exp(-math.log(RET_ROPE_THETA) * jnp.arange(half, dtype=F32) * (2.0 / dk))
    ang = jnp.arange(seq, dtype=F32)[:, None] * inv[None, :]
    cos1, sin1, zero = jnp.cos(ang), jnp.sin(ang), jnp.zeros((seq, half), F32)
    reps = LANE // dk
    cos = jnp.tile(jnp.concatenate([cos1, cos1], axis=1), (1, reps))
    sp = jnp.tile(jnp.concatenate([zero, sin1], axis=1), (1, reps))
    sm = jnp.tile(jnp.concatenate([-sin1, zero], axis=1), (1, reps))
    log_gamma = jnp.log1p(-jnp.exp2(-5.0 - jnp.arange(h, dtype=F32)))
    idx = jnp.arange(c, dtype=F32)
    rel = idx[:, None] - idx[None, :]
    idec = jnp.where(rel[None] >= 0,
                     jnp.exp(jnp.maximum(rel, 0.0)[None] * log_gamma[:, None, None]), 0.0)
    qdec = jnp.exp((idx + 1.0)[:, None] * log_gamma[None, :])
    kdec = jnp.exp((c - 1.0 - idx)[:, None] * log_gamma[None, :])
    cdec = jnp.exp(c * log_gamma)[None, :]
    rep = lambda a: jnp.repeat(a, dv, axis=1)
    lane_head = np.arange(h * dk) // dk
    hmask = jnp.asarray(lane_head[None, :] == np.arange(h)[:, None], F32)
    bmask = jnp.asarray(lane_head[:, None] == (np.arange(h * dv) // dv)[None, :], F32)
    return cos, sp, sm, hmask, idec, rep(qdec), rep(kdec), rep(cdec), bmask


def retention(proj, batch, seq, tables, cast):
    c = RET_CHUNK
    nch = seq // c
    qw = RET_HEADS * RET_QK_DIM
    col = lambda name: _NAT[name][0] // _NAT[name][1]
    const2 = lambda n: (0, 0)
    cos, sp, sm, hmask, idec, qdec, kdec, cdec, bmask = tables
    tab = pl.BlockSpec((c, LANE), lambda n: (n, 0))
    cast_in, cast_out, cast_shape = _cast_specs(cast, (nch,))
    proj3 = proj.reshape(batch, seq, proj.shape[1])
    seg = lambda name, width: pl.BlockSpec((batch, c, width), lambda n: (0, n, col(name)))
    out, w_cast = pl.pallas_call(
        _retention_kernel,
        grid=(nch,),
        in_specs=[cast_in, seg("rq", qw), seg("rk", qw), seg("rv", RET_OUT), seg("rg", RET_OUT),
                  tab, tab, tab,
                  pl.BlockSpec(hmask.shape, const2),
                  pl.BlockSpec(idec.shape, lambda n: (0, 0, 0)),
                  pl.BlockSpec(qdec.shape, const2),
                  pl.BlockSpec(kdec.shape, const2),
                  pl.BlockSpec(cdec.shape, const2),
                  pl.BlockSpec(bmask.shape, const2)],
        out_specs=[pl.BlockSpec((batch, c, RET_OUT), lambda n: (0, n, 0)), cast_out],
        out_shape=[jax.ShapeDtypeStruct((batch, seq, RET_OUT), BF16), cast_shape],
        scratch_shapes=[pltpu.VMEM((batch, qw, RET_OUT), F32)],
        compiler_params=_cparams(("arbitrary",)),
        name="retention",
    )(cast[0], proj3, proj3, proj3, proj3, cos, sp, sm, hmask, idec, qdec, kdec, cdec, bmask)
    return out.reshape(batch * seq, RET_OUT), w_cast


LOG2E = math.log2(math.e)
M_INIT = float(np.finfo(np.float32).min)


def _flash_update(s, v_c, m_ref, l_ref, acc_ref, h):
    reps = s.shape[1] // LANE

    def fold(x, op):
        tiles = [x[:, r * LANE:(r + 1) * LANE] for r in range(reps)]
        while len(tiles) > 1:
            tiles = [op(a, b) for a, b in zip(tiles[::2], tiles[1::2])]
        return tiles[0]

    m_prev = m_ref[h]
    m_new = jnp.maximum(m_prev, jnp.max(fold(s, jnp.maximum), axis=1, keepdims=True))
    p = jnp.exp2(s - jnp.tile(m_new, (1, reps)))
    alpha = jnp.exp2(m_prev - m_new)
    l_ref[h] = alpha * l_ref[h] + jnp.sum(fold(p, jnp.add), axis=1, keepdims=True)
    m_ref[h] = m_new
    acc_ref[h] = acc_ref[h] * alpha + jnp.dot(p.astype(BF16), v_c, preferred_element_type=F32)


INT_MIN = -2 ** 31
ROW_BLOCK = 64
PASSES_PER_CHECK = 4
COUNT_ROWS = 32
KV_PER_LOOP = 4


def _dsa_kernel(*refs, tq, topk, nqi):
    qi_refs = refs[:nqi]
    (w_ref, kidx_ref, q_ref, k_ref, v_ref, o_ref,
     wb_ref, lg_ref, key_ref, bias_ref, qs_ref, m_ref, l_ref, acc_ref) = refs[nqi:]
    tk = tq
    i = pl.program_id(1)
    nchunks = i + 1
    reps = tk // LANE
    group = DSA_HEADS // DSA_KV_HEADS
    d = DSA_HEAD_DIM
    heads_per_ref = IDX_HEADS // nqi

    w = w_ref[...].astype(F32) * (IDX_HEADS ** -0.5)
    for h in range(IDX_HEADS):
        wb_ref[h] = jnp.broadcast_to(w[:, h:h + 1], (tq, LANE))

    def score_chunk(c, diagonal):
        r0 = pl.multiple_of(c * tk, tk)
        kc = kidx_ref[pl.ds(r0, tk), :]
        for h in range(IDX_HEADS):
            r, t = divmod(h, heads_per_ref)
            lg_ref[h] = lax.dot_general(qi_refs[r][:, t * LANE:(t + 1) * LANE], kc,
                                        (((1,), (1,)), ((), ())), preferred_element_type=F32)
        for rb in range(tq // ROW_BLOCK):
            q0 = rb * ROW_BLOCK
            rows = slice(q0, q0 + ROW_BLOCK)
            acc = jnp.zeros((ROW_BLOCK, tk), F32)
            for h in range(IDX_HEADS):
                acc = acc + jnp.maximum(lg_ref[h, rows, :], 0.0) * jnp.tile(wb_ref[h, rows, :],
                                                                            (1, reps))
            bits = pltpu.bitcast(acc, jnp.int32)
            key = bits ^ (lax.shift_right_arithmetic(bits, 31) & 0x7FFFFFFF)
            if diagonal:
                kpos = lax.broadcasted_iota(jnp.int32, (ROW_BLOCK, tk), 1)
                qpos = q0 + lax.broadcasted_iota(jnp.int32, (ROW_BLOCK, tk), 0)
                key = jnp.where(kpos <= qpos, key, INT_MIN)
            key_ref[c, rows, :] = key
        bias_ref[c] = pltpu.bitcast(key_ref[c].T, F32)

    def score_body(c, _):
        score_chunk(c, False)
        return 0

    lax.fori_loop(0, i, score_body, 0)
    score_chunk(i, True)

    def count_ge(cand):
        def body(c, cnt):
            ge = (pltpu.bitcast(bias_ref[c], jnp.int32) >= cand).astype(jnp.int32)
            return cnt + jnp.sum(ge.reshape(tk // COUNT_ROWS, COUNT_ROWS, tq), axis=0)
        cnt = lax.fori_loop(0, nchunks, body, jnp.zeros((COUNT_ROWS, tq), jnp.int32))
        return jnp.sum(cnt, axis=0, keepdims=True)

    def bit_cond(state):
        b, _, cnt = state
        return jnp.logical_and(b < 32, jnp.max(jnp.abs(cnt - topk)) > 0)

    def bit_body(state):
        b, t, cnt = state
        for u in range(PASSES_PER_CHECK):
            cand = t ^ lax.shift_left(jnp.int32(1), 31 - (b + u))
            n = count_ge(cand)
            take = n >= topk
            t, cnt = jnp.where(take, cand, t), jnp.where(take, n, cnt)
        return b + PASSES_PER_CHECK, t, cnt

    t0 = jnp.full((1, tq), INT_MIN, jnp.int32)
    c0 = jnp.full((1, tq), nchunks * tk, jnp.int32)
    _, thr_row, cnt_row = lax.while_loop(bit_cond, bit_body, (jnp.int32(0), t0, c0))
    ties = jnp.max(cnt_row) > topk

    @pl.when(jnp.logical_not(ties))
    def _():
        thr_min = jnp.maximum(thr_row, INT_MIN + 1)
        thr = jnp.broadcast_to(thr_min, (LANE, tq)).T
        thr = jnp.tile(thr, (1, reps))

        def bias_body(c, _):
            bias_ref[c] = jnp.where(key_ref[c] >= thr, 0.0, -jnp.inf)
            return 0

        lax.fori_loop(0, nchunks, bias_body, 0)

    @pl.when(ties)
    def _():
        def above_body(c, acc):
            gt = (pltpu.bitcast(bias_ref[c], jnp.int32) > thr_row).astype(jnp.int32)
            return acc + jnp.sum(gt.reshape(tk // COUNT_ROWS, COUNT_ROWS, tq), axis=0)
        above = lax.fori_loop(0, nchunks, above_body, jnp.zeros((COUNT_ROWS, tq), jnp.int32))
        need = (topk - jnp.sum(above, axis=0, keepdims=True)).astype(F32)
        lower = (lax.broadcasted_iota(jnp.int32, (tk, tk), 1)
                 < lax.broadcasted_iota(jnp.int32, (tk, tk), 0)).astype(BF16)

        def tie_body(c, before):
            kt = pltpu.bitcast(bias_ref[c], jnp.int32)
            eq = kt == thr_row
            eq_b = jnp.where(eq, 1.0, 0.0).astype(BF16)
            rank = jnp.dot(lower, eq_b, preferred_element_type=F32) + before
            sel = jnp.logical_or(kt > thr_row, jnp.logical_and(eq, rank < need))
            sel = jnp.logical_and(sel, kt > INT_MIN)
            bias_ref[c] = jnp.where(sel, 0.0, -jnp.inf).T
            return before + jnp.sum(eq_b.astype(F32), axis=0, keepdims=True)

        lax.fori_loop(0, nchunks, tie_body, jnp.zeros((1, tq), F32))

    @pl.when(nchunks < key_ref.shape[0])
    def _():
        bias_ref[nchunks] = jnp.full((tq, tk), -jnp.inf, F32)

    hpl = KV_PER_LOOP * group
    for g0 in range(0, DSA_KV_HEADS, KV_PER_LOOP):
        for hh in range(hpl):
            h = g0 * group + hh
            qs_ref[hh * tq:(hh + 1) * tq, :] = q_ref[:, h * d:(h + 1) * d]
        m_ref[...] = jnp.full_like(m_ref, M_INIT)
        l_ref[...] = jnp.zeros_like(l_ref)
        acc_ref[...] = jnp.zeros_like(acc_ref)

        def att_body(c2, _, g0=g0):
            r0 = pl.multiple_of(c2 * 2 * tk, 2 * tk)
            bias = jnp.concatenate([bias_ref[2 * c2], bias_ref[2 * c2 + 1]], axis=1)
            for gg in range(KV_PER_LOOP):
                g = g0 + gg
                kc = k_ref[pl.ds(r0, 2 * tk), g * d:(g + 1) * d]
                vc = v_ref[pl.ds(r0, 2 * tk), g * d:(g + 1) * d]
                s_all = lax.dot_general(qs_ref[gg * group * tq:(gg + 1) * group * tq, :], kc,
                                        (((1,), (1,)), ((), ())), preferred_element_type=F32)
                _flash_update(s_all + jnp.tile(bias, (group, 1)), vc, m_ref, l_ref, acc_ref, gg)
            return 0

        lax.fori_loop(0, (nchunks + 1) // 2, att_body, 0)
        for hh in range(hpl):
            h = g0 * group + hh
            gg, rows = hh // group, slice((hh % group) * tq, (hh % group + 1) * tq)
            o_ref[:, h * d:(h + 1) * d] = (acc_ref[gg, rows, :]
                                           / l_ref[gg, rows, :]).astype(o_ref.dtype)


def dsa_attention(proj, tail, batch, seq, *, tq=256):
    nq = seq // tq
    assert nq % 2 == 0
    topk = min(DSA_TOPK, seq // 4)
    hpl = KV_PER_LOOP * DSA_HEADS // DSA_KV_HEADS
    gtq = DSA_HEADS // DSA_KV_HEADS * tq
    kvw = DSA_KV_HEADS * DSA_HEAD_DIM
    iqw = IDX_HEADS * IDX_HEAD_DIM
    nqi = iqw // QI_BLOCK
    col = lambda name, width: _NAT[name][0] // width
    qi_specs = [pl.BlockSpec((tq, QI_BLOCK), lambda b, i, r=r: (b * nq + i, col("iq", QI_BLOCK) + r))
                for r in range(nqi)]
    return pl.pallas_call(
        functools.partial(_dsa_kernel, tq=tq, topk=topk, nqi=nqi),
        grid=(batch, nq),
        in_specs=qi_specs + [
            pl.BlockSpec((tq, LANE), lambda b, i: (b * nq + i, _TAIL["iw"][0] // LANE)),
            pl.BlockSpec((seq, IDX_HEAD_DIM), lambda b, i: (b, col("ik", IDX_HEAD_DIM))),
            pl.BlockSpec((tq, DSA_OUT), lambda b, i: (b * nq + i, col("dq", DSA_OUT))),
            pl.BlockSpec((seq, kvw), lambda b, i: (b, col("dk", kvw))),
            pl.BlockSpec((seq, kvw), lambda b, i: (b, col("dv", kvw)))],
        out_specs=pl.BlockSpec((tq, DSA_OUT), lambda b, i: (b * nq + i, 0)),
        out_shape=jax.ShapeDtypeStruct((batch * seq, DSA_OUT), BF16),
        scratch_shapes=[pltpu.VMEM((IDX_HEADS, tq, LANE), F32),
                        pltpu.VMEM((IDX_HEADS, tq, tq), F32),
                        pltpu.VMEM((nq, tq, tq), jnp.int32),
                        pltpu.VMEM((nq, tq, tq), F32),
                        pltpu.VMEM((hpl * tq, DSA_HEAD_DIM), BF16),
                        pltpu.VMEM((KV_PER_LOOP, gtq, LANE), F32),
                        pltpu.VMEM((KV_PER_LOOP, gtq, LANE), F32),
                        pltpu.VMEM((KV_PER_LOOP, gtq, DSA_HEAD_DIM), F32)],
        compiler_params=_cparams(("parallel", "arbitrary")),
        name="dsa_attention",
    )(*([proj] * nqi), tail, proj, proj, proj, proj)


def _mla_kernel(cast_in_ref, q_ref, kn_ref, kpe_ref, v_ref, o_ref, cast_out_ref, m_ref, l_ref,
                acc_ref, *, tq, hb):
    tk = tq
    i = pl.program_id(2)
    cast_out_ref[...] = cast_in_ref[...].astype(BF16)
    m_ref[...] = jnp.full_like(m_ref, M_INIT)
    l_ref[...] = jnp.zeros_like(l_ref)
    acc_ref[...] = jnp.zeros_like(acc_ref)
    causal = (lax.broadcasted_iota(jnp.int32, (tq, tk), 1)
              <= lax.broadcasted_iota(jnp.int32, (tq, tk), 0))

    def step(c, masked):
        r0 = pl.multiple_of(c * tk, tk)
        kpe = kpe_ref[pl.ds(r0, tk), :]
        for h in range(hb):
            cols = slice(h * MLA_NOPE_DIM, (h + 1) * MLA_NOPE_DIM)
            kf = jnp.concatenate([kn_ref[pl.ds(r0, tk), cols], kpe], axis=1)
            s = lax.dot_general(q_ref[:, h * MLA_QK_PAD:(h + 1) * MLA_QK_PAD], kf,
                                (((1,), (1,)), ((), ())), preferred_element_type=F32)
            if masked:
                s = jnp.where(causal, s, -jnp.inf)
            _flash_update(s, v_ref[pl.ds(r0, tk), cols], m_ref, l_ref, acc_ref, h)

    def body(c, _):
        step(c, False)
        return 0

    lax.fori_loop(0, i, body, 0)
    step(i, True)
    for h in range(hb):
        o_ref[:, h * MLA_V_DIM:(h + 1) * MLA_V_DIM] = (acc_ref[h] / l_ref[h]).astype(o_ref.dtype)


def mla_attention(q, kv, tail, batch, seq, cast, *, tq=512, hb=4):
    nq = seq // tq
    ng = MLA_HEADS // hb
    cast_in, cast_out, cast_shape = _cast_specs(cast, (batch, ng, nq))
    return pl.pallas_call(
        functools.partial(_mla_kernel, tq=tq, hb=hb),
        grid=(batch, ng, nq),
        in_specs=[cast_in,
                  pl.BlockSpec((tq, hb * MLA_QK_PAD), lambda b, g, i: (b * nq + i, g)),
                  pl.BlockSpec((seq, hb * MLA_NOPE_DIM), lambda b, g, i: (b, g)),
                  pl.BlockSpec((seq, LANE), lambda b, g, i: (b, _TAIL["kr"][0] // LANE)),
                  pl.BlockSpec((seq, hb * MLA_V_DIM), lambda b, g, i: (b, ng + g))],
        out_specs=[pl.BlockSpec((tq, hb * MLA_V_DIM), lambda b, g, i: (b * nq + i, g)), cast_out],
        out_shape=[jax.ShapeDtypeStruct((batch * seq, MLA_OUT), BF16), cast_shape],
        scratch_shapes=[pltpu.VMEM((hb, tq, LANE), F32), pltpu.VMEM((hb, tq, LANE), F32),
                        pltpu.VMEM((hb, tq, MLA_V_DIM), F32)],
        compiler_params=_cparams(("arbitrary", "arbitrary", "arbitrary")),
        name="mla_attention",
    )(cast[0], q, kv, tail, kv)


def _tail_w_in(wt):
    parts, pos = [], 0
    for name in sorted(_TAIL, key=lambda s: _TAIL[s][0]):
        off, width = _TAIL[name]
        assert off == pos
        a = wt[_NAT[name][0]:_NAT[name][0] + _NAT[name][1]]
        parts.append(a)
        if a.shape[0] < width:
            parts.append(jnp.zeros((width - a.shape[0], wt.shape[1]), wt.dtype))
        pos += width
    assert pos == TAIL_WIDTH
    return jnp.concatenate(parts, axis=0)


def _relayout_w_uq(w):
    dep, r = w.shape[:2]
    a = w.reshape(dep, r, MLA_HEADS, MLA_NOPE_DIM + MLA_ROPE_DIM)
    a = jnp.pad(a, ((0, 0), (0, 0), (0, 0), (0, MLA_QK_PAD - a.shape[3])))
    return a.reshape(dep, r, MLA_HEADS * MLA_QK_PAD).astype(BF16)


def _relayout_w_ukv(w):
    dep, r = w.shape[:2]
    a = w.reshape(dep, r, MLA_HEADS, 2, MLA_NOPE_DIM).transpose(0, 1, 3, 2, 4)
    return a.reshape(dep, r, 2 * MLA_HEADS * MLA_NOPE_DIM).astype(BF16)


def _main_tile_modes():
    modes = [(1.0, None)] * (MAIN_WIDTH // LANE)

    def fill(name, mode):
        off, width = _NAT[name]
        for t in range(off // LANE, (off + width) // LANE):
            modes[t] = mode

    fill("dq", (DSA_HEAD_DIM ** -0.5 * LOG2E, 0))
    fill("dk", (1.0, 0))
    fill("iq", (IDX_HEAD_DIM ** -0.5, 0))
    fill("ik", (1.0, 0))
    return modes


def _tail_tile_modes():
    modes = [(1.0, None)] * (TAIL_WIDTH // LANE)
    modes[_TAIL["kr"][0] // LANE] = (1.0, 0)
    return modes


def kernel(x, attn_norm, w_in, mla_q_norm, mla_kv_norm, w_uq, w_ukv, w_o, mlp_norm, w_up,
           w_down, final_norm):
    batch, seq, d_model = x.shape
    depth = w_in.shape[0]
    m = batch * seq

    ret_tables = _retention_tables(seq)
    part_tab = _rope_tables(seq, PARTIAL_ROPE_THETA, PARTIAL_ROT_DIM)
    mla_tab = _rope_tables(seq, MLA_ROPE_THETA, MLA_ROPE_DIM)
    mla_scale = (MLA_NOPE_DIM + MLA_ROPE_DIM) ** -0.5 * LOG2E
    uq_modes = [(mla_scale, None), (mla_scale, 0)] * MLA_HEADS

    w_uq_b, w_ukv_b = _relayout_w_uq(w_uq), _relayout_w_ukv(w_ukv)
    w_in_t = jnp.swapaxes(w_in, 1, 2)

    h = x.reshape(m, d_model)
    xin, w_in_l = rmsnorm(h, attn_norm[0], BF16, cast=(w_in_t, 0))
    for l in range(depth):
        pre = None if l == 0 else attn_norm[l]
        proj = matmul(xin, w_in_l, BF16, name="mm_in", w_t=True, n_out=MAIN_WIDTH, bn=MAIN_BN,
                      prenorm=pre, tile_modes=_main_tile_modes(), rope_tabs=(part_tab,), seq=seq)
        tail = matmul(xin, _tail_w_in(w_in_l), BF16, name="mm_in_tail", w_t=True, bn=TAIL_WIDTH,
                      prenorm=pre, tile_modes=_tail_tile_modes(), rope_tabs=(mla_tab,), seq=seq)

        ret, w_o_l = retention(proj, batch, seq, ret_tables, cast=(w_o, l))
        dsa = dsa_attention(proj, tail, batch, seq)

        q_full = matmul(tail, w_uq_b, BF16, layer=l, name="mm_uq", x_cols=_TAIL["cq"],
                        prenorm=mla_q_norm[l], tile_modes=uq_modes, rope_tabs=(mla_tab,), seq=seq)
        kv_up = matmul(tail, w_ukv_b, BF16, layer=l, name="mm_ukv", x_cols=_TAIL["ckv"],
                       prenorm=mla_kv_norm[l])
        mla, w_up_l = mla_attention(q_full, kv_up, tail, batch, seq, cast=(w_up, l))

        h, hb = matmul((ret, dsa, mla), w_o_l, F32, res=h, name="mm_o", emit_bf16=True)
        up, w_down_l = matmul(hb, w_up_l, BF16, act="relu2", name="mm_up", prenorm=mlp_norm[l],
                              cast=(w_down, l))
        if l + 1 < depth:
            h, xin, w_in_l = matmul(up, w_down_l, F32, res=h, bk=DOWN_BK, name="mm_down",
                                    emit_bf16=True, cast=(w_in_t, l + 1))
        else:
            h = matmul(up, w_down_l, F32, res=h, bk=DOWN_BK, name="mm_down")
    out = rmsnorm(h, final_norm, x.dtype)
    return out.reshape(batch, seq, d_model)
```

```python
import functools
import math

import numpy as np
import jax
import jax.numpy as jnp
from jax import lax
from jax.experimental import pallas as pl
from jax.experimental.pallas import tpu as pltpu

F32 = jnp.float32
BF16 = jnp.bfloat16

RET_HEADS = 8
RET_QK_DIM = 64
RET_V_DIM = 128
RET_CHUNK = 128
RET_ROPE_THETA = 10000.0
DSA_HEADS = 12
DSA_KV_HEADS = 4
DSA_HEAD_DIM = 128
IDX_HEADS = 32
IDX_HEAD_DIM = 128
DSA_TOPK = 256
PARTIAL_ROT_DIM = 32
PARTIAL_ROPE_THETA = 500000.0
MLA_HEADS = 12
MLA_Q_LORA = 768
MLA_KV_LORA = 256
MLA_NOPE_DIM = 128
MLA_ROPE_DIM = 64
MLA_V_DIM = 128
MLA_ROPE_THETA = 10000.0
NORM_EPS = 1e-6

LANE = 128
VMEM_LIMIT = 60 * 1024 * 1024

RET_OUT = RET_HEADS * RET_V_DIM
DSA_OUT = DSA_HEADS * DSA_HEAD_DIM
MLA_OUT = MLA_HEADS * MLA_V_DIM
MLA_QK_PAD = 256

_NAT = {}
_off = 0
for _name, _w in (("rq", 512), ("rk", 512), ("rv", 1024), ("rg", 1024), ("dq", 1536),
                  ("dk", 512), ("dv", 512), ("iq", 4096), ("ik", 128), ("iw", 32),
                  ("cq", 768), ("ckv", 256), ("kr", 64)):
    _NAT[_name] = (_off, _w)
    _off += _w
IN_WIDTH = _off
MAIN_WIDTH = 10240
MAIN_BN = 1024
_TAIL = {"cq": (0, 768), "ckv": (768, 256), "iw": (1024, 128), "kr": (1152, 128)}
TAIL_WIDTH = 1280
QI_BLOCK = 512
DOWN_BK = 2048
NORM_ROWS = 64
NORM_ONCE_MIN = 2 ** 21


def _cparams(sem):
    return pltpu.CompilerParams(dimension_semantics=sem, vmem_limit_bytes=VMEM_LIMIT)


def _rmsnorm_kernel(x_ref, g_ref, *rest):
    o_ref = rest[-1] if len(rest) == 1 else rest[1]
    if len(rest) == 3:
        rest[2][...] = rest[0][...].astype(BF16)
    x = x_ref[...].astype(F32)
    ms = jnp.mean(x * x, axis=-1, keepdims=True)
    o_ref[...] = (x * lax.rsqrt(ms + NORM_EPS) * g_ref[...]).astype(o_ref.dtype)


def rmsnorm(x, g, out_dtype, *, bm=256, cast=None):
    m, width = x.shape
    assert m % bm == 0
    in_specs = [pl.BlockSpec((bm, width), lambda i: (i, 0)),
                pl.BlockSpec((1, width), lambda i: (0, 0))]
    out_specs = pl.BlockSpec((bm, width), lambda i: (i, 0))
    out_shape = jax.ShapeDtypeStruct((m, width), out_dtype)
    args = [x, g.reshape(1, width).astype(F32)]
    if cast is not None:
        cast_in, cast_out, cast_shape = _cast_specs(cast, (m // bm,))
        in_specs.append(cast_in)
        args.append(cast[0])
        out_specs, out_shape = [out_specs, cast_out], [out_shape, cast_shape]
    return pl.pallas_call(
        _rmsnorm_kernel,
        grid=(m // bm,),
        in_specs=in_specs,
        out_specs=out_specs,
        out_shape=out_shape,
        compiler_params=_cparams(("arbitrary",)),
        name="rmsnorm",
    )(*args)


def _mm_kernel(*refs, nx, act, has_res, has_norm, norm_once, nk, groups, halves, ntab, has_cast,
               emit_bf16, w_t):
    refs = list(refs)
    take = lambda n=1: [refs.pop(0) for _ in range(n)]
    x_refs = take(nx)
    (w_ref,) = take()
    r_ref = take()[0] if has_res else None
    g_ref = take()[0] if has_norm else None
    tab_refs = take(3 * ntab)
    cast_in_ref = take()[0] if has_cast else None
    (o_ref,) = take()
    ob_ref = take()[0] if emit_bf16 else None
    if has_cast:
        take()[0][...] = cast_in_ref[...].astype(BF16)
    acc_ref = take()[0] if nk > 1 else None
    xn_ref = take()[0] if norm_once else None

    def store_tiles(acc, pattern):
        for t, (scale, ridx) in enumerate(pattern):
            x = acc[:, t * LANE:(t + 1) * LANE]
            if ridx is not None:
                a, bp, bm = (tab_refs[3 * ridx + k][...] for k in range(3))
                half = halves[ridx]
                x = x * a + pltpu.roll(x, half, 1) * bp + pltpu.roll(x, LANE - half, 1) * bm
            if scale != 1.0:
                x = x * scale
            o_ref[:, t * LANE:(t + 1) * LANE] = x.astype(o_ref.dtype)

    def epilogue(acc):
        if act == "relu2":
            acc = jnp.square(jnp.maximum(acc, 0.0))
        if has_res:
            acc = acc + r_ref[...]
        if emit_bf16:
            ob_ref[...] = acc.astype(BF16)
        if groups is None:
            o_ref[...] = acc.astype(o_ref.dtype)
        elif len(groups) == 1:
            store_tiles(acc, groups[0][0])
        else:
            j = pl.program_id(1)
            for pattern, js in groups:
                cond = functools.reduce(jnp.logical_or, [j == jj for jj in js])
                pl.when(cond)(functools.partial(store_tiles, acc, pattern))

    def normalised(x):
        xf = x.astype(F32)
        ms = jnp.mean(xf * xf, axis=-1, keepdims=True)
        return (xf * lax.rsqrt(ms + NORM_EPS) * g_ref[...]).astype(BF16)

    if norm_once:
        @pl.when(pl.program_id(1) == 0)
        def _():
            def rows_body(r, _):
                rows = pl.ds(pl.multiple_of(r * NORM_ROWS, NORM_ROWS), NORM_ROWS)
                xn_ref[rows, :] = normalised(x_refs[0][rows, :])
                return 0
            lax.fori_loop(0, xn_ref.shape[0] // NORM_ROWS, rows_body, 0)

    if nk > 1:
        @pl.when(pl.program_id(2) == 0)
        def _():
            acc_ref[...] = jnp.zeros_like(acc_ref)

    part, k0 = None, 0
    for x_ref in x_refs:
        kw = x_ref.shape[1]
        if norm_once:
            x = xn_ref[...]
        elif has_norm:
            x = normalised(x_ref[...])
        else:
            x = x_ref[...]
        if w_t:
            d = lax.dot_general(x, w_ref[...], (((1,), (1,)), ((), ())),
                                preferred_element_type=F32)
        else:
            d = jnp.dot(x, w_ref[k0:k0 + kw, :], preferred_element_type=F32)
        part = d if part is None else part + d
        k0 += kw
    if nk == 1:
        epilogue(part)
    else:
        acc_ref[...] += part

        @pl.when(pl.program_id(2) == nk - 1)
        def _():
            epilogue(acc_ref[...])


def matmul(x, w, out_dtype, *, name, layer=None, act=None, res=None, bm=1024, bn=1024, bk=4096,
           n_out=None, tile_modes=None, rope_tabs=(), seq=None, cast=None, x_cols=None,
           prenorm=None, w_t=False, emit_bf16=False):
    xs = x if isinstance(x, (tuple, list)) else (x,)
    m = xs[0].shape[0]
    kdim = sum(a.shape[1] for a in xs) if x_cols is None else x_cols[1]
    assert kdim == w.shape[-1 if w_t else -2]
    n = w.shape[-2 if w_t else -1] if n_out is None else n_out
    bm, bn, bk = min(bm, m), min(bn, n), min(bk, kdim)
    assert m % bm == 0 and n % bn == 0 and kdim % bk == 0
    nk = kdim // bk
    assert len(xs) == 1 or nk == 1
    if x_cols is not None:
        assert len(xs) == 1 and nk == 1 and x_cols[0] % kdim == 0
        xcb = x_cols[0] // kdim
        in_specs = [pl.BlockSpec((bm, kdim), lambda i, j, k: (i, xcb))]
    elif len(xs) == 1:
        in_specs = [pl.BlockSpec((bm, bk), lambda i, j, k: (i, k))]
    else:
        in_specs = [pl.BlockSpec((bm, a.shape[1]), lambda i, j, k: (i, 0)) for a in xs]
    if w_t:
        assert layer is None and len(xs) == 1
        in_specs.append(pl.BlockSpec((bn, bk), lambda i, j, k: (j, k)))
    elif layer is None:
        in_specs.append(pl.BlockSpec((bk, bn), lambda i, j, k: (k, j)))
    else:
        in_specs.append(pl.BlockSpec((None, bk, bn), lambda i, j, k: (layer, k, j)))
    args = [*xs, w]
    if res is not None:
        in_specs.append(pl.BlockSpec((bm, bn), lambda i, j, k: (i, j)))
        args.append(res)
    if prenorm is not None:
        assert nk == 1 and len(xs) == 1
        in_specs.append(pl.BlockSpec((1, kdim), lambda i, j, k: (0, 0)))
        args.append(prenorm.reshape(1, kdim).astype(F32))
    groups = None
    if tile_modes is not None:
        assert len(tile_modes) == n // LANE and seq % bm == 0
        per_block = bn // LANE
        by_pattern = {}
        for j in range(n // bn):
            by_pattern.setdefault(tuple(tile_modes[j * per_block:(j + 1) * per_block]), []).append(j)
        groups = tuple((p, tuple(js)) for p, js in by_pattern.items())
        nseq = seq // bm
        for a, bp, bmt, _ in rope_tabs:
            in_specs += [pl.BlockSpec((bm, LANE), lambda i, j, k: (i % nseq, 0))] * 3
            args += [a, bp, bmt]
    tile_spec = pl.BlockSpec((bm, bn), lambda i, j, k: (i, j))
    out_specs = [tile_spec]
    out_shape = [jax.ShapeDtypeStruct((m, n), out_dtype)]
    if emit_bf16:
        out_specs.append(tile_spec)
        out_shape.append(jax.ShapeDtypeStruct((m, n), BF16))
    norm_once = prenorm is not None and n // bn > 1 and bm * kdim >= NORM_ONCE_MIN
    ordered = cast is not None or norm_once
    sem = ("arbitrary",) * 3 if ordered else ("parallel", "parallel", "arbitrary")
    if cast is not None:
        spec_in, spec_out, shape_out = _cast_specs(cast, (m // bm, n // bn, nk))
        in_specs.append(spec_in)
        args.append(cast[0])
        out_specs.append(spec_out)
        out_shape.append(shape_out)
    scratch = [pltpu.VMEM((bm, bn), F32)] if nk > 1 else []
    if norm_once:
        scratch.append(pltpu.VMEM((bm, kdim), BF16))
    outs = pl.pallas_call(
        functools.partial(_mm_kernel, nx=len(xs), act=act, has_res=res is not None,
                          has_norm=prenorm is not None, norm_once=norm_once, nk=nk,
                          groups=groups, halves=tuple(t[3] for t in rope_tabs),
                          ntab=len(rope_tabs), has_cast=cast is not None, emit_bf16=emit_bf16,
                          w_t=w_t),
        grid=(m // bm, n // bn, nk),
        in_specs=in_specs,
        out_specs=out_specs,
        out_shape=out_shape,
        scratch_shapes=scratch,
        compiler_params=_cparams(sem),
        name=name,
    )(*args)
    return outs[0] if len(outs) == 1 else tuple(outs)


def _cast_specs(cast, grid):
    src, layer = cast
    _, rows, cols = src.shape
    nsteps = math.prod(grid)
    rb = 16 * max(1, -(-rows // (16 * nsteps)))
    while rows % rb:
        rb += 16
    nb = rows // rb

    def step(*ids):
        s = 0
        for g, i in zip(grid, ids):
            s = s * g + i
        return jnp.minimum(s, nb - 1)

    spec_in = pl.BlockSpec((None, rb, cols), lambda *ids: (layer, step(*ids), 0))
    spec_out = pl.BlockSpec((rb, cols), lambda *ids: (step(*ids), 0))
    return spec_in, spec_out, jax.ShapeDtypeStruct((rows, cols), BF16)


def _rope_tables(seq, theta, rot_dim):
    half = rot_dim // 2
    inv = jnp.exp(-math.log(theta) * jnp.arange(half, dtype=F32) * (2.0 / rot_dim))
    ang = jnp.arange(seq, dtype=F32)[:, None] * inv[None, :]
    cos, sin = jnp.cos(ang), jnp.sin(ang)
    pad = LANE - rot_dim
    a = jnp.concatenate([cos, cos, jnp.ones((seq, pad), F32)], axis=1)
    bp = jnp.concatenate([jnp.zeros((seq, half), F32), sin, jnp.zeros((seq, pad), F32)], axis=1)
    bm = jnp.concatenate([-sin, jnp.zeros((seq, half + pad), F32)], axis=1)
    return a, bp, bm, half


def _retention_kernel(cast_in_ref, q_ref, k_ref, v_ref, g_ref, cos_ref, sp_ref, sm_ref, hmask_ref,
                      idec_ref, qdec_ref, kdec_ref, cdec_ref, bmask_ref, o_ref, cast_out_ref,
                      state_ref):
    half = RET_QK_DIM // 2
    cast_out_ref[...] = cast_in_ref[...].astype(BF16)

    @pl.when(pl.program_id(0) == 0)
    def _():
        state_ref[...] = jnp.zeros_like(state_ref)

    cos, sp, sm = cos_ref[...], sp_ref[...], sm_ref[...]

    def rot(x):
        tiles = []
        for t in range(x.shape[1] // LANE):
            xt = x[:, t * LANE:(t + 1) * LANE]
            tiles.append(xt * cos + pltpu.roll(xt, half, 1) * sp
                         + pltpu.roll(xt, LANE - half, 1) * sm)
        return jnp.concatenate(tiles, axis=1)

    for b in range(q_ref.shape[0]):
        q = rot(q_ref[b].astype(F32))
        k = rot(k_ref[b].astype(F32)) * (RET_QK_DIM ** -0.5)
        v = v_ref[b].astype(F32)
        kt = k.T.astype(BF16)
        state = state_ref[b]
        cross = jnp.dot(q.astype(BF16), state.astype(BF16),
                        preferred_element_type=F32) * qdec_ref[...]
        for h in range(RET_HEADS):
            cols = slice(h * RET_V_DIM, (h + 1) * RET_V_DIM)
            qm = (q * hmask_ref[h:h + 1, :]).astype(BF16)
            scores = jnp.dot(qm, kt, preferred_element_type=F32) * idec_ref[h]
            inner = jnp.dot(scores.astype(BF16), v[:, cols].astype(BF16),
                            preferred_element_type=F32)
            o = inner + cross[:, cols]
            o = o * lax.rsqrt(jnp.mean(o * o, axis=-1, keepdims=True) + NORM_EPS)
            gate = g_ref[b, :, cols].astype(F32)
            o_ref[b, :, cols] = (gate * jax.nn.sigmoid(gate) * o).astype(o_ref.dtype)
        vd = (v * kdec_ref[...]).astype(BF16)
        upd = jnp.dot(kt, vd, preferred_element_type=F32)
        state_ref[b] = state * cdec_ref[...] + upd * bmask_ref[...]


def _retention_tables(seq):
    h, c, dv, dk = RET_HEADS, RET_CHUNK, RET_V_DIM, RET_QK_DIM
    half = dk // 2
    inv = jnp.exp(-math.log(RET_ROPE_THETA) * jnp.arange(half, dtype=F32) * (2.0 / dk))
    ang = jnp.arange(seq, dtype=F32)[:, None] * inv[None, :]
    cos1, sin1, zero = jnp.cos(ang), jnp.sin(ang), jnp.zeros((seq, half), F32)
    reps = LANE // dk
    cos = jnp.tile(jnp.concatenate([cos1, cos1], axis=1), (1, reps))
    sp = jnp.tile(jnp.concatenate([zero, sin1], axis=1), (1, reps))
    sm = jnp.tile(jnp.concatenate([-sin1, zero], axis=1), (1, reps))
    log_gamma = jnp.log1p(-jnp.exp2(-5.0 - jnp.arange(h, dtype=F32)))
    idx = jnp.arange(c, dtype=F32)
    rel = idx[:, None] - idx[None, :]
    idec = jnp.where(rel[None] >= 0,
                     jnp.exp(jnp.maximum(rel, 0.0)[None] * log_gamma[:, None, None]), 0.0)
    qdec = jnp.exp((idx + 1.0)[:, None] * log_gamma[None, :])
    kdec = jnp.exp((c - 1.0 - idx)[:, None] * log_gamma[None, :])
    cdec = jnp.exp(c * log_gamma)[None, :]
    rep = lambda a: jnp.repeat(a, dv, axis=1)
    lane_head = np.arange(h * dk) // dk
    hmask = jnp.asarray(lane_head[None, :] == np.arange(h)[:, None], F32)
    bmask = jnp.asarray(lane_head[:, None] == (np.arange(h * dv) // dv)[None, :], F32)
    return cos, sp, sm, hmask, idec, rep(qdec), rep(kdec), rep(cdec), bmask


def retention(proj, batch, seq, tables, cast):
    c = RET_CHUNK
    nch = seq // c
    qw = RET_HEADS * RET_QK_DIM
    col = lambda name: _NAT[name][0] // _NAT[name][1]
    const2 = lambda n: (0, 0)
    cos, sp, sm, hmask, idec, qdec, kdec, cdec, bmask = tables
    tab = pl.BlockSpec((c, LANE), lambda n: (n, 0))
    cast_in, cast_out, cast_shape = _cast_specs(cast, (nch,))
    proj3 = proj.reshape(batch, seq, proj.shape[1])
    seg = lambda name, width: pl.BlockSpec((batch, c, width), lambda n: (0, n, col(name)))
    out, w_cast = pl.pallas_call(
        _retention_kernel,
        grid=(nch,),
        in_specs=[cast_in, seg("rq", qw), seg("rk", qw), seg("rv", RET_OUT), seg("rg", RET_OUT),
                  tab, tab, tab,
                  pl.BlockSpec(hmask.shape, const2),
                  pl.BlockSpec(idec.shape, lambda n: (0, 0, 0)),
                  pl.BlockSpec(qdec.shape, const2),
                  pl.BlockSpec(kdec.shape, const2),
                  pl.BlockSpec(cdec.shape, const2),
                  pl.BlockSpec(bmask.shape, const2)],
        out_specs=[pl.BlockSpec((batch, c, RET_OUT), lambda n: (0, n, 0)), cast_out],
        out_shape=[jax.ShapeDtypeStruct((batch, seq, RET_OUT), BF16), cast_shape],
        scratch_shapes=[pltpu.VMEM((batch, qw, RET_OUT), F32)],
        compiler_params=_cparams(("arbitrary",)),
        name="retention",
    )(cast[0], proj3, proj3, proj3, proj3, cos, sp, sm, hmask, idec, qdec, kdec, cdec, bmask)
    return out.reshape(batch * seq, RET_OUT), w_cast


LOG2E = math.log2(math.e)
M_INIT = float(np.finfo(np.float32).min)


def _flash_update(s, v_c, m_ref, l_ref, acc_ref, h, rows=slice(None)):
    reps = s.shape[1] // LANE

    def fold(x, op):
        tiles = [x[:, r * LANE:(r + 1) * LANE] for r in range(reps)]
        while len(tiles) > 1:
            tiles = [op(a, b) for a, b in zip(tiles[::2], tiles[1::2])]
        return tiles[0]

    m_prev = m_ref[h, rows, :]
    m_new = jnp.maximum(m_prev, jnp.max(fold(s, jnp.maximum), axis=1, keepdims=True))
    p = jnp.exp2(s - jnp.tile(m_new, (1, reps)))
    alpha = jnp.exp2(m_prev - m_new)
    l_ref[h, rows, :] = alpha * l_ref[h, rows, :] + jnp.sum(fold(p, jnp.add), axis=1,
                                                            keepdims=True)
    m_ref[h, rows, :] = m_new
    acc_ref[h, rows, :] = acc_ref[h, rows, :] * alpha + jnp.dot(p.astype(BF16), v_c,
                                                                preferred_element_type=F32)


INT_MIN = -2 ** 31
ROW_BLOCK = 64
PASSES_PER_CHECK = 4
COUNT_ROWS = 32
KV_PER_LOOP = 4


def _dsa_kernel(*refs, tq, topk, nqi):
    qi_refs = refs[:nqi]
    (w_ref, kidx_ref, q_ref, k_ref, v_ref, o_ref,
     wb_ref, lg_ref, key_ref, bias_ref, qs_ref, m_ref, l_ref, acc_ref) = refs[nqi:]
    tk = tq
    i = pl.program_id(1)
    nchunks = i + 1
    reps = tk // LANE
    group = DSA_HEADS // DSA_KV_HEADS
    d = DSA_HEAD_DIM
    heads_per_ref = IDX_HEADS // nqi

    w = w_ref[...].astype(F32) * (IDX_HEADS ** -0.5)
    for h in range(IDX_HEADS):
        wb_ref[h] = jnp.broadcast_to(w[:, h:h + 1], (tq, LANE))

    def score_chunk(c, diagonal):
        r0 = pl.multiple_of(c * tk, tk)
        kc = kidx_ref[pl.ds(r0, tk), :]
        for h in range(IDX_HEADS):
            r, t = divmod(h, heads_per_ref)
            lg_ref[h] = lax.dot_general(qi_refs[r][:, t * LANE:(t + 1) * LANE], kc,
                                        (((1,), (1,)), ((), ())), preferred_element_type=F32)
        for rb in range(tq // ROW_BLOCK):
            q0 = rb * ROW_BLOCK
            rows = slice(q0, q0 + ROW_BLOCK)
            acc = jnp.zeros((ROW_BLOCK, tk), F32)
            for h in range(IDX_HEADS):
                acc = acc + jnp.maximum(lg_ref[h, rows, :], 0.0) * jnp.tile(wb_ref[h, rows, :],
                                                                            (1, reps))
            bits = pltpu.bitcast(acc, jnp.int32)
            key = bits ^ (lax.shift_right_arithmetic(bits, 31) & 0x7FFFFFFF)
            if diagonal:
                kpos = lax.broadcasted_iota(jnp.int32, (ROW_BLOCK, tk), 1)
                qpos = q0 + lax.broadcasted_iota(jnp.int32, (ROW_BLOCK, tk), 0)
                key = jnp.where(kpos <= qpos, key, INT_MIN)
            key_ref[c, rows, :] = key
        bias_ref[c] = pltpu.bitcast(key_ref[c].T, F32)

    def score_body(c, _):
        score_chunk(c, False)
        return 0

    lax.fori_loop(0, i, score_body, 0)
    score_chunk(i, True)

    def count_ge(cand):
        def body(c, cnt):
            ge = (pltpu.bitcast(bias_ref[c], jnp.int32) >= cand).astype(jnp.int32)
            return cnt + jnp.sum(ge.reshape(tk // COUNT_ROWS, COUNT_ROWS, tq), axis=0)
        cnt = lax.fori_loop(0, nchunks, body, jnp.zeros((COUNT_ROWS, tq), jnp.int32))
        return jnp.sum(cnt, axis=0, keepdims=True)

    def bit_cond(state):
        b, _, _, excess = state
        return jnp.logical_and(b < 32, excess > 0)

    def bit_body(state):
        b, t, cnt, _ = state
        for u in range(PASSES_PER_CHECK):
            cand = t ^ lax.shift_left(jnp.int32(1), 31 - (b + u))
            n = count_ge(cand)
            take = n >= topk
            t, cnt = jnp.where(take, cand, t), jnp.where(take, n, cnt)
        return b + PASSES_PER_CHECK, t, cnt, jnp.max(cnt) - topk

    t0 = jnp.full((1, tq), INT_MIN, jnp.int32)
    total = nchunks * tk
    c0 = jnp.full((1, tq), total, jnp.int32)
    _, thr_row, _, excess = lax.while_loop(bit_cond, bit_body,
                                           (jnp.int32(0), t0, c0, total - topk))
    ties = excess > 0

    @pl.when(jnp.logical_not(ties))
    def _():
        thr_min = jnp.maximum(thr_row, INT_MIN + 1)
        thr = jnp.broadcast_to(thr_min, (LANE, tq)).T
        thr = jnp.tile(thr, (1, reps))

        def bias_body(c, _):
            bias_ref[c] = jnp.where(key_ref[c] >= thr, 0.0, -jnp.inf)
            return 0

        lax.fori_loop(0, nchunks, bias_body, 0)

    @pl.when(ties)
    def _():
        def above_body(c, acc):
            gt = (pltpu.bitcast(bias_ref[c], jnp.int32) > thr_row).astype(jnp.int32)
            return acc + jnp.sum(gt.reshape(tk // COUNT_ROWS, COUNT_ROWS, tq), axis=0)
        above = lax.fori_loop(0, nchunks, above_body, jnp.zeros((COUNT_ROWS, tq), jnp.int32))
        need = (topk - jnp.sum(above, axis=0, keepdims=True)).astype(F32)
        lower = (lax.broadcasted_iota(jnp.int32, (tk, tk), 1)
                 < lax.broadcasted_iota(jnp.int32, (tk, tk), 0)).astype(BF16)

        def tie_body(c, before):
            kt = pltpu.bitcast(bias_ref[c], jnp.int32)
            eq = kt == thr_row
            eq_b = jnp.where(eq, 1.0, 0.0).astype(BF16)
            rank = jnp.dot(lower, eq_b, preferred_element_type=F32) + before
            sel = jnp.logical_or(kt > thr_row, jnp.logical_and(eq, rank < need))
            sel = jnp.logical_and(sel, kt > INT_MIN)
            bias_ref[c] = jnp.where(sel, 0.0, -jnp.inf).T
            return before + jnp.sum(eq_b.astype(F32), axis=0, keepdims=True)

        lax.fori_loop(0, nchunks, tie_body, jnp.zeros((1, tq), F32))

    @pl.when(nchunks < key_ref.shape[0])
    def _():
        bias_ref[nchunks] = jnp.full((tq, tk), -jnp.inf, F32)

    hpl = KV_PER_LOOP * group
    for g0 in range(0, DSA_KV_HEADS, KV_PER_LOOP):
        for hh in range(hpl):
            h = g0 * group + hh
            qs_ref[hh * tq:(hh + 1) * tq, :] = q_ref[:, h * d:(h + 1) * d]
        m_ref[...] = jnp.full_like(m_ref, M_INIT)
        l_ref[...] = jnp.zeros_like(l_ref)
        acc_ref[...] = jnp.zeros_like(acc_ref)

        def att_body(c2, _, g0=g0):
            r0 = pl.multiple_of(c2 * 2 * tk, 2 * tk)
            bias = jnp.concatenate([bias_ref[2 * c2], bias_ref[2 * c2 + 1]], axis=1)
            for gg in range(KV_PER_LOOP):
                g = g0 + gg
                kc = k_ref[pl.ds(r0, 2 * tk), g * d:(g + 1) * d]
                vc = v_ref[pl.ds(r0, 2 * tk), g * d:(g + 1) * d]
                s_all = lax.dot_general(qs_ref[gg * group * tq:(gg + 1) * group * tq, :], kc,
                                        (((1,), (1,)), ((), ())), preferred_element_type=F32)
                _flash_update(s_all + jnp.tile(bias, (group, 1)), vc, m_ref, l_ref, acc_ref, gg)
            return 0

        lax.fori_loop(0, (nchunks + 1) // 2, att_body, 0)
        for hh in range(hpl):
            h = g0 * group + hh
            gg, rows = hh // group, slice((hh % group) * tq, (hh % group + 1) * tq)
            o_ref[:, h * d:(h + 1) * d] = (acc_ref[gg, rows, :]
                                           / l_ref[gg, rows, :]).astype(o_ref.dtype)


def dsa_attention(proj, tail, batch, seq, *, tq=256):
    nq = seq // tq
    assert nq % 2 == 0
    topk = min(DSA_TOPK, seq // 4)
    hpl = KV_PER_LOOP * DSA_HEADS // DSA_KV_HEADS
    gtq = DSA_HEADS // DSA_KV_HEADS * tq
    kvw = DSA_KV_HEADS * DSA_HEAD_DIM
    iqw = IDX_HEADS * IDX_HEAD_DIM
    nqi = iqw // QI_BLOCK
    col = lambda name, width: _NAT[name][0] // width
    qi_specs = [pl.BlockSpec((tq, QI_BLOCK), lambda b, i, r=r: (b * nq + i, col("iq", QI_BLOCK) + r))
                for r in range(nqi)]
    return pl.pallas_call(
        functools.partial(_dsa_kernel, tq=tq, topk=topk, nqi=nqi),
        grid=(batch, nq),
        in_specs=qi_specs + [
            pl.BlockSpec((tq, LANE), lambda b, i: (b * nq + i, _TAIL["iw"][0] // LANE)),
            pl.BlockSpec((seq, IDX_HEAD_DIM), lambda b, i: (b, col("ik", IDX_HEAD_DIM))),
            pl.BlockSpec((tq, DSA_OUT), lambda b, i: (b * nq + i, col("dq", DSA_OUT))),
            pl.BlockSpec((seq, kvw), lambda b, i: (b, col("dk", kvw))),
            pl.BlockSpec((seq, kvw), lambda b, i: (b, col("dv", kvw)))],
        out_specs=pl.BlockSpec((tq, DSA_OUT), lambda b, i: (b * nq + i, 0)),
        out_shape=jax.ShapeDtypeStruct((batch * seq, DSA_OUT), BF16),
        scratch_shapes=[pltpu.VMEM((IDX_HEADS, tq, LANE), F32),
                        pltpu.VMEM((IDX_HEADS, tq, tq), F32),
                        pltpu.VMEM((nq, tq, tq), jnp.int32),
                        pltpu.VMEM((nq, tq, tq), F32),
                        pltpu.VMEM((hpl * tq, DSA_HEAD_DIM), BF16),
                        pltpu.VMEM((KV_PER_LOOP, gtq, LANE), F32),
                        pltpu.VMEM((KV_PER_LOOP, gtq, LANE), F32),
                        pltpu.VMEM((KV_PER_LOOP, gtq, DSA_HEAD_DIM), F32)],
        compiler_params=_cparams(("parallel", "arbitrary")),
        name="dsa_attention",
    )(*([proj] * nqi), tail, proj, proj, proj, proj)


def _mla_kernel(cast_in_ref, q_ref, kn_ref, kpe_ref, v_ref, o_ref, cast_out_ref, m_ref, l_ref,
                acc_ref, *, tq, hb):
    tk = tq
    i = pl.program_id(2)
    cast_out_ref[...] = cast_in_ref[...].astype(BF16)
    m_ref[...] = jnp.full_like(m_ref, M_INIT)
    l_ref[...] = jnp.zeros_like(l_ref)
    acc_ref[...] = jnp.zeros_like(acc_ref)
    half = tq // 2

    def scores(h, rows, r0, nkeys):
        cols = slice(h * MLA_NOPE_DIM, (h + 1) * MLA_NOPE_DIM)
        kf = jnp.concatenate([kn_ref[pl.ds(r0, nkeys), cols], kpe_ref[pl.ds(r0, nkeys), :]],
                             axis=1)
        s = lax.dot_general(q_ref[rows, h * MLA_QK_PAD:(h + 1) * MLA_QK_PAD], kf,
                            (((1,), (1,)), ((), ())), preferred_element_type=F32)
        return s, v_ref[pl.ds(r0, nkeys), cols]

    def body(c, _):
        r0 = pl.multiple_of(c * tk, tk)
        for h in range(hb):
            s, vc = scores(h, slice(None), r0, tk)
            _flash_update(s, vc, m_ref, l_ref, acc_ref, h)
        return 0

    lax.fori_loop(0, i, body, 0)

    r0 = pl.multiple_of(i * tk, tk)
    for h in range(hb):
        s, vc = scores(h, slice(None), r0, half)
        keep = (lax.broadcasted_iota(jnp.int32, (tq, half), 1)
                <= lax.broadcasted_iota(jnp.int32, (tq, half), 0))
        _flash_update(jnp.where(keep, s, -jnp.inf), vc, m_ref, l_ref, acc_ref, h)
    for h in range(hb):
        s, vc = scores(h, slice(half, tq), r0 + half, half)
        keep = (lax.broadcasted_iota(jnp.int32, (half, half), 1)
                <= lax.broadcasted_iota(jnp.int32, (half, half), 0))
        _flash_update(jnp.where(keep, s, -jnp.inf), vc, m_ref, l_ref, acc_ref, h,
                      rows=slice(half, tq))
    for h in range(hb):
        o_ref[:, h * MLA_V_DIM:(h + 1) * MLA_V_DIM] = (acc_ref[h] / l_ref[h]).astype(o_ref.dtype)


def mla_attention(q, kv, tail, batch, seq, cast, *, tq=512, hb=4):
    nq = seq // tq
    ng = MLA_HEADS // hb
    cast_in, cast_out, cast_shape = _cast_specs(cast, (batch, ng, nq))
    return pl.pallas_call(
        functools.partial(_mla_kernel, tq=tq, hb=hb),
        grid=(batch, ng, nq),
        in_specs=[cast_in,
                  pl.BlockSpec((tq, hb * MLA_QK_PAD), lambda b, g, i: (b * nq + i, g)),
                  pl.BlockSpec((seq, hb * MLA_NOPE_DIM), lambda b, g, i: (b, g)),
                  pl.BlockSpec((seq, LANE), lambda b, g, i: (b, _TAIL["kr"][0] // LANE)),
                  pl.BlockSpec((seq, hb * MLA_V_DIM), lambda b, g, i: (b, ng + g))],
        out_specs=[pl.BlockSpec((tq, hb * MLA_V_DIM), lambda b, g, i: (b * nq + i, g)), cast_out],
        out_shape=[jax.ShapeDtypeStruct((batch * seq, MLA_OUT), BF16), cast_shape],
        scratch_shapes=[pltpu.VMEM((hb, tq, LANE), F32), pltpu.VMEM((hb, tq, LANE), F32),
                        pltpu.VMEM((hb, tq, MLA_V_DIM), F32)],
        compiler_params=_cparams(("arbitrary", "arbitrary", "arbitrary")),
        name="mla_attention",
    )(cast[0], q, kv, tail, kv)


def _tail_w_in(wt):
    parts, pos = [], 0
    for name in sorted(_TAIL, key=lambda s: _TAIL[s][0]):
        off, width = _TAIL[name]
        assert off == pos
        a = wt[_NAT[name][0]:_NAT[name][0] + _NAT[name][1]]
        parts.append(a)
        if a.shape[0] < width:
            parts.append(jnp.zeros((width - a.shape[0], wt.shape[1]), wt.dtype))
        pos += width
    assert pos == TAIL_WIDTH
    return jnp.concatenate(parts, axis=0)


def _relayout_w_uq(w):
    dep, r = w.shape[:2]
    a = w.reshape(dep, r, MLA_HEADS, MLA_NOPE_DIM + MLA_ROPE_DIM)
    a = jnp.pad(a, ((0, 0), (0, 0), (0, 0), (0, MLA_QK_PAD - a.shape[3])))
    return a.reshape(dep, r, MLA_HEADS * MLA_QK_PAD).astype(BF16)


def _relayout_w_ukv(w):
    dep, r = w.shape[:2]
    a = w.reshape(dep, r, MLA_HEADS, 2, MLA_NOPE_DIM).transpose(0, 1, 3, 2, 4)
    return a.reshape(dep, r, 2 * MLA_HEADS * MLA_NOPE_DIM).astype(BF16)


def _main_tile_modes():
    modes = [(1.0, None)] * (MAIN_WIDTH // LANE)

    def fill(name, mode):
        off, width = _NAT[name]
        for t in range(off // LANE, (off + width) // LANE):
            modes[t] = mode

    fill("dq", (DSA_HEAD_DIM ** -0.5 * LOG2E, 0))
    fill("dk", (1.0, 0))
    fill("iq", (IDX_HEAD_DIM ** -0.5, 0))
    fill("ik", (1.0, 0))
    return modes


def _tail_tile_modes():
    modes = [(1.0, None)] * (TAIL_WIDTH // LANE)
    modes[_TAIL["kr"][0] // LANE] = (1.0, 0)
    return modes


def kernel(x, attn_norm, w_in, mla_q_norm, mla_kv_norm, w_uq, w_ukv, w_o, mlp_norm, w_up,
           w_down, final_norm):
    batch, seq, d_model = x.shape
    depth = w_in.shape[0]
    m = batch * seq

    ret_tables = _retention_tables(seq)
    part_tab = _rope_tables(seq, PARTIAL_ROPE_THETA, PARTIAL_ROT_DIM)
    mla_tab = _rope_tables(seq, MLA_ROPE_THETA, MLA_ROPE_DIM)
    mla_scale = (MLA_NOPE_DIM + MLA_ROPE_DIM) ** -0.5 * LOG2E
    uq_modes = [(mla_scale, None), (mla_scale, 0)] * MLA_HEADS

    w_uq_b, w_ukv_b = _relayout_w_uq(w_uq), _relayout_w_ukv(w_ukv)
    w_in_t = jnp.swapaxes(w_in, 1, 2)

    h = x.reshape(m, d_model)
    xin, w_in_l = rmsnorm(h, attn_norm[0], BF16, cast=(w_in_t, 0))
    for l in range(depth):
        pre = None if l == 0 else attn_norm[l]
        proj = matmul(xin, w_in_l, BF16, name="mm_in", w_t=True, n_out=MAIN_WIDTH, bn=MAIN_BN,
                      prenorm=pre, tile_modes=_main_tile_modes(), rope_tabs=(part_tab,), seq=seq)
        tail = matmul(xin, _tail_w_in(w_in_l), BF16, name="mm_in_tail", w_t=True, bn=TAIL_WIDTH,
                      prenorm=pre, tile_modes=_tail_tile_modes(), rope_tabs=(mla_tab,), seq=seq)

        ret, w_o_l = retention(proj, batch, seq, ret_tables, cast=(w_o, l))
        dsa = dsa_attention(proj, tail, batch, seq)

        q_full = matmul(tail, w_uq_b, BF16, layer=l, name="mm_uq", x_cols=_TAIL["cq"],
                        prenorm=mla_q_norm[l], tile_modes=uq_modes, rope_tabs=(mla_tab,), seq=seq)
        kv_up = matmul(tail, w_ukv_b, BF16, layer=l, name="mm_ukv", x_cols=_TAIL["ckv"],
                       prenorm=mla_kv_norm[l])
        mla, w_up_l = mla_attention(q_full, kv_up, tail, batch, seq, cast=(w_up, l))

        h, hb = matmul((ret, dsa, mla), w_o_l, F32, res=h, name="mm_o", emit_bf16=True)
        up, w_down_l = matmul(hb, w_up_l, BF16, act="relu2", name="mm_up", prenorm=mlp_norm[l],
                              cast=(w_down, l))
        if l + 1 < depth:
            h, xin, w_in_l = matmul(up, w_down_l, F32, res=h, bk=DOWN_BK, name="mm_down",
                                    emit_bf16=True, cast=(w_in_t, l + 1))
        else:
            h = matmul(up, w_down_l, F32, res=h, bk=DOWN_BK, name="mm_down")
    out = rmsnorm(h, final_norm, x.dtype)
    return out.reshape(batch, seq, d_model)
```

```python
import functools
import math

import numpy as np
import jax
import jax.numpy as jnp
from jax import lax
from jax.experimental import pallas as pl
from jax.experimental.pallas import tpu as pltpu

F32 = jnp.float32
BF16 = jnp.bfloat16

RET_HEADS = 8
RET_QK_DIM = 64
RET_V_DIM = 128
RET_CHUNK = 128
RET_ROPE_THETA = 10000.0
DSA_HEADS = 12
DSA_KV_HEADS = 4
DSA_HEAD_DIM = 128
IDX_HEADS = 32
IDX_HEAD_DIM = 128
DSA_TOPK = 256
PARTIAL_ROT_DIM = 32
PARTIAL_ROPE_THETA = 500000.0
MLA_HEADS = 12
MLA_Q_LORA = 768
MLA_KV_LORA = 256
MLA_NOPE_DIM = 128
MLA_ROPE_DIM = 64
MLA_V_DIM = 128
MLA_ROPE_THETA = 10000.0
NORM_EPS = 1e-6

LANE = 128
VMEM_LIMIT = 60 * 1024 * 1024

RET_OUT = RET_HEADS * RET_V_DIM
DSA_OUT = DSA_HEADS * DSA_HEAD_DIM
MLA_OUT = MLA_HEADS * MLA_V_DIM
MLA_QK_PAD = 256

_NAT = {}
_off = 0
for _name, _w in (("rq", 512), ("rk", 512), ("rv", 1024), ("rg", 1024), ("dq", 1536),
                  ("dk", 512), ("dv", 512), ("iq", 4096), ("ik", 128), ("iw", 32),
                  ("cq", 768), ("ckv", 256), ("kr", 64)):
    _NAT[_name] = (_off, _w)
    _off += _w
IN_WIDTH = _off
MAIN_WIDTH = 10240
MAIN_BN = 1024
_TAIL = {"cq": (0, 768), "ckv": (768, 256), "iw": (1024, 128), "kr": (1152, 128)}
TAIL_WIDTH = 1280
QI_BLOCK = 512
DOWN_BK = 2048
NORM_ROWS = 64
NORM_ONCE_MIN = 2 ** 21


def _cparams(sem):
    return pltpu.CompilerParams(dimension_semantics=sem, vmem_limit_bytes=VMEM_LIMIT)


def _rmsnorm_kernel(x_ref, g_ref, *rest):
    o_ref = rest[-1] if len(rest) == 1 else rest[1]
    if len(rest) == 3:
        rest[2][...] = rest[0][...].astype(BF16)
    x = x_ref[...].astype(F32)
    ms = jnp.mean(x * x, axis=-1, keepdims=True)
    o_ref[...] = (x * lax.rsqrt(ms + NORM_EPS) * g_ref[...]).astype(o_ref.dtype)


def rmsnorm(x, g, out_dtype, *, bm=256, cast=None):
    m, width = x.shape
    assert m % bm == 0
    in_specs = [pl.BlockSpec((bm, width), lambda i: (i, 0)),
                pl.BlockSpec((1, width), lambda i: (0, 0))]
    out_specs = pl.BlockSpec((bm, width), lambda i: (i, 0))
    out_shape = jax.ShapeDtypeStruct((m, width), out_dtype)
    args = [x, g.reshape(1, width).astype(F32)]
    if cast is not None:
        cast_in, cast_out, cast_shape = _cast_specs(cast, (m // bm,))
        in_specs.append(cast_in)
        args.append(cast[0])
        out_specs, out_shape = [out_specs, cast_out], [out_shape, cast_shape]
    return pl.pallas_call(
        _rmsnorm_kernel,
        grid=(m // bm,),
        in_specs=in_specs,
        out_specs=out_specs,
        out_shape=out_shape,
        compiler_params=_cparams(("arbitrary",)),
        name="rmsnorm",
    )(*args)


def _mm_kernel(*refs, nx, act, has_res, has_norm, norm_once, nk, groups, halves, ntab, has_cast,
               emit_bf16, w_t):
    refs = list(refs)
    take = lambda n=1: [refs.pop(0) for _ in range(n)]
    x_refs = take(nx)
    (w_ref,) = take()
    r_ref = take()[0] if has_res else None
    g_ref = take()[0] if has_norm else None
    tab_refs = take(3 * ntab)
    cast_in_ref = take()[0] if has_cast else None
    (o_ref,) = take()
    ob_ref = take()[0] if emit_bf16 else None
    if has_cast:
        take()[0][...] = cast_in_ref[...].astype(BF16)
    acc_ref = take()[0] if nk > 1 else None
    xn_ref = take()[0] if norm_once else None

    def store_tiles(acc, pattern):
        for t, (scale, ridx) in enumerate(pattern):
            x = acc[:, t * LANE:(t + 1) * LANE]
            if ridx is not None:
                a, bp, bm = (tab_refs[3 * ridx + k][...] for k in range(3))
                half = halves[ridx]
                x = x * a + pltpu.roll(x, half, 1) * bp + pltpu.roll(x, LANE - half, 1) * bm
            if scale != 1.0:
                x = x * scale
            o_ref[:, t * LANE:(t + 1) * LANE] = x.astype(o_ref.dtype)

    res_in_acc = nk > 1 and has_res and act is None

    def epilogue(acc):
        if act == "relu2":
            acc = jnp.square(jnp.maximum(acc, 0.0))
        if has_res and not res_in_acc:
            acc = acc + r_ref[...]
        if emit_bf16:
            ob_ref[...] = acc.astype(BF16)
        if groups is None:
            o_ref[...] = acc.astype(o_ref.dtype)
        elif len(groups) == 1:
            store_tiles(acc, groups[0][0])
        else:
            j = pl.program_id(1)
            for pattern, js in groups:
                cond = functools.reduce(jnp.logical_or, [j == jj for jj in js])
                pl.when(cond)(functools.partial(store_tiles, acc, pattern))

    def normalised(x):
        xf = x.astype(F32)
        ms = jnp.mean(xf * xf, axis=-1, keepdims=True)
        return (xf * lax.rsqrt(ms + NORM_EPS) * g_ref[...]).astype(BF16)

    if norm_once:
        @pl.when(pl.program_id(1) == 0)
        def _():
            def rows_body(r, _):
                rows = pl.ds(pl.multiple_of(r * NORM_ROWS, NORM_ROWS), NORM_ROWS)
                xn_ref[rows, :] = normalised(x_refs[0][rows, :])
                return 0
            lax.fori_loop(0, xn_ref.shape[0] // NORM_ROWS, rows_body, 0)

    if nk > 1:
        @pl.when(pl.program_id(2) == 0)
        def _():
            acc_ref[...] = r_ref[...] if res_in_acc else jnp.zeros_like(acc_ref)

    part, k0 = None, 0
    for x_ref in x_refs:
        kw = x_ref.shape[1]
        if norm_once:
            x = xn_ref[...]
        elif has_norm:
            x = normalised(x_ref[...])
        else:
            x = x_ref[...]
        if w_t:
            d = lax.dot_general(x, w_ref[...], (((1,), (1,)), ((), ())),
                                preferred_element_type=F32)
        else:
            d = jnp.dot(x, w_ref[k0:k0 + kw, :], preferred_element_type=F32)
        part = d if part is None else part + d
        k0 += kw
    if nk == 1:
        epilogue(part)
    else:
        acc_ref[...] += part

        @pl.when(pl.program_id(2) == nk - 1)
        def _():
            epilogue(acc_ref[...])


def matmul(x, w, out_dtype, *, name, layer=None, act=None, res=None, bm=1024, bn=1024, bk=4096,
           n_out=None, tile_modes=None, rope_tabs=(), seq=None, cast=None, x_cols=None,
           prenorm=None, w_t=False, emit_bf16=False):
    xs = x if isinstance(x, (tuple, list)) else (x,)
    m = xs[0].shape[0]
    kdim = sum(a.shape[1] for a in xs) if x_cols is None else x_cols[1]
    assert kdim == w.shape[-1 if w_t else -2]
    n = w.shape[-2 if w_t else -1] if n_out is None else n_out
    bm, bn, bk = min(bm, m), min(bn, n), min(bk, kdim)
    assert m % bm == 0 and n % bn == 0 and kdim % bk == 0
    nk = kdim // bk
    assert len(xs) == 1 or nk == 1
    if x_cols is not None:
        assert len(xs) == 1 and nk == 1 and x_cols[0] % kdim == 0
        xcb = x_cols[0] // kdim
        in_specs = [pl.BlockSpec((bm, kdim), lambda i, j, k: (i, xcb))]
    elif len(xs) == 1:
        in_specs = [pl.BlockSpec((bm, bk), lambda i, j, k: (i, k))]
    else:
        in_specs = [pl.BlockSpec((bm, a.shape[1]), lambda i, j, k: (i, 0)) for a in xs]
    if w_t:
        assert layer is None and len(xs) == 1
        in_specs.append(pl.BlockSpec((bn, bk), lambda i, j, k: (j, k)))
    elif layer is None:
        in_specs.append(pl.BlockSpec((bk, bn), lambda i, j, k: (k, j)))
    else:
        in_specs.append(pl.BlockSpec((None, bk, bn), lambda i, j, k: (layer, k, j)))
    args = [*xs, w]
    if res is not None:
        in_specs.append(pl.BlockSpec((bm, bn), lambda i, j, k: (i, j)))
        args.append(res)
    if prenorm is not None:
        assert nk == 1 and len(xs) == 1
        in_specs.append(pl.BlockSpec((1, kdim), lambda i, j, k: (0, 0)))
        args.append(prenorm.reshape(1, kdim).astype(F32))
    groups = None
    if tile_modes is not None:
        assert len(tile_modes) == n // LANE and seq % bm == 0
        per_block = bn // LANE
        by_pattern = {}
        for j in range(n // bn):
            by_pattern.setdefault(tuple(tile_modes[j * per_block:(j + 1) * per_block]), []).append(j)
        groups = tuple((p, tuple(js)) for p, js in by_pattern.items())
        nseq = seq // bm
        for a, bp, bmt, _ in rope_tabs:
            in_specs += [pl.BlockSpec((bm, LANE), lambda i, j, k: (i % nseq, 0))] * 3
            args += [a, bp, bmt]
    tile_spec = pl.BlockSpec((bm, bn), lambda i, j, k: (i, j))
    out_specs = [tile_spec]
    out_shape = [jax.ShapeDtypeStruct((m, n), out_dtype)]
    if emit_bf16:
        out_specs.append(tile_spec)
        out_shape.append(jax.ShapeDtypeStruct((m, n), BF16))
    norm_once = prenorm is not None and n // bn > 1 and bm * kdim >= NORM_ONCE_MIN
    ordered = cast is not None or norm_once
    sem = ("arbitrary",) * 3 if ordered else ("parallel", "parallel", "arbitrary")
    if cast is not None:
        spec_in, spec_out, shape_out = _cast_specs(cast, (m // bm, n // bn, nk))
        in_specs.append(spec_in)
        args.append(cast[0])
        out_specs.append(spec_out)
        out_shape.append(shape_out)
    scratch = [pltpu.VMEM((bm, bn), F32)] if nk > 1 else []
    if norm_once:
        scratch.append(pltpu.VMEM((bm, kdim), BF16))
    outs = pl.pallas_call(
        functools.partial(_mm_kernel, nx=len(xs), act=act, has_res=res is not None,
                          has_norm=prenorm is not None, norm_once=norm_once, nk=nk,
                          groups=groups, halves=tuple(t[3] for t in rope_tabs),
                          ntab=len(rope_tabs), has_cast=cast is not None, emit_bf16=emit_bf16,
                          w_t=w_t),
        grid=(m // bm, n // bn, nk),
        in_specs=in_specs,
        out_specs=out_specs,
        out_shape=out_shape,
        scratch_shapes=scratch,
        compiler_params=_cparams(sem),
        name=name,
    )(*args)
    return outs[0] if len(outs) == 1 else tuple(outs)


def _cast_specs(cast, grid):
    src, layer = cast
    _, rows, cols = src.shape
    nsteps = math.prod(grid)
    rb = 16 * max(1, -(-rows // (16 * nsteps)))
    while rows % rb:
        rb += 16
    nb = rows // rb

    def step(*ids):
        s = 0
        for g, i in zip(grid, ids):
            s = s * g + i
        return jnp.minimum(s, nb - 1)

    spec_in = pl.BlockSpec((None, rb, cols), lambda *ids: (layer, step(*ids), 0))
    spec_out = pl.BlockSpec((rb, cols), lambda *ids: (step(*ids), 0))
    return spec_in, spec_out, jax.ShapeDtypeStruct((rows, cols), BF16)


def _rope_tables(seq, theta, rot_dim):
    half = rot_dim // 2
    inv = jnp.exp(-math.log(theta) * jnp.arange(half, dtype=F32) * (2.0 / rot_dim))
    ang = jnp.arange(seq, dtype=F32)[:, None] * inv[None, :]
    cos, sin = jnp.cos(ang), jnp.sin(ang)
    pad = LANE - rot_dim
    a = jnp.concatenate([cos, cos, jnp.ones((seq, pad), F32)], axis=1)
    bp = jnp.concatenate([jnp.zeros((seq, half), F32), sin, jnp.zeros((seq, pad), F32)], axis=1)
    bm = jnp.concatenate([-sin, jnp.zeros((seq, half + pad), F32)], axis=1)
    return a, bp, bm, half


def _retention_kernel(cast_in_ref, q_ref, k_ref, v_ref, g_ref, cos_ref, sp_ref, sm_ref, hmask_ref,
                      idec_ref, qdec_ref, kdec_ref, cdec_ref, bmask_ref, o_ref, cast_out_ref,
                      state_ref):
    half = RET_QK_DIM // 2
    cast_out_ref[...] = cast_in_ref[...].astype(BF16)

    @pl.when(pl.program_id(0) == 0)
    def _():
        state_ref[...] = jnp.zeros_like(state_ref)

    cos, sp, sm = cos_ref[...], sp_ref[...], sm_ref[...]

    def rot(x):
        tiles = []
        for t in range(x.shape[1] // LANE):
            xt = x[:, t * LANE:(t + 1) * LANE]
            tiles.append(xt * cos + pltpu.roll(xt, half, 1) * sp
                         + pltpu.roll(xt, LANE - half, 1) * sm)
        return jnp.concatenate(tiles, axis=1)

    for b in range(q_ref.shape[0]):
        q = rot(q_ref[b].astype(F32))
        k = rot(k_ref[b].astype(F32)) * (RET_QK_DIM ** -0.5)
        v = v_ref[b].astype(F32)
        kt = k.T.astype(BF16)
        state = state_ref[b]
        cross = jnp.dot(q.astype(BF16), state.astype(BF16),
                        preferred_element_type=F32) * qdec_ref[...]
        for h in range(RET_HEADS):
            cols = slice(h * RET_V_DIM, (h + 1) * RET_V_DIM)
            qm = (q * hmask_ref[h:h + 1, :]).astype(BF16)
            scores = jnp.dot(qm, kt, preferred_element_type=F32) * idec_ref[h]
            inner = jnp.dot(scores.astype(BF16), v[:, cols].astype(BF16),
                            preferred_element_type=F32)
            o = inner + cross[:, cols]
            o = o * lax.rsqrt(jnp.mean(o * o, axis=-1, keepdims=True) + NORM_EPS)
            gate = g_ref[b, :, cols].astype(F32)
            o_ref[b, :, cols] = (gate * jax.nn.sigmoid(gate) * o).astype(o_ref.dtype)
        vd = (v * kdec_ref[...]).astype(BF16)
        upd = jnp.dot(kt, vd, preferred_element_type=F32)
        state_ref[b] = state * cdec_ref[...] + upd * bmask_ref[...]


def _retention_tables(seq):
    h, c, dv, dk = RET_HEADS, RET_CHUNK, RET_V_DIM, RET_QK_DIM
    half = dk // 2
    inv = jnp.exp(-math.log(RET_ROPE_THETA) * jnp.arange(half, dtype=F32) * (2.0 / dk))
    ang = jnp.arange(seq, dtype=F32)[:, None] * inv[None, :]
    cos1, sin1, zero = jnp.cos(ang), jnp.sin(ang), jnp.zeros((seq, half), F32)
    reps = LANE // dk
    cos = jnp.tile(jnp.concatenate([cos1, cos1], axis=1), (1, reps))
    sp = jnp.tile(jnp.concatenate([zero, sin1], axis=1), (1, reps))
    sm = jnp.tile(jnp.concatenate([-sin1, zero], axis=1), (1, reps))
    log_gamma = jnp.log1p(-jnp.exp2(-5.0 - jnp.arange(h, dtype=F32)))
    idx = jnp.arange(c, dtype=F32)
    rel = idx[:, None] - idx[None, :]
    idec = jnp.where(rel[None] >= 0,
                     jnp.exp(jnp.maximum(rel, 0.0)[None] * log_gamma[:, None, None]), 0.0)
    qdec = jnp.exp((idx + 1.0)[:, None] * log_gamma[None, :])
    kdec = jnp.exp((c - 1.0 - idx)[:, None] * log_gamma[None, :])
    cdec = jnp.exp(c * log_gamma)[None, :]
    rep = lambda a: jnp.repeat(a, dv, axis=1)
    lane_head = np.arange(h * dk) // dk
    hmask = jnp.asarray(lane_head[None, :] == np.arange(h)[:, None], F32)
    bmask = jnp.asarray(lane_head[:, None] == (np.arange(h * dv) // dv)[None, :], F32)
    return cos, sp, sm, hmask, idec, rep(qdec), rep(kdec), rep(cdec), bmask


def retention(proj, batch, seq, tables, cast):
    c = RET_CHUNK
    nch = seq // c
    qw = RET_HEADS * RET_QK_DIM
    col = lambda name: _NAT[name][0] // _NAT[name][1]
    const2 = lambda n: (0, 0)
    cos, sp, sm, hmask, idec, qdec, kdec, cdec, bmask = tables
    tab = pl.BlockSpec((c, LANE), lambda n: (n, 0))
    cast_in, cast_out, cast_shape = _cast_specs(cast, (nch,))
    proj3 = proj.reshape(batch, seq, proj.shape[1])
    seg = lambda name, width: pl.BlockSpec((batch, c, width), lambda n: (0, n, col(name)))
    out, w_cast = pl.pallas_call(
        _retention_kernel,
        grid=(nch,),
        in_specs=[cast_in, seg("rq", qw), seg("rk", qw), seg("rv", RET_OUT), seg("rg", RET_OUT),
                  tab, tab, tab,
                  pl.BlockSpec(hmask.shape, const2),
                  pl.BlockSpec(idec.shape, lambda n: (0, 0, 0)),
                  pl.BlockSpec(qdec.shape, const2),
                  pl.BlockSpec(kdec.shape, const2),
                  pl.BlockSpec(cdec.shape, const2),
                  pl.BlockSpec(bmask.shape, const2)],
        out_specs=[pl.BlockSpec((batch, c, RET_OUT), lambda n: (0, n, 0)), cast_out],
        out_shape=[jax.ShapeDtypeStruct((batch, seq, RET_OUT), BF16), cast_shape],
        scratch_shapes=[pltpu.VMEM((batch, qw, RET_OUT), F32)],
        compiler_params=_cparams(("arbitrary",)),
        name="retention",
    )(cast[0], proj3, proj3, proj3, proj3, cos, sp, sm, hmask, idec, qdec, kdec, cdec, bmask)
    return out.reshape(batch * seq, RET_OUT), w_cast


LOG2E = math.log2(math.e)
M_INIT = float(np.finfo(np.float32).min)


def _flash_update(s, v_c, m_ref, l_ref, acc_ref, h, rows=slice(None)):
    reps = s.shape[1] // LANE

    def fold(x, op):
        tiles = [x[:, r * LANE:(r + 1) * LANE] for r in range(reps)]
        while len(tiles) > 1:
            tiles = [op(a, b) for a, b in zip(tiles[::2], tiles[1::2])]
        return tiles[0]

    m_prev = m_ref[h, rows, :]
    m_new = jnp.maximum(m_prev, jnp.max(fold(s, jnp.maximum), axis=1, keepdims=True))
    p = jnp.exp2(s - jnp.tile(m_new, (1, reps)))
    alpha = jnp.exp2(m_prev - m_new)
    l_ref[h, rows, :] = alpha * l_ref[h, rows, :] + jnp.sum(fold(p, jnp.add), axis=1,
                                                            keepdims=True)
    m_ref[h, rows, :] = m_new
    acc_ref[h, rows, :] = acc_ref[h, rows, :] * alpha + jnp.dot(p.astype(BF16), v_c,
                                                                preferred_element_type=F32)


INT_MIN = -2 ** 31
ROW_BLOCK = 64
PASSES_PER_CHECK = 4
COUNT_ROWS = 32
KV_PER_LOOP = 4


def _dsa_kernel(*refs, tq, topk, nqi):
    qi_refs = refs[:nqi]
    (w_ref, kidx_ref, q_ref, k_ref, v_ref, o_ref,
     wb_ref, lg_ref, key_ref, bias_ref, qs_ref, m_ref, l_ref, acc_ref) = refs[nqi:]
    tk = tq
    i = pl.program_id(1)
    nchunks = i + 1
    reps = tk // LANE
    group = DSA_HEADS // DSA_KV_HEADS
    d = DSA_HEAD_DIM
    heads_per_ref = IDX_HEADS // nqi

    w = w_ref[...].astype(F32) * (IDX_HEADS ** -0.5)
    for h in range(IDX_HEADS):
        wb_ref[h] = jnp.broadcast_to(w[:, h:h + 1], (tq, LANE))

    def score_chunk(c, diagonal):
        r0 = pl.multiple_of(c * tk, tk)
        kc = kidx_ref[pl.ds(r0, tk), :]
        for h in range(IDX_HEADS):
            r, t = divmod(h, heads_per_ref)
            lg_ref[h] = lax.dot_general(qi_refs[r][:, t * LANE:(t + 1) * LANE], kc,
                                        (((1,), (1,)), ((), ())), preferred_element_type=F32)
        for rb in range(tq // ROW_BLOCK):
            q0 = rb * ROW_BLOCK
            rows = slice(q0, q0 + ROW_BLOCK)
            acc = jnp.zeros((ROW_BLOCK, tk), F32)
            for h in range(IDX_HEADS):
                acc = acc + jnp.maximum(lg_ref[h, rows, :], 0.0) * jnp.tile(wb_ref[h, rows, :],
                                                                            (1, reps))
            bits = pltpu.bitcast(acc, jnp.int32)
            key = bits ^ (lax.shift_right_arithmetic(bits, 31) & 0x7FFFFFFF)
            if diagonal:
                kpos = lax.broadcasted_iota(jnp.int32, (ROW_BLOCK, tk), 1)
                qpos = q0 + lax.broadcasted_iota(jnp.int32, (ROW_BLOCK, tk), 0)
                key = jnp.where(kpos <= qpos, key, INT_MIN)
            key_ref[c, rows, :] = key
        bias_ref[c] = pltpu.bitcast(key_ref[c].T, F32)

    def score_body(c, _):
        score_chunk(c, False)
        return 0

    lax.fori_loop(0, i, score_body, 0)
    score_chunk(i, True)

    def count_ge(cand):
        def body(c, cnt):
            ge = (pltpu.bitcast(bias_ref[c], jnp.int32) >= cand).astype(jnp.int32)
            return cnt + jnp.sum(ge.reshape(tk // COUNT_ROWS, COUNT_ROWS, tq), axis=0)
        cnt = lax.fori_loop(0, nchunks, body, jnp.zeros((COUNT_ROWS, tq), jnp.int32))
        return jnp.sum(cnt, axis=0, keepdims=True)

    def bit_cond(state):
        b, _, _, excess = state
        return jnp.logical_and(b < 32, excess > 0)

    def bit_body(state):
        b, t, cnt, _ = state
        for u in range(PASSES_PER_CHECK):
            cand = t ^ lax.shift_left(jnp.int32(1), 31 - (b + u))
            n = count_ge(cand)
            take = n >= topk
            t, cnt = jnp.where(take, cand, t), jnp.where(take, n, cnt)
        return b + PASSES_PER_CHECK, t, cnt, jnp.max(cnt) - topk

    t0 = jnp.full((1, tq), INT_MIN, jnp.int32)
    total = nchunks * tk
    c0 = jnp.full((1, tq), total, jnp.int32)
    _, thr_row, _, excess = lax.while_loop(bit_cond, bit_body,
                                           (jnp.int32(0), t0, c0, total - topk))
    ties = excess > 0

    @pl.when(jnp.logical_not(ties))
    def _():
        thr_min = jnp.maximum(thr_row, INT_MIN + 1)
        thr = jnp.broadcast_to(thr_min, (LANE, tq)).T
        thr = jnp.tile(thr, (1, reps))

        def bias_body(c, _):
            bias_ref[c] = jnp.where(key_ref[c] >= thr, 0.0, -jnp.inf)
            return 0

        lax.fori_loop(0, nchunks, bias_body, 0)

    @pl.when(ties)
    def _():
        def above_body(c, acc):
            gt = (pltpu.bitcast(bias_ref[c], jnp.int32) > thr_row).astype(jnp.int32)
            return acc + jnp.sum(gt.reshape(tk // COUNT_ROWS, COUNT_ROWS, tq), axis=0)
        above = lax.fori_loop(0, nchunks, above_body, jnp.zeros((COUNT_ROWS, tq), jnp.int32))
        need = (topk - jnp.sum(above, axis=0, keepdims=True)).astype(F32)
        lower = (lax.broadcasted_iota(jnp.int32, (tk, tk), 1)
                 < lax.broadcasted_iota(jnp.int32, (tk, tk), 0)).astype(BF16)

        def tie_body(c, before):
            kt = pltpu.bitcast(bias_ref[c], jnp.int32)
            eq = kt == thr_row
            eq_b = jnp.where(eq, 1.0, 0.0).astype(BF16)
            rank = jnp.dot(lower, eq_b, preferred_element_type=F32) + before
            sel = jnp.logical_or(kt > thr_row, jnp.logical_and(eq, rank < need))
            sel = jnp.logical_and(sel, kt > INT_MIN)
            bias_ref[c] = jnp.where(sel, 0.0, -jnp.inf).T
            return before + jnp.sum(eq_b.astype(F32), axis=0, keepdims=True)

        lax.fori_loop(0, nchunks, tie_body, jnp.zeros((1, tq), F32))

    @pl.when(nchunks < key_ref.shape[0])
    def _():
        bias_ref[nchunks] = jnp.full((tq, tk), -jnp.inf, F32)

    hpl = KV_PER_LOOP * group
    for g0 in range(0, DSA_KV_HEADS, KV_PER_LOOP):
        for hh in range(hpl):
            h = g0 * group + hh
            qs_ref[hh * tq:(hh + 1) * tq, :] = q_ref[:, h * d:(h + 1) * d]
        m_ref[...] = jnp.full_like(m_ref, M_INIT)
        l_ref[...] = jnp.zeros_like(l_ref)
        acc_ref[...] = jnp.zeros_like(acc_ref)

        def att_body(c2, _, g0=g0):
            r0 = pl.multiple_of(c2 * 2 * tk, 2 * tk)
            bias = jnp.concatenate([bias_ref[2 * c2], bias_ref[2 * c2 + 1]], axis=1)
            for gg in range(KV_PER_LOOP):
                g = g0 + gg
                kc = k_ref[pl.ds(r0, 2 * tk), g * d:(g + 1) * d]
                vc = v_ref[pl.ds(r0, 2 * tk), g * d:(g + 1) * d]
                s_all = lax.dot_general(qs_ref[gg * group * tq:(gg + 1) * group * tq, :], kc,
                                        (((1,), (1,)), ((), ())), preferred_element_type=F32)
                _flash_update(s_all + jnp.tile(bias, (group, 1)), vc, m_ref, l_ref, acc_ref, gg)
            return 0

        lax.fori_loop(0, (nchunks + 1) // 2, att_body, 0)
        for hh in range(hpl):
            h = g0 * group + hh
            gg, rows = hh // group, slice((hh % group) * tq, (hh % group + 1) * tq)
            o_ref[:, h * d:(h + 1) * d] = (acc_ref[gg, rows, :]
                                           / l_ref[gg, rows, :]).astype(o_ref.dtype)


def dsa_attention(proj, tail, batch, seq, *, tq=256):
    nq = seq // tq
    assert nq % 2 == 0
    topk = min(DSA_TOPK, seq // 4)
    hpl = KV_PER_LOOP * DSA_HEADS // DSA_KV_HEADS
    gtq = DSA_HEADS // DSA_KV_HEADS * tq
    kvw = DSA_KV_HEADS * DSA_HEAD_DIM
    iqw = IDX_HEADS * IDX_HEAD_DIM
    nqi = iqw // QI_BLOCK
    col = lambda name, width: _NAT[name][0] // width
    qi_specs = [pl.BlockSpec((tq, QI_BLOCK), lambda b, i, r=r: (b * nq + i, col("iq", QI_BLOCK) + r))
                for r in range(nqi)]
    return pl.pallas_call(
        functools.partial(_dsa_kernel, tq=tq, topk=topk, nqi=nqi),
        grid=(batch, nq),
        in_specs=qi_specs + [
            pl.BlockSpec((tq, LANE), lambda b, i: (b * nq + i, _TAIL["iw"][0] // LANE)),
            pl.BlockSpec((seq, IDX_HEAD_DIM), lambda b, i: (b, col("ik", IDX_HEAD_DIM))),
            pl.BlockSpec((tq, DSA_OUT), lambda b, i: (b * nq + i, col("dq", DSA_OUT))),
            pl.BlockSpec((seq, kvw), lambda b, i: (b, col("dk", kvw))),
            pl.BlockSpec((seq, kvw), lambda b, i: (b, col("dv", kvw)))],
        out_specs=pl.BlockSpec((tq, DSA_OUT), lambda b, i: (b * nq + i, 0)),
        out_shape=jax.ShapeDtypeStruct((batch * seq, DSA_OUT), BF16),
        scratch_shapes=[pltpu.VMEM((IDX_HEADS, tq, LANE), F32),
                        pltpu.VMEM((IDX_HEADS, tq, tq), F32),
                        pltpu.VMEM((nq, tq, tq), jnp.int32),
                        pltpu.VMEM((nq, tq, tq), F32),
                        pltpu.VMEM((hpl * tq, DSA_HEAD_DIM), BF16),
                        pltpu.VMEM((KV_PER_LOOP, gtq, LANE), F32),
                        pltpu.VMEM((KV_PER_LOOP, gtq, LANE), F32),
                        pltpu.VMEM((KV_PER_LOOP, gtq, DSA_HEAD_DIM), F32)],
        compiler_params=_cparams(("parallel", "arbitrary")),
        name="dsa_attention",
    )(*([proj] * nqi), tail, proj, proj, proj, proj)


def _mla_kernel(cast_in_ref, q_ref, kn_ref, kpe_ref, v_ref, o_ref, cast_out_ref, m_ref, l_ref,
                acc_ref, *, tq, hb):
    tk = tq
    i = pl.program_id(2)
    cast_out_ref[...] = cast_in_ref[...].astype(BF16)
    m_ref[...] = jnp.full_like(m_ref, M_INIT)
    l_ref[...] = jnp.zeros_like(l_ref)
    acc_ref[...] = jnp.zeros_like(acc_ref)
    half = tq // 2

    def scores(h, rows, r0, nkeys):
        cols = slice(h * MLA_NOPE_DIM, (h + 1) * MLA_NOPE_DIM)
        kf = jnp.concatenate([kn_ref[pl.ds(r0, nkeys), cols], kpe_ref[pl.ds(r0, nkeys), :]],
                             axis=1)
        s = lax.dot_general(q_ref[rows, h * MLA_QK_PAD:(h + 1) * MLA_QK_PAD], kf,
                            (((1,), (1,)), ((), ())), preferred_element_type=F32)
        return s, v_ref[pl.ds(r0, nkeys), cols]

    def full_step(r0, nkeys):
        for h in range(hb):
            s, vc = scores(h, slice(None), r0, nkeys)
            _flash_update(s, vc, m_ref, l_ref, acc_ref, h)

    def pair_body(c2, _):
        full_step(pl.multiple_of(c2 * 2 * tk, 2 * tk), 2 * tk)
        return 0

    lax.fori_loop(0, i // 2, pair_body, 0)

    @pl.when(i % 2 == 1)
    def _():
        full_step(pl.multiple_of((i - 1) * tk, tk), tk)

    r0 = pl.multiple_of(i * tk, tk)
    for h in range(hb):
        s, vc = scores(h, slice(None), r0, half)
        keep = (lax.broadcasted_iota(jnp.int32, (tq, half), 1)
                <= lax.broadcasted_iota(jnp.int32, (tq, half), 0))
        _flash_update(jnp.where(keep, s, -jnp.inf), vc, m_ref, l_ref, acc_ref, h)
    for h in range(hb):
        s, vc = scores(h, slice(half, tq), r0 + half, half)
        keep = (lax.broadcasted_iota(jnp.int32, (half, half), 1)
                <= lax.broadcasted_iota(jnp.int32, (half, half), 0))
        _flash_update(jnp.where(keep, s, -jnp.inf), vc, m_ref, l_ref, acc_ref, h,
                      rows=slice(half, tq))
    for h in range(hb):
        o_ref[:, h * MLA_V_DIM:(h + 1) * MLA_V_DIM] = (acc_ref[h] / l_ref[h]).astype(o_ref.dtype)


def mla_attention(q, kv, tail, batch, seq, cast, *, tq=512, hb=4):
    nq = seq // tq
    ng = MLA_HEADS // hb
    cast_in, cast_out, cast_shape = _cast_specs(cast, (batch, ng, nq))
    return pl.pallas_call(
        functools.partial(_mla_kernel, tq=tq, hb=hb),
        grid=(batch, ng, nq),
        in_specs=[cast_in,
                  pl.BlockSpec((tq, hb * MLA_QK_PAD), lambda b, g, i: (b * nq + i, g)),
                  pl.BlockSpec((seq, hb * MLA_NOPE_DIM), lambda b, g, i: (b, g)),
                  pl.BlockSpec((seq, LANE), lambda b, g, i: (b, _TAIL["kr"][0] // LANE)),
                  pl.BlockSpec((seq, hb * MLA_V_DIM), lambda b, g, i: (b, ng + g))],
        out_specs=[pl.BlockSpec((tq, hb * MLA_V_DIM), lambda b, g, i: (b * nq + i, g)), cast_out],
        out_shape=[jax.ShapeDtypeStruct((batch * seq, MLA_OUT), BF16), cast_shape],
        scratch_shapes=[pltpu.VMEM((hb, tq, LANE), F32), pltpu.VMEM((hb, tq, LANE), F32),
                        pltpu.VMEM((hb, tq, MLA_V_DIM), F32)],
        compiler_params=_cparams(("arbitrary", "arbitrary", "arbitrary")),
        name="mla_attention",
    )(cast[0], q, kv, tail, kv)


def _tail_w_in(wt):
    parts, pos = [], 0
    for name in sorted(_TAIL, key=lambda s: _TAIL[s][0]):
        off, width = _TAIL[name]
        assert off == pos
        a = wt[_NAT[name][0]:_NAT[name][0] + _NAT[name][1]]
        parts.append(a)
        if a.shape[0] < width:
            parts.append(jnp.zeros((width - a.shape[0], wt.shape[1]), wt.dtype))
        pos += width
    assert pos == TAIL_WIDTH
    return jnp.concatenate(parts, axis=0)


def _relayout_w_uq(w):
    dep, r = w.shape[:2]
    a = w.reshape(dep, r, MLA_HEADS, MLA_NOPE_DIM + MLA_ROPE_DIM)
    a = jnp.pad(a, ((0, 0), (0, 0), (0, 0), (0, MLA_QK_PAD - a.shape[3])))
    return a.reshape(dep, r, MLA_HEADS * MLA_QK_PAD).astype(BF16)


def _relayout_w_ukv(w):
    dep, r = w.shape[:2]
    a = w.reshape(dep, r, MLA_HEADS, 2, MLA_NOPE_DIM).transpose(0, 1, 3, 2, 4)
    return a.reshape(dep, r, 2 * MLA_HEADS * MLA_NOPE_DIM).astype(BF16)


def _main_tile_modes():
    modes = [(1.0, None)] * (MAIN_WIDTH // LANE)

    def fill(name, mode):
        off, width = _NAT[name]
        for t in range(off // LANE, (off + width) // LANE):
            modes[t] = mode

    fill("dq", (DSA_HEAD_DIM ** -0.5 * LOG2E, 0))
    fill("dk", (1.0, 0))
    fill("iq", (IDX_HEAD_DIM ** -0.5, 0))
    fill("ik", (1.0, 0))
    return modes


def _tail_tile_modes():
    modes = [(1.0, None)] * (TAIL_WIDTH // LANE)
    modes[_TAIL["kr"][0] // LANE] = (1.0, 0)
    return modes


def kernel(x, attn_norm, w_in, mla_q_norm, mla_kv_norm, w_uq, w_ukv, w_o, mlp_norm, w_up,
           w_down, final_norm):
    batch, seq, d_model = x.shape
    depth = w_in.shape[0]
    m = batch * seq

    ret_tables = _retention_tables(seq)
    part_tab = _rope_tables(seq, PARTIAL_ROPE_THETA, PARTIAL_ROT_DIM)
    mla_tab = _rope_tables(seq, MLA_ROPE_THETA, MLA_ROPE_DIM)
    mla_scale = (MLA_NOPE_DIM + MLA_ROPE_DIM) ** -0.5 * LOG2E
    uq_modes = [(mla_scale, None), (mla_scale, 0)] * MLA_HEADS

    w_uq_b, w_ukv_b = _relayout_w_uq(w_uq), _relayout_w_ukv(w_ukv)
    w_in_t = jnp.swapaxes(w_in, 1, 2)

    h = x.reshape(m, d_model)
    xin, w_in_l = rmsnorm(h, attn_norm[0], BF16, cast=(w_in_t, 0))
    for l in range(depth):
        pre = None if l == 0 else attn_norm[l]
        proj = matmul(xin, w_in_l, BF16, name="mm_in", w_t=True, n_out=MAIN_WIDTH, bn=MAIN_BN,
                      prenorm=pre, tile_modes=_main_tile_modes(), rope_tabs=(part_tab,), seq=seq)
        tail = matmul(xin, _tail_w_in(w_in_l), BF16, name="mm_in_tail", w_t=True, bn=TAIL_WIDTH,
                      prenorm=pre, tile_modes=_tail_tile_modes(), rope_tabs=(mla_tab,), seq=seq)

        ret, w_o_l = retention(proj, batch, seq, ret_tables, cast=(w_o, l))
        dsa = dsa_attention(proj, tail, batch, seq)

        q_full = matmul(tail, w_uq_b, BF16, layer=l, name="mm_uq", x_cols=_TAIL["cq"],
                        prenorm=mla_q_norm[l], tile_modes=uq_modes, rope_tabs=(mla_tab,), seq=seq)
        kv_up = matmul(tail, w_ukv_b, BF16, layer=l, name="mm_ukv", x_cols=_TAIL["ckv"],
                       prenorm=mla_kv_norm[l])
        mla, w_up_l = mla_attention(q_full, kv_up, tail, batch, seq, cast=(w_up, l))

        h, hb = matmul((ret, dsa, mla), w_o_l, F32, res=h, name="mm_o", emit_bf16=True)
        up, w_down_l = matmul(hb, w_up_l, BF16, act="relu2", name="mm_up", prenorm=mlp_norm[l],
                              cast=(w_down, l))
        if l + 1 < depth:
            h, xin, w_in_l = matmul(up, w_down_l, F32, res=h, bk=DOWN_BK, name="mm_down",
                                    emit_bf16=True, cast=(w_in_t, l + 1))
        else:
            h = matmul(up, w_down_l, F32, res=h, bk=DOWN_BK, name="mm_down")
    out = rmsnorm(h, final_norm, x.dtype)
    return out.reshape(batch, seq, d_model)
```

```python
import functools
import math

import numpy as np
import jax
import jax.numpy as jnp
from jax import lax
from jax.experimental import pallas as pl
from jax.experimental.pallas import tpu as pltpu

F32 = jnp.float32
BF16 = jnp.bfloat16

RET_HEADS = 8
RET_QK_DIM = 64
RET_V_DIM = 128
RET_CHUNK = 128
RET_ROPE_THETA = 10000.0
DSA_HEADS = 12
DSA_KV_HEADS = 4
DSA_HEAD_DIM = 128
IDX_HEADS = 32
IDX_HEAD_DIM = 128
DSA_TOPK = 256
PARTIAL_ROT_DIM = 32
PARTIAL_ROPE_THETA = 500000.0
MLA_HEADS = 12
MLA_Q_LORA = 768
MLA_KV_LORA = 256
MLA_NOPE_DIM = 128
MLA_ROPE_DIM = 64
MLA_V_DIM = 128
MLA_ROPE_THETA = 10000.0
NORM_EPS = 1e-6

LANE = 128
VMEM_LIMIT = 60 * 1024 * 1024

RET_OUT = RET_HEADS * RET_V_DIM
DSA_OUT = DSA_HEADS * DSA_HEAD_DIM
MLA_OUT = MLA_HEADS * MLA_V_DIM
MLA_QK_PAD = 256

_NAT = {}
_off = 0
for _name, _w in (("rq", 512), ("rk", 512), ("rv", 1024), ("rg", 1024), ("dq", 1536),
                  ("dk", 512), ("dv", 512), ("iq", 4096), ("ik", 128), ("iw", 32),
                  ("cq", 768), ("ckv", 256), ("kr", 64)):
    _NAT[_name] = (_off, _w)
    _off += _w
IN_WIDTH = _off
MAIN_WIDTH = 10240
MAIN_BN = 1024
_TAIL = {"cq": (0, 768), "ckv": (768, 256), "iw": (1024, 128), "kr": (1152, 128)}
TAIL_WIDTH = 1280
QI_BLOCK = 512
DOWN_BK = 2048
NORM_ROWS = 64
NORM_ONCE_MIN = 2 ** 21


def _cparams(sem):
    return pltpu.CompilerParams(dimension_semantics=sem, vmem_limit_bytes=VMEM_LIMIT)


def _rmsnorm_kernel(x_ref, g_ref, *rest):
    o_ref = rest[-1] if len(rest) == 1 else rest[1]
    if len(rest) == 3:
        rest[2][...] = rest[0][...].astype(BF16)
    x = x_ref[...].astype(F32)
    ms = jnp.mean(x * x, axis=-1, keepdims=True)
    o_ref[...] = (x * lax.rsqrt(ms + NORM_EPS) * g_ref[...]).astype(o_ref.dtype)


def rmsnorm(x, g, out_dtype, *, bm=256, cast=None):
    m, width = x.shape
    assert m % bm == 0
    in_specs = [pl.BlockSpec((bm, width), lambda i: (i, 0)),
                pl.BlockSpec((1, width), lambda i: (0, 0))]
    out_specs = pl.BlockSpec((bm, width), lambda i: (i, 0))
    out_shape = jax.ShapeDtypeStruct((m, width), out_dtype)
    args = [x, g.reshape(1, width).astype(F32)]
    if cast is not None:
        cast_in, cast_out, cast_shape = _cast_specs(cast, (m // bm,))
        in_specs.append(cast_in)
        args.append(cast[0])
        out_specs, out_shape = [out_specs, cast_out], [out_shape, cast_shape]
    return pl.pallas_call(
        _rmsnorm_kernel,
        grid=(m // bm,),
        in_specs=in_specs,
        out_specs=out_specs,
        out_shape=out_shape,
        compiler_params=_cparams(("arbitrary",)),
        name="rmsnorm",
    )(*args)


def _mm_kernel(*refs, nx, act, has_res, has_norm, norm_once, nk, groups, halves, ntab, has_cast,
               emit_bf16, w_t):
    refs = list(refs)
    take = lambda n=1: [refs.pop(0) for _ in range(n)]
    x_refs = take(nx)
    (w_ref,) = take()
    r_ref = take()[0] if has_res else None
    g_ref = take()[0] if has_norm else None
    tab_refs = take(3 * ntab)
    cast_in_ref = take()[0] if has_cast else None
    (o_ref,) = take()
    ob_ref = take()[0] if emit_bf16 else None
    if has_cast:
        take()[0][...] = cast_in_ref[...].astype(BF16)
    acc_ref = take()[0] if nk > 1 else None
    xn_ref = take()[0] if norm_once else None

    def store_tiles(acc, pattern):
        for t, (scale, ridx) in enumerate(pattern):
            x = acc[:, t * LANE:(t + 1) * LANE]
            if ridx is not None:
                a, bp, bm = (tab_refs[3 * ridx + k][...] for k in range(3))
                half = halves[ridx]
                x = x * a + pltpu.roll(x, half, 1) * bp + pltpu.roll(x, LANE - half, 1) * bm
            if scale != 1.0:
                x = x * scale
            o_ref[:, t * LANE:(t + 1) * LANE] = x.astype(o_ref.dtype)

    res_in_acc = nk > 1 and has_res and act is None

    def epilogue(acc):
        if act == "relu2":
            acc = jnp.square(jnp.maximum(acc, 0.0))
        if has_res and not res_in_acc:
            acc = acc + r_ref[...]
        if emit_bf16:
            ob_ref[...] = acc.astype(BF16)
        if groups is None:
            o_ref[...] = acc.astype(o_ref.dtype)
        elif len(groups) == 1:
            store_tiles(acc, groups[0][0])
        else:
            j = pl.program_id(1)
            for pattern, js in groups:
                cond = functools.reduce(jnp.logical_or, [j == jj for jj in js])
                pl.when(cond)(functools.partial(store_tiles, acc, pattern))

    def normalised(x):
        xf = x.astype(F32)
        ms = jnp.mean(xf * xf, axis=-1, keepdims=True)
        return (xf * lax.rsqrt(ms + NORM_EPS) * g_ref[...]).astype(BF16)

    if norm_once:
        @pl.when(pl.program_id(1) == 0)
        def _():
            def rows_body(r, _):
                rows = pl.ds(pl.multiple_of(r * NORM_ROWS, NORM_ROWS), NORM_ROWS)
                xn_ref[rows, :] = normalised(x_refs[0][rows, :])
                return 0
            lax.fori_loop(0, xn_ref.shape[0] // NORM_ROWS, rows_body, 0)

    if nk > 1:
        @pl.when(pl.program_id(2) == 0)
        def _():
            acc_ref[...] = r_ref[...] if res_in_acc else jnp.zeros_like(acc_ref)

    part, k0 = None, 0
    for x_ref in x_refs:
        kw = x_ref.shape[1]
        if norm_once:
            x = xn_ref[...]
        elif has_norm:
            x = normalised(x_ref[...])
        else:
            x = x_ref[...]
        if w_t:
            d = lax.dot_general(x, w_ref[...], (((1,), (1,)), ((), ())),
                                preferred_element_type=F32)
        else:
            d = jnp.dot(x, w_ref[k0:k0 + kw, :], preferred_element_type=F32)
        part = d if part is None else part + d
        k0 += kw
    if nk == 1:
        epilogue(part)
    else:
        acc_ref[...] += part

        @pl.when(pl.program_id(2) == nk - 1)
        def _():
            epilogue(acc_ref[...])


def matmul(x, w, out_dtype, *, name, layer=None, act=None, res=None, bm=1024, bn=1024, bk=4096,
           n_out=None, tile_modes=None, rope_tabs=(), seq=None, cast=None, x_cols=None,
           prenorm=None, w_t=False, emit_bf16=False):
    xs = x if isinstance(x, (tuple, list)) else (x,)
    m = xs[0].shape[0]
    kdim = sum(a.shape[1] for a in xs) if x_cols is None else x_cols[1]
    assert kdim == w.shape[-1 if w_t else -2]
    n = w.shape[-2 if w_t else -1] if n_out is None else n_out
    bm, bn, bk = min(bm, m), min(bn, n), min(bk, kdim)
    assert m % bm == 0 and n % bn == 0 and kdim % bk == 0
    nk = kdim // bk
    assert len(xs) == 1 or nk == 1
    if x_cols is not None:
        assert len(xs) == 1 and nk == 1 and x_cols[0] % kdim == 0
        xcb = x_cols[0] // kdim
        in_specs = [pl.BlockSpec((bm, kdim), lambda i, j, k: (i, xcb))]
    elif len(xs) == 1:
        in_specs = [pl.BlockSpec((bm, bk), lambda i, j, k: (i, k))]
    else:
        in_specs = [pl.BlockSpec((bm, a.shape[1]), lambda i, j, k: (i, 0)) for a in xs]
    if w_t:
        assert layer is None and len(xs) == 1
        in_specs.append(pl.BlockSpec((bn, bk), lambda i, j, k: (j, k)))
    elif layer is None:
        in_specs.append(pl.BlockSpec((bk, bn), lambda i, j, k: (k, j)))
    else:
        in_specs.append(pl.BlockSpec((None, bk, bn), lambda i, j, k: (layer, k, j)))
    args = [*xs, w]
    if res is not None:
        in_specs.append(pl.BlockSpec((bm, bn), lambda i, j, k: (i, j)))
        args.append(res)
    if prenorm is not None:
        assert nk == 1 and len(xs) == 1
        in_specs.append(pl.BlockSpec((1, kdim), lambda i, j, k: (0, 0)))
        args.append(prenorm.reshape(1, kdim).astype(F32))
    groups = None
    if tile_modes is not None:
        assert len(tile_modes) == n // LANE and seq % bm == 0
        per_block = bn // LANE
        by_pattern = {}
        for j in range(n // bn):
            by_pattern.setdefault(tuple(tile_modes[j * per_block:(j + 1) * per_block]), []).append(j)
        groups = tuple((p, tuple(js)) for p, js in by_pattern.items())
        nseq = seq // bm
        for a, bp, bmt, _ in rope_tabs:
            in_specs += [pl.BlockSpec((bm, LANE), lambda i, j, k: (i % nseq, 0))] * 3
            args += [a, bp, bmt]
    tile_spec = pl.BlockSpec((bm, bn), lambda i, j, k: (i, j))
    out_specs = [tile_spec]
    out_shape = [jax.ShapeDtypeStruct((m, n), out_dtype)]
    if emit_bf16:
        out_specs.append(tile_spec)
        out_shape.append(jax.ShapeDtypeStruct((m, n), BF16))
    norm_once = prenorm is not None and n // bn > 1 and bm * kdim >= NORM_ONCE_MIN
    ordered = cast is not None or norm_once
    sem = ("arbitrary",) * 3 if ordered else ("parallel", "parallel", "arbitrary")
    if cast is not None:
        spec_in, spec_out, shape_out = _cast_specs(cast, (m // bm, n // bn, nk))
        in_specs.append(spec_in)
        args.append(cast[0])
        out_specs.append(spec_out)
        out_shape.append(shape_out)
    scratch = [pltpu.VMEM((bm, bn), F32)] if nk > 1 else []
    if norm_once:
        scratch.append(pltpu.VMEM((bm, kdim), BF16))
    outs = pl.pallas_call(
        functools.partial(_mm_kernel, nx=len(xs), act=act, has_res=res is not None,
                          has_norm=prenorm is not None, norm_once=norm_once, nk=nk,
                          groups=groups, halves=tuple(t[3] for t in rope_tabs),
                          ntab=len(rope_tabs), has_cast=cast is not None, emit_bf16=emit_bf16,
                          w_t=w_t),
        grid=(m // bm, n // bn, nk),
        in_specs=in_specs,
        out_specs=out_specs,
        out_shape=out_shape,
        scratch_shapes=scratch,
        compiler_params=_cparams(sem),
        name=name,
    )(*args)
    return outs[0] if len(outs) == 1 else tuple(outs)


def _cast_specs(cast, grid):
    src, layer = cast
    _, rows, cols = src.shape
    nsteps = math.prod(grid)
    rb = 16 * max(1, -(-rows // (16 * nsteps)))
    while rows % rb:
        rb += 16
    nb = rows // rb

    def step(*ids):
        s = 0
        for g, i in zip(grid, ids):
            s = s * g + i
        return jnp.minimum(s, nb - 1)

    spec_in = pl.BlockSpec((None, rb, cols), lambda *ids: (layer, step(*ids), 0))
    spec_out = pl.BlockSpec((rb, cols), lambda *ids: (step(*ids), 0))
    return spec_in, spec_out, jax.ShapeDtypeStruct((rows, cols), BF16)


def _rope_tables(seq, theta, rot_dim):
    half = rot_dim // 2
    inv = jnp.exp(-math.log(theta) * jnp.arange(half, dtype=F32) * (2.0 / rot_dim))
    ang = jnp.arange(seq, dtype=F32)[:, None] * inv[None, :]
    cos, sin = jnp.cos(ang), jnp.sin(ang)
    pad = LANE - rot_dim
    a = jnp.concatenate([cos, cos, jnp.ones((seq, pad), F32)], axis=1)
    bp = jnp.concatenate([jnp.zeros((seq, half), F32), sin, jnp.zeros((seq, pad), F32)], axis=1)
    bm = jnp.concatenate([-sin, jnp.zeros((seq, half + pad), F32)], axis=1)
    return a, bp, bm, half


def _retention_kernel(cast_in_ref, q_ref, k_ref, v_ref, g_ref, cos_ref, sp_ref, sm_ref, hmask_ref,
                      idec_ref, qdec_ref, kdec_ref, cdec_ref, bmask_ref, o_ref, cast_out_ref,
                      state_ref):
    half = RET_QK_DIM // 2
    cast_out_ref[...] = cast_in_ref[...].astype(BF16)

    @pl.when(pl.program_id(0) == 0)
    def _():
        state_ref[...] = jnp.zeros_like(state_ref)

    cos, sp, sm = cos_ref[...], sp_ref[...], sm_ref[...]

    def rot(x):
        tiles = []
        for t in range(x.shape[1] // LANE):
            xt = x[:, t * LANE:(t + 1) * LANE]
            tiles.append(xt * cos + pltpu.roll(xt, half, 1) * sp
                         + pltpu.roll(xt, LANE - half, 1) * sm)
        return jnp.concatenate(tiles, axis=1)

    for b in range(q_ref.shape[0]):
        q = rot(q_ref[b].astype(F32))
        k = rot(k_ref[b].astype(F32)) * (RET_QK_DIM ** -0.5)
        v = v_ref[b].astype(F32)
        kt = k.T.astype(BF16)
        state = state_ref[b]
        cross = jnp.dot(q.astype(BF16), state.astype(BF16),
                        preferred_element_type=F32) * qdec_ref[...]
        for h in range(RET_HEADS):
            cols = slice(h * RET_V_DIM, (h + 1) * RET_V_DIM)
            qm = (q * hmask_ref[h:h + 1, :]).astype(BF16)
            scores = jnp.dot(qm, kt, preferred_element_type=F32) * idec_ref[h]
            inner = jnp.dot(scores.astype(BF16), v[:, cols].astype(BF16),
                            preferred_element_type=F32)
            o = inner + cross[:, cols]
            o = o * lax.rsqrt(jnp.mean(o * o, axis=-1, keepdims=True) + NORM_EPS)
            gate = g_ref[b, :, cols].astype(F32)
            o_ref[b, :, cols] = (gate * jax.nn.sigmoid(gate) * o).astype(o_ref.dtype)
        vd = (v * kdec_ref[...]).astype(BF16)
        upd = jnp.dot(kt, vd, preferred_element_type=F32)
        state_ref[b] = state * cdec_ref[...] + upd * bmask_ref[...]


def _retention_tables(seq):
    h, c, dv, dk = RET_HEADS, RET_CHUNK, RET_V_DIM, RET_QK_DIM
    half = dk // 2
    inv = jnp.exp(-math.log(RET_ROPE_THETA) * jnp.arange(half, dtype=F32) * (2.0 / dk))
    ang = jnp.arange(seq, dtype=F32)[:, None] * inv[None, :]
    cos1, sin1, zero = jnp.cos(ang), jnp.sin(ang), jnp.zeros((seq, half), F32)
    reps = LANE // dk
    cos = jnp.tile(jnp.concatenate([cos1, cos1], axis=1), (1, reps))
    sp = jnp.tile(jnp.concatenate([zero, sin1], axis=1), (1, reps))
    sm = jnp.tile(jnp.concatenate([-sin1, zero], axis=1), (1, reps))
    log_gamma = jnp.log1p(-jnp.exp2(-5.0 - jnp.arange(h, dtype=F32)))
    idx = jnp.arange(c, dtype=F32)
    rel = idx[:, None] - idx[None, :]
    idec = jnp.where(rel[None] >= 0,
                     jnp.exp(jnp.maximum(rel, 0.0)[None] * log_gamma[:, None, None]), 0.0)
    qdec = jnp.exp((idx + 1.0)[:, None] * log_gamma[None, :])
    kdec = jnp.exp((c - 1.0 - idx)[:, None] * log_gamma[None, :])
    cdec = jnp.exp(c * log_gamma)[None, :]
    rep = lambda a: jnp.repeat(a, dv, axis=1)
    lane_head = np.arange(h * dk) // dk
    hmask = jnp.asarray(lane_head[None, :] == np.arange(h)[:, None], F32)
    bmask = jnp.asarray(lane_head[:, None] == (np.arange(h * dv) // dv)[None, :], F32)
    return cos, sp, sm, hmask, idec, rep(qdec), rep(kdec), rep(cdec), bmask


def retention(proj, batch, seq, tables, cast):
    c = RET_CHUNK
    nch = seq // c
    qw = RET_HEADS * RET_QK_DIM
    col = lambda name: _NAT[name][0] // _NAT[name][1]
    const2 = lambda n: (0, 0)
    cos, sp, sm, hmask, idec, qdec, kdec, cdec, bmask = tables
    tab = pl.BlockSpec((c, LANE), lambda n: (n, 0))
    cast_in, cast_out, cast_shape = _cast_specs(cast, (nch,))
    proj3 = proj.reshape(batch, seq, proj.shape[1])
    seg = lambda name, width: pl.BlockSpec((batch, c, width), lambda n: (0, n, col(name)))
    out, w_cast = pl.pallas_call(
        _retention_kernel,
        grid=(nch,),
        in_specs=[cast_in, seg("rq", qw), seg("rk", qw), seg("rv", RET_OUT), seg("rg", RET_OUT),
                  tab, tab, tab,
                  pl.BlockSpec(hmask.shape, const2),
                  pl.BlockSpec(idec.shape, lambda n: (0, 0, 0)),
                  pl.BlockSpec(qdec.shape, const2),
                  pl.BlockSpec(kdec.shape, const2),
                  pl.BlockSpec(cdec.shape, const2),
                  pl.BlockSpec(bmask.shape, const2)],
        out_specs=[pl.BlockSpec((batch, c, RET_OUT), lambda n: (0, n, 0)), cast_out],
        out_shape=[jax.ShapeDtypeStruct((batch, seq, RET_OUT), BF16), cast_shape],
        scratch_shapes=[pltpu.VMEM((batch, qw, RET_OUT), F32)],
        compiler_params=_cparams(("arbitrary",)),
        name="retention",
    )(cast[0], proj3, proj3, proj3, proj3, cos, sp, sm, hmask, idec, qdec, kdec, cdec, bmask)
    return out.reshape(batch * seq, RET_OUT), w_cast


LOG2E = math.log2(math.e)
M_INIT = float(np.finfo(np.float32).min)


def _flash_update(s, v_c, m_ref, l_ref, acc_ref, h, rows=slice(None)):
    reps = s.shape[1] // LANE

    def fold(x, op):
        tiles = [x[:, r * LANE:(r + 1) * LANE] for r in range(reps)]
        while len(tiles) > 1:
            tiles = [op(a, b) for a, b in zip(tiles[::2], tiles[1::2])]
        return tiles[0]

    m_prev = m_ref[h, rows, :]
    m_new = jnp.maximum(m_prev, jnp.max(fold(s, jnp.maximum), axis=1, keepdims=True))
    p = jnp.exp2(s - jnp.tile(m_new, (1, reps)))
    alpha = jnp.exp2(m_prev - m_new)
    l_ref[h, rows, :] = alpha * l_ref[h, rows, :] + jnp.sum(fold(p, jnp.add), axis=1,
                                                            keepdims=True)
    m_ref[h, rows, :] = m_new
    acc_ref[h, rows, :] = acc_ref[h, rows, :] * alpha + jnp.dot(p.astype(BF16), v_c,
                                                                preferred_element_type=F32)


INT_MIN = -2 ** 31
ROW_BLOCK = 64
PASSES_PER_CHECK = 4
COUNT_ROWS = 32
KV_PER_LOOP = 4


def _dsa_kernel(*refs, tq, topk, nqi):
    qi_refs = refs[:nqi]
    (w_ref, kidx_ref, q_ref, k_ref, v_ref, o_ref,
     wb_ref, lg_ref, key_ref, bias_ref, qs_ref, m_ref, l_ref, acc_ref) = refs[nqi:]
    tk = tq
    i = pl.program_id(1)
    nchunks = i + 1
    reps = tk // LANE
    group = DSA_HEADS // DSA_KV_HEADS
    d = DSA_HEAD_DIM
    heads_per_ref = IDX_HEADS // nqi

    w = w_ref[...].astype(F32) * (IDX_HEADS ** -0.5)
    for h in range(IDX_HEADS):
        wb_ref[h] = jnp.broadcast_to(w[:, h:h + 1], (tq, LANE))

    def score_chunk(c, diagonal):
        r0 = pl.multiple_of(c * tk, tk)
        kc = kidx_ref[pl.ds(r0, tk), :]
        for h in range(IDX_HEADS):
            r, t = divmod(h, heads_per_ref)
            lg_ref[h] = lax.dot_general(qi_refs[r][:, t * LANE:(t + 1) * LANE], kc,
                                        (((1,), (1,)), ((), ())), preferred_element_type=F32)
        for rb in range(tq // ROW_BLOCK):
            q0 = rb * ROW_BLOCK
            rows = slice(q0, q0 + ROW_BLOCK)
            acc = jnp.zeros((ROW_BLOCK, tk), F32)
            for h in range(IDX_HEADS):
                acc = acc + jnp.maximum(lg_ref[h, rows, :], 0.0) * jnp.tile(wb_ref[h, rows, :],
                                                                            (1, reps))
            bits = pltpu.bitcast(acc, jnp.int32)
            key = bits ^ (lax.shift_right_arithmetic(bits, 31) & 0x7FFFFFFF)
            if diagonal:
                kpos = lax.broadcasted_iota(jnp.int32, (ROW_BLOCK, tk), 1)
                qpos = q0 + lax.broadcasted_iota(jnp.int32, (ROW_BLOCK, tk), 0)
                key = jnp.where(kpos <= qpos, key, INT_MIN)
            key_ref[c, rows, :] = key
        bias_ref[c] = pltpu.bitcast(key_ref[c].T, F32)

    def score_body(c, _):
        score_chunk(c, False)
        return 0

    lax.fori_loop(0, i, score_body, 0)
    score_chunk(i, True)

    def count_ge(cand):
        def body(c, cnt):
            ge = (pltpu.bitcast(bias_ref[c], jnp.int32) >= cand).astype(jnp.int32)
            return cnt + jnp.sum(ge.reshape(tk // COUNT_ROWS, COUNT_ROWS, tq), axis=0)
        cnt = lax.fori_loop(0, nchunks, body, jnp.zeros((COUNT_ROWS, tq), jnp.int32))
        return jnp.sum(cnt, axis=0, keepdims=True)

    def bit_cond(state):
        b, _, _, excess = state
        return jnp.logical_and(b < 32, excess > 0)

    def bit_body(state):
        b, t, cnt, _ = state
        for u in range(PASSES_PER_CHECK):
            cand = t ^ lax.shift_left(jnp.int32(1), 31 - (b + u))
            n = count_ge(cand)
            take = n >= topk
            t, cnt = jnp.where(take, cand, t), jnp.where(take, n, cnt)
        return b + PASSES_PER_CHECK, t, cnt, jnp.max(cnt) - topk

    t0 = jnp.full((1, tq), INT_MIN, jnp.int32)
    total = nchunks * tk
    c0 = jnp.full((1, tq), total, jnp.int32)
    _, thr_row, _, excess = lax.while_loop(bit_cond, bit_body,
                                           (jnp.int32(0), t0, c0, total - topk))
    ties = excess > 0

    @pl.when(jnp.logical_not(ties))
    def _():
        thr_min = jnp.maximum(thr_row, INT_MIN + 1)
        thr = jnp.broadcast_to(thr_min, (LANE, tq)).T
        thr = jnp.tile(thr, (1, reps))

        def bias_body(c, _):
            bias_ref[c] = jnp.where(key_ref[c] >= thr, 0.0, -jnp.inf)
            return 0

        lax.fori_loop(0, nchunks, bias_body, 0)

    @pl.when(ties)
    def _():
        def above_body(c, acc):
            gt = (pltpu.bitcast(bias_ref[c], jnp.int32) > thr_row).astype(jnp.int32)
            return acc + jnp.sum(gt.reshape(tk // COUNT_ROWS, COUNT_ROWS, tq), axis=0)
        above = lax.fori_loop(0, nchunks, above_body, jnp.zeros((COUNT_ROWS, tq), jnp.int32))
        need = (topk - jnp.sum(above, axis=0, keepdims=True)).astype(F32)
        lower = (lax.broadcasted_iota(jnp.int32, (tk, tk), 1)
                 < lax.broadcasted_iota(jnp.int32, (tk, tk), 0)).astype(BF16)

        def tie_body(c, before):
            kt = pltpu.bitcast(bias_ref[c], jnp.int32)
            eq = kt == thr_row
            eq_b = jnp.where(eq, 1.0, 0.0).astype(BF16)
            rank = jnp.dot(lower, eq_b, preferred_element_type=F32) + before
            sel = jnp.logical_or(kt > thr_row, jnp.logical_and(eq, rank < need))
            sel = jnp.logical_and(sel, kt > INT_MIN)
            bias_ref[c] = jnp.where(sel, 0.0, -jnp.inf).T
            return before + jnp.sum(eq_b.astype(F32), axis=0, keepdims=True)

        lax.fori_loop(0, nchunks, tie_body, jnp.zeros((1, tq), F32))

    @pl.when(nchunks < key_ref.shape[0])
    def _():
        bias_ref[nchunks] = jnp.full((tq, tk), -jnp.inf, F32)

    hpl = KV_PER_LOOP * group
    for g0 in range(0, DSA_KV_HEADS, KV_PER_LOOP):
        for hh in range(hpl):
            h = g0 * group + hh
            qs_ref[hh * tq:(hh + 1) * tq, :] = q_ref[:, h * d:(h + 1) * d]
        m_ref[...] = jnp.full_like(m_ref, M_INIT)
        l_ref[...] = jnp.zeros_like(l_ref)
        acc_ref[...] = jnp.zeros_like(acc_ref)

        def att_body(c2, _, g0=g0):
            r0 = pl.multiple_of(c2 * 2 * tk, 2 * tk)
            bias = jnp.concatenate([bias_ref[2 * c2], bias_ref[2 * c2 + 1]], axis=1)
            for gg in range(KV_PER_LOOP):
                g = g0 + gg
                kc = k_ref[pl.ds(r0, 2 * tk), g * d:(g + 1) * d]
                vc = v_ref[pl.ds(r0, 2 * tk), g * d:(g + 1) * d]
                s_all = lax.dot_general(qs_ref[gg * group * tq:(gg + 1) * group * tq, :], kc,
                                        (((1,), (1,)), ((), ())), preferred_element_type=F32)
                _flash_update(s_all + jnp.tile(bias, (group, 1)), vc, m_ref, l_ref, acc_ref, gg)
            return 0

        lax.fori_loop(0, (nchunks + 1) // 2, att_body, 0)
        for hh in range(hpl):
            h = g0 * group + hh
            gg, rows = hh // group, slice((hh % group) * tq, (hh % group + 1) * tq)
            o_ref[:, h * d:(h + 1) * d] = (acc_ref[gg, rows, :]
                                           / l_ref[gg, rows, :]).astype(o_ref.dtype)


def dsa_attention(proj, tail, batch, seq, *, tq=256):
    nq = seq // tq
    assert nq % 2 == 0
    topk = min(DSA_TOPK, seq // 4)
    hpl = KV_PER_LOOP * DSA_HEADS // DSA_KV_HEADS
    gtq = DSA_HEADS // DSA_KV_HEADS * tq
    kvw = DSA_KV_HEADS * DSA_HEAD_DIM
    iqw = IDX_HEADS * IDX_HEAD_DIM
    nqi = iqw // QI_BLOCK
    col = lambda name, width: _NAT[name][0] // width
    qi_specs = [pl.BlockSpec((tq, QI_BLOCK), lambda b, i, r=r: (b * nq + i, col("iq", QI_BLOCK) + r))
                for r in range(nqi)]
    return pl.pallas_call(
        functools.partial(_dsa_kernel, tq=tq, topk=topk, nqi=nqi),
        grid=(batch, nq),
        in_specs=qi_specs + [
            pl.BlockSpec((tq, LANE), lambda b, i: (b * nq + i, _TAIL["iw"][0] // LANE)),
            pl.BlockSpec((seq, IDX_HEAD_DIM), lambda b, i: (b, col("ik", IDX_HEAD_DIM))),
            pl.BlockSpec((tq, DSA_OUT), lambda b, i: (b * nq + i, col("dq", DSA_OUT))),
            pl.BlockSpec((seq, kvw), lambda b, i: (b, col("dk", kvw))),
            pl.BlockSpec((seq, kvw), lambda b, i: (b, col("dv", kvw)))],
        out_specs=pl.BlockSpec((tq, DSA_OUT), lambda b, i: (b * nq + i, 0)),
        out_shape=jax.ShapeDtypeStruct((batch * seq, DSA_OUT), BF16),
        scratch_shapes=[pltpu.VMEM((IDX_HEADS, tq, LANE), F32),
                        pltpu.VMEM((IDX_HEADS, tq, tq), F32),
                        pltpu.VMEM((nq, tq, tq), jnp.int32),
                        pltpu.VMEM((nq, tq, tq), F32),
                        pltpu.VMEM((hpl * tq, DSA_HEAD_DIM), BF16),
                        pltpu.VMEM((KV_PER_LOOP, gtq, LANE), F32),
                        pltpu.VMEM((KV_PER_LOOP, gtq, LANE), F32),
                        pltpu.VMEM((KV_PER_LOOP, gtq, DSA_HEAD_DIM), F32)],
        compiler_params=_cparams(("parallel", "arbitrary")),
        name="dsa_attention",
    )(*([proj] * nqi), tail, proj, proj, proj, proj)


def _mla_kernel(cast_in_ref, q_ref, kn_ref, kpe_ref, v_ref, o_ref, cast_out_ref, m_ref, l_ref,
                acc_ref, *, tq, hb):
    tk = tq
    i = pl.program_id(2)
    cast_out_ref[...] = cast_in_ref[...].astype(BF16)
    m_ref[...] = jnp.full_like(m_ref, M_INIT)
    l_ref[...] = jnp.zeros_like(l_ref)
    acc_ref[...] = jnp.zeros_like(acc_ref)
    half = tq // 2

    def scores(h, rows, r0, nkeys):
        cols = slice(h * MLA_NOPE_DIM, (h + 1) * MLA_NOPE_DIM)
        kf = jnp.concatenate([kn_ref[pl.ds(r0, nkeys), cols], kpe_ref[pl.ds(r0, nkeys), :]],
                             axis=1)
        s = lax.dot_general(q_ref[rows, h * MLA_QK_PAD:(h + 1) * MLA_QK_PAD], kf,
                            (((1,), (1,)), ((), ())), preferred_element_type=F32)
        return s, v_ref[pl.ds(r0, nkeys), cols]

    def full_step(r0, nkeys):
        for h in range(hb):
            s, vc = scores(h, slice(None), r0, nkeys)
            _flash_update(s, vc, m_ref, l_ref, acc_ref, h)

    def pair_body(c2, _):
        full_step(pl.multiple_of(c2 * 2 * tk, 2 * tk), 2 * tk)
        return 0

    lax.fori_loop(0, i // 2, pair_body, 0)

    @pl.when(i % 2 == 1)
    def _():
        full_step(pl.multiple_of((i - 1) * tk, tk), tk)

    r0 = pl.multiple_of(i * tk, tk)
    for h in range(hb):
        s, vc = scores(h, slice(None), r0, half)
        keep = (lax.broadcasted_iota(jnp.int32, (tq, half), 1)
                <= lax.broadcasted_iota(jnp.int32, (tq, half), 0))
        _flash_update(jnp.where(keep, s, -jnp.inf), vc, m_ref, l_ref, acc_ref, h)
    for h in range(hb):
        s, vc = scores(h, slice(half, tq), r0 + half, half)
        keep = (lax.broadcasted_iota(jnp.int32, (half, half), 1)
                <= lax.broadcasted_iota(jnp.int32, (half, half), 0))
        _flash_update(jnp.where(keep, s, -jnp.inf), vc, m_ref, l_ref, acc_ref, h,
                      rows=slice(half, tq))
    for h in range(hb):
        o_ref[:, h * MLA_V_DIM:(h + 1) * MLA_V_DIM] = (acc_ref[h] / l_ref[h]).astype(o_ref.dtype)


def mla_attention(q, kv, tail, batch, seq, cast, *, tq=512, hb=4):
    nq = seq // tq
    ng = MLA_HEADS // hb
    cast_in, cast_out, cast_shape = _cast_specs(cast, (batch, ng, nq))
    return pl.pallas_call(
        functools.partial(_mla_kernel, tq=tq, hb=hb),
        grid=(batch, ng, nq),
        in_specs=[cast_in,
                  pl.BlockSpec((tq, hb * MLA_QK_PAD), lambda b, g, i: (b * nq + i, g)),
                  pl.BlockSpec((seq, hb * MLA_NOPE_DIM), lambda b, g, i: (b, g)),
                  pl.BlockSpec((seq, LANE), lambda b, g, i: (b, _TAIL["kr"][0] // LANE)),
                  pl.BlockSpec((seq, hb * MLA_V_DIM), lambda b, g, i: (b, ng + g))],
        out_specs=[pl.BlockSpec((tq, hb * MLA_V_DIM), lambda b, g, i: (b * nq + i, g)), cast_out],
        out_shape=[jax.ShapeDtypeStruct((batch * seq, MLA_OUT), BF16), cast_shape],
        scratch_shapes=[pltpu.VMEM((hb, tq, LANE), F32), pltpu.VMEM((hb, tq, LANE), F32),
                        pltpu.VMEM((hb, tq, MLA_V_DIM), F32)],
        compiler_params=_cparams(("arbitrary", "arbitrary", "arbitrary")),
        name="mla_attention",
    )(cast[0], q, kv, tail, kv)


def _tail_w_in(wt):
    parts, pos = [], 0
    for name in sorted(_TAIL, key=lambda s: _TAIL[s][0]):
        off, width = _TAIL[name]
        assert off == pos
        a = wt[_NAT[name][0]:_NAT[name][0] + _NAT[name][1]]
        parts.append(a)
        if a.shape[0] < width:
            parts.append(jnp.zeros((width - a.shape[0], wt.shape[1]), wt.dtype))
        pos += width
    assert pos == TAIL_WIDTH
    return jnp.concatenate(parts, axis=0)


def _relayout_w_uq(w):
    dep, r = w.shape[:2]
    a = w.reshape(dep, r, MLA_HEADS, MLA_NOPE_DIM + MLA_ROPE_DIM)
    a = jnp.pad(a, ((0, 0), (0, 0), (0, 0), (0, MLA_QK_PAD - a.shape[3])))
    return a.reshape(dep, r, MLA_HEADS * MLA_QK_PAD).astype(BF16)


def _relayout_w_ukv(w):
    dep, r = w.shape[:2]
    a = w.reshape(dep, r, MLA_HEADS, 2, MLA_NOPE_DIM).transpose(0, 1, 3, 2, 4)
    return a.reshape(dep, r, 2 * MLA_HEADS * MLA_NOPE_DIM).astype(BF16)


def _main_tile_modes():
    modes = [(1.0, None)] * (MAIN_WIDTH // LANE)

    def fill(name, mode):
        off, width = _NAT[name]
        for t in range(off // LANE, (off + width) // LANE):
            modes[t] = mode

    fill("dq", (DSA_HEAD_DIM ** -0.5 * LOG2E, 0))
    fill("dk", (1.0, 0))
    fill("iq", (IDX_HEAD_DIM ** -0.5, 0))
    fill("ik", (1.0, 0))
    return modes


def _tail_tile_modes():
    modes = [(1.0, None)] * (TAIL_WIDTH // LANE)
    modes[_TAIL["kr"][0] // LANE] = (1.0, 0)
    return modes


def kernel(x, attn_norm, w_in, mla_q_norm, mla_kv_norm, w_uq, w_ukv, w_o, mlp_norm, w_up,
           w_down, final_norm):
    batch, seq, d_model = x.shape
    depth = w_in.shape[0]
    m = batch * seq

    ret_tables = _retention_tables(seq)
    part_tab = _rope_tables(seq, PARTIAL_ROPE_THETA, PARTIAL_ROT_DIM)
    mla_tab = _rope_tables(seq, MLA_ROPE_THETA, MLA_ROPE_DIM)
    mla_scale = (MLA_NOPE_DIM + MLA_ROPE_DIM) ** -0.5 * LOG2E
    uq_modes = [(mla_scale, None), (mla_scale, 0)] * MLA_HEADS

    w_uq_b, w_ukv_b = _relayout_w_uq(w_uq), _relayout_w_ukv(w_ukv)
    w_in_t = jnp.swapaxes(w_in, 1, 2)

    h = x.reshape(m, d_model)
    xin, w_in_l = rmsnorm(h, attn_norm[0], BF16, cast=(w_in_t, 0))
    for l in range(depth):
        pre = None if l == 0 else attn_norm[l]
        proj = matmul(xin, w_in_l, BF16, name="mm_in", w_t=True, n_out=MAIN_WIDTH, bn=MAIN_BN,
                      prenorm=pre, tile_modes=_main_tile_modes(), rope_tabs=(part_tab,), seq=seq)
        tail = matmul(xin, _tail_w_in(w_in_l), BF16, name="mm_in_tail", w_t=True, bn=TAIL_WIDTH,
                      prenorm=pre, tile_modes=_tail_tile_modes(), rope_tabs=(mla_tab,), seq=seq)

        ret, w_o_l = retention(proj, batch, seq, ret_tables, cast=(w_o, l))
        dsa = dsa_attention(proj, tail, batch, seq)

        q_full = matmul(tail, w_uq_b, BF16, layer=l, name="mm_uq", x_cols=_TAIL["cq"],
                        prenorm=mla_q_norm[l], tile_modes=uq_modes, rope_tabs=(mla_tab,), seq=seq,
                        cast=(w_in_t, l + 1) if l + 1 < depth else None)
        if l + 1 < depth:
            q_full, w_in_next = q_full
        kv_up = matmul(tail, w_ukv_b, BF16, layer=l, name="mm_ukv", x_cols=_TAIL["ckv"],
                       prenorm=mla_kv_norm[l])
        mla, w_up_l = mla_attention(q_full, kv_up, tail, batch, seq, cast=(w_up, l))

        h, hb = matmul((ret, dsa, mla), w_o_l, F32, res=h, name="mm_o", emit_bf16=True)
        up, w_down_l = matmul(hb, w_up_l, BF16, act="relu2", name="mm_up", prenorm=mlp_norm[l],
                              cast=(w_down, l))
        if l + 1 < depth:
            h, xin = matmul(up, w_down_l, F32, res=h, bk=DOWN_BK, name="mm_down",
                            emit_bf16=True)
            w_in_l = w_in_next
        else:
            h = matmul(up, w_down_l, F32, res=h, bk=DOWN_BK, name="mm_down")
    out = rmsnorm(h, final_norm, x.dtype)
    return out.reshape(batch, seq, d_model)
```
